```python
import jax, jax.numpy as jnp
from jax import lax
import numpy as np

D_MODEL = 1024
BATCH = 8
SEQ = 2048
DEPTH = 2
DEC_BATCH = 128
DEC_SEQ = 1
PAST_LEN = 8192
PAGE_SIZE = 128

N_PAIRS = DEPTH // 2
HEAD_DIM = 64
ROT_DIM = HEAD_DIM // 4
ROPE_THETA = 500000.0
D_FF = 4 * D_MODEL
EPS = 1e-6
BLOCK = 128

CONV_CH = D_MODEL // 2
CONV_WIDTH = 31
SC_CH = D_MODEL // 2
SC_WIDTH = 3
IN_COLS_CONV = 2 * CONV_CH + 3 * SC_CH
MIX_COLS_CONV = CONV_CH + SC_CH

SWA_HEADS = 8
SWA_KV = 2
SWA_WINDOW = 128
DIL_HEADS = 4
DIL_KV = 1
DIL_PATTERNS = ((128, 1), (512, 4), (2048, 16))
IN_COLS_ATTN = (SWA_HEADS + 2 * SWA_KV) * HEAD_DIM + len(DIL_PATTERNS) * (DIL_HEADS + 2 * DIL_KV) * HEAD_DIM
MIX_COLS_ATTN = (SWA_HEADS + DIL_HEADS) * HEAD_DIM

kernel_name = 'hybrid_conv_swa_dilated_decode_step'


def rmsnorm(x, g):
    xf = x.astype(jnp.float32)
    y = xf * lax.rsqrt(jnp.mean(xf * xf, -1, keepdims=True) + EPS)
    return (y * g.astype(jnp.float32)).astype(x.dtype)


def layernorm(x, g, b):
    xf = x.astype(jnp.float32)
    mu = jnp.mean(xf, -1, keepdims=True)
    var = jnp.mean(jnp.square(xf - mu), -1, keepdims=True)
    y = (xf - mu) * lax.rsqrt(var + EPS) * g.astype(jnp.float32) + b.astype(jnp.float32)
    return y.astype(x.dtype)


def split_cols(u, sizes):
    out, off = [], 0
    for s in sizes:
        out.append(u[..., off:off + s])
        off += s
    return out


def rope(x, pos):
    half = ROT_DIM // 2
    inv = ROPE_THETA ** (-jnp.arange(half, dtype=jnp.float32) / half)
    ang = pos.astype(jnp.float32)[:, None] * inv[None, :]
    cos, sin = jnp.cos(ang)[:, None, :], jnp.sin(ang)[:, None, :]
    xf = x.astype(jnp.float32)
    x1, x2 = xf[..., :half], xf[..., half:ROT_DIM]
    y = jnp.concatenate([x1 * cos - x2 * sin, x2 * cos + x1 * sin, xf[..., ROT_DIM:]], -1)
    return y.astype(x.dtype)


def dwconv_valid(x, w):
    return lax.conv_general_dilated(x, w.astype(x.dtype)[:, None, :], window_strides=(1,), padding='VALID',
                                    dimension_numbers=('NWC', 'WIO', 'NWC'), feature_group_count=x.shape[-1])


def sqrelu_mlp(h, w1, w2):
    return jnp.square(jax.nn.relu(h @ w1)) @ w2


def attend(q, k, v, mask, sink=None):
    s = jnp.einsum('...qhgd,...khd->...hgqk', q.astype(jnp.float32), k.astype(jnp.float32)) * (HEAD_DIM ** -0.5)
    s = jnp.where(mask, s, -jnp.inf)
    m = jnp.max(s, -1)
    if sink is not None:
        sk = sink.astype(jnp.float32).reshape(q.shape[-3], q.shape[-2], 1)
        m = jnp.maximum(m, sk)
    p = jnp.exp(s - m[..., None])
    den = jnp.sum(p, -1)
    if sink is not None:
        den = den + jnp.exp(sk - m)
    o = jnp.einsum('...hgqk,...khd->...qhgd', p, v.astype(jnp.float32))
    o = o / jnp.moveaxis(den, -1, -3)[..., None]
    lse = jnp.moveaxis(m + jnp.log(den), -1, -3)
    return o.astype(q.dtype), lse


def banded_attend(q, k, v, window, sink=None):
    b, n = q.shape[:2]
    blk = min(BLOCK, n)
    nb = -(-n // blk)
    npad = nb * blk - n
    nprev = -(-window // blk)
    qb = jnp.pad(q, [(0, 0), (0, npad)] + [(0, 0)] * (q.ndim - 2)).reshape((b, nb, blk) + q.shape[2:])

    def windows(t):
        tp = jnp.pad(t, [(0, 0), (nprev * blk, npad)] + [(0, 0)] * (t.ndim - 2))
        return jnp.concatenate([tp[:, j * blk: j * blk + nb * blk].reshape((b, nb, blk) + t.shape[2:])
                                for j in range(nprev + 1)], axis=2)

    kw, vw = windows(k), windows(v)
    qi = jnp.arange(nb)[:, None] * blk + jnp.arange(blk)[None, :]
    ki = (jnp.arange(nb)[:, None] - nprev) * blk + jnp.arange((nprev + 1) * blk)[None, :]
    dist = qi[:, :, None] - ki[:, None, :]
    mask = (ki[:, None, :] >= 0) & (dist >= 0) & (dist <= window)
    o, lse = attend(qb, kw, vw, mask[:, None, None], sink)
    o = o.reshape((b, nb * blk) + q.shape[2:])[:, :n]
    lse = lse.reshape((b, nb * blk) + q.shape[2:4])[:, :n]
    return o, lse


def dilated_prompt(q, k, v, window, dil):
    b, s = q.shape[:2]

    def split(t):
        return jnp.moveaxis(t.reshape((b, s // dil, dil) + t.shape[2:]), 2, 1).reshape((b * dil, s // dil) + t.shape[2:])

    def merge(t):
        return jnp.moveaxis(t.reshape((b, dil, s // dil) + t.shape[2:]), 1, 2).reshape((b, s) + t.shape[2:])

    o, lse = banded_attend(split(q), split(k), split(v), window // dil)
    return merge(o), merge(lse)


def dilated_sample(q, kc, vc, window, dil, buf_len):
    ds = q.shape[1]
    steps = jnp.arange(window // dil + 1)
    idx = buf_len + jnp.arange(ds)[:, None] - steps[None, :] * dil
    valid = idx >= 0
    idx = jnp.maximum(idx, 0)
    kg = jnp.take(kc, idx, axis=1)
    vg = jnp.take(vc, idx, axis=1)
    o, lse = attend(q[:, :, None], kg, vg, valid[:, None, None, None, :])
    return o[:, :, 0], lse[:, :, 0]


def window_sample(q, kc, vc, window, sink):
    ds = q.shape[1]
    kpos = jnp.arange(kc.shape[1])
    qpos = kc.shape[1] - ds + jnp.arange(ds)
    dist = qpos[:, None] - kpos[None, :]
    mask = (dist >= 0) & (dist <= window)
    return attend(q, kc, vc, mask, sink)


def conv_mixers(h, buf_a, buf_b, w_in, a_w, a_b, a_ln_g, a_ln_b, b_w, w_out):
    a_val, a_gate, b_x, b_gb, b_gc = split_cols(h @ w_in, [CONV_CH, CONV_CH, SC_CH, SC_CH, SC_CH])
    ga = jnp.concatenate([buf_a, a_val * jax.nn.sigmoid(a_gate)], 1)
    ya = jax.nn.silu(layernorm(dwconv_valid(ga, a_w) + a_b, a_ln_g, a_ln_b))
    zb = jnp.concatenate([buf_b, b_gc * b_x], 1)
    yb = b_gb * dwconv_valid(zb, b_w)
    y = jnp.concatenate([ya, yb], -1) @ w_out
    return y, ga[:, -(CONV_WIDTH - 1):], zb[:, -(SC_WIDTH - 1):]


def attn_qkv(h, pos, w_in):
    b, t, _ = h.shape
    sizes = [SWA_HEADS * HEAD_DIM, SWA_KV * HEAD_DIM, SWA_KV * HEAD_DIM] + \
            [DIL_HEADS * HEAD_DIM, DIL_KV * HEAD_DIM, DIL_KV * HEAD_DIM] * len(DIL_PATTERNS)
    parts = split_cols(h @ w_in, sizes)

    def qkv(pq, pk, pv, nq, nkv):
        q = rope(pq.reshape(b, t, nq, HEAD_DIM), pos).reshape(b, t, nkv, nq // nkv, HEAD_DIM)
        k = rope(pk.reshape(b, t, nkv, HEAD_DIM), pos)
        v = pv.reshape(b, t, nkv, HEAD_DIM)
        return q, k, v

    swa = qkv(parts[0], parts[1], parts[2], SWA_HEADS, SWA_KV)
    dil = [qkv(parts[3 + 3 * i], parts[4 + 3 * i], parts[5 + 3 * i], DIL_HEADS, DIL_KV) for i in range(len(DIL_PATTERNS))]
    return swa, dil


def attn_merge(o_swa, dil_outs, w_out):
    b, t = o_swa.shape[:2]
    wts = jax.nn.softmax(jnp.stack([l for _, l in dil_outs]), axis=0)[..., None]
    o_dil = jnp.sum(wts * jnp.stack([o.astype(jnp.float32) for o, _ in dil_outs]), axis=0)
    mix = jnp.concatenate([o_swa.reshape(b, t, -1), o_dil.astype(o_swa.dtype).reshape(b, t, -1)], -1)
    return mix @ w_out


def attn_prompt(h, w_in, sinks, w_out):
    t = h.shape[1]
    (q, k, v), dil = attn_qkv(h, jnp.arange(t), w_in)
    o_swa, _ = banded_attend(q, k, v, SWA_WINDOW, sinks)
    dil_outs = [dilated_prompt(dq, dk, dv, w, d) for (dq, dk, dv), (w, d) in zip(dil, DIL_PATTERNS)]
    y = attn_merge(o_swa, dil_outs, w_out)
    swa_state = jnp.stack([k, v], 2)[:, -min(SWA_WINDOW, t):]
    dil_states = [jnp.stack([dk, dv], 2)[:, -min(w, t):] for (_, dk, dv), (w, _) in zip(dil, DIL_PATTERNS)]
    return y, swa_state, dil_states


def attn_sample(h, swa_cache, dil_caches, w_in, sinks, w_out):
    ds = h.shape[1]
    (q, k, v), dil = attn_qkv(h, PAST_LEN + jnp.arange(ds), w_in)
    cat = jnp.concatenate([swa_cache, jnp.stack([k, v], 2)], 1)
    o_swa, _ = window_sample(q, cat[:, :, 0], cat[:, :, 1], SWA_WINDOW, sinks)
    swa_state = cat[:, -min(SWA_WINDOW, PAST_LEN + ds):]
    dil_outs, dil_states = [], []
    for (dq, dk, dv), (w, d), c in zip(dil, DIL_PATTERNS, dil_caches):
        cd = jnp.concatenate([c, jnp.stack([dk, dv], 2)], 1)
        dil_outs.append(dilated_sample(dq, cd[:, :, 0], cd[:, :, 1], w, d, c.shape[1]))
        dil_states.append(cd[:, -min(w, PAST_LEN + ds):])
    y = attn_merge(o_swa, dil_outs, w_out)
    return y, swa_state, dil_states


def setup_inputs(seed: int = 0) -> dict:
    key = jax.random.key(seed)
    keys = list(jax.random.split(key, 24))

    def nrm(shape, scale):
        return jax.random.normal(keys.pop(), shape, jnp.float32) * scale

    l_swa = min(SWA_WINDOW, PAST_LEN)
    l_dil = [min(w, PAST_LEN) for w, _ in DIL_PATTERNS]
    return {
        'x_prompt': nrm((BATCH, SEQ, D_MODEL), 1.0),
        'x_sample': nrm((DEC_BATCH, DEC_SEQ, D_MODEL), 1.0),
        'state_conv_a': nrm((N_PAIRS, DEC_BATCH, CONV_WIDTH - 1, CONV_CH), 0.5),
        'state_conv_b': nrm((N_PAIRS, DEC_BATCH, SC_WIDTH - 1, SC_CH), 0.5),
        'cache_swa_kv': nrm((N_PAIRS, DEC_BATCH, l_swa, 2, SWA_KV, HEAD_DIM), 1.0),
        'cache_dil0_kv': nrm((N_PAIRS, DEC_BATCH, l_dil[0], 2, DIL_KV, HEAD_DIM), 1.0),
        'cache_dil1_kv': nrm((N_PAIRS, DEC_BATCH, l_dil[1], 2, DIL_KV, HEAD_DIM), 1.0),
        'cache_dil2_kv': nrm((N_PAIRS, DEC_BATCH, l_dil[2], 2, DIL_KV, HEAD_DIM), 1.0),
        'norm_g': 1.0 + nrm((DEPTH, 4, D_MODEL), 0.05),
        'w_in_conv': nrm((N_PAIRS, D_MODEL, IN_COLS_CONV), D_MODEL ** -0.5),
        'conv_a_w': nrm((N_PAIRS, CONV_WIDTH, CONV_CH), CONV_WIDTH ** -0.5),
        'conv_a_b': nrm((N_PAIRS, CONV_CH), 0.02),
        'conv_a_ln_g': 1.0 + nrm((N_PAIRS, CONV_CH), 0.05),
        'conv_a_ln_b': nrm((N_PAIRS, CONV_CH), 0.02),
        'conv_b_w': nrm((N_PAIRS, SC_WIDTH, SC_CH), SC_WIDTH ** -0.5),
        'w_out_conv': nrm((N_PAIRS, MIX_COLS_CONV, D_MODEL), MIX_COLS_CONV ** -0.5),
        'w_in_attn': nrm((N_PAIRS, D_MODEL, IN_COLS_ATTN), D_MODEL ** -0.5),
        'attn_sinks': nrm((N_PAIRS, SWA_HEADS), 0.5),
        'w_out_attn': nrm((N_PAIRS, MIX_COLS_ATTN, D_MODEL), MIX_COLS_ATTN ** -0.5),
        'mlp_w1': nrm((DEPTH, D_MODEL, D_FF), D_MODEL ** -0.5),
        'mlp_w2': nrm((DEPTH, D_FF, D_MODEL), D_FF ** -0.5),
    }


def reference(x_prompt, x_sample, state_conv_a, state_conv_b, cache_swa_kv, cache_dil0_kv, cache_dil1_kv,
              cache_dil2_kv, norm_g, w_in_conv, conv_a_w, conv_a_b, conv_a_ln_g, conv_a_ln_b, conv_b_w,
              w_out_conv, w_in_attn, attn_sinks, w_out_attn, mlp_w1, mlp_w2):
    names = ('a_p', 'a_s', 'b_p', 'b_s', 'swa_p', 'swa_s', 'd0_p', 'd0_s', 'd1_p', 'd1_s', 'd2_p', 'd2_s')
    st = {n: [] for n in names}
    hp, hs = x_prompt, x_sample
    for layer in range(DEPTH):
        p = layer // 2
        g = norm_g[layer]
        up, us = rmsnorm(hp, g[0]), rmsnorm(hs, g[0])
        if layer % 2 == 0:
            cw = (w_in_conv[p], conv_a_w[p], conv_a_b[p], conv_a_ln_g[p], conv_a_ln_b[p], conv_b_w[p], w_out_conv[p])
            za = jnp.zeros((up.shape[0], CONV_WIDTH - 1, CONV_CH), up.dtype)
            zb = jnp.zeros((up.shape[0], SC_WIDTH - 1, SC_CH), up.dtype)
            yp, a_p, b_p = conv_mixers(up, za, zb, *cw)
            ys, a_s, b_s = conv_mixers(us, state_conv_a[p], state_conv_b[p], *cw)
            st['a_p'].append(a_p); st['a_s'].append(a_s)
            st['b_p'].append(b_p); st['b_s'].append(b_s)
        else:
            yp, swa_p, dil_p = attn_prompt(up, w_in_attn[p], attn_sinks[p], w_out_attn[p])
            ys, swa_s, dil_s = attn_sample(us, cache_swa_kv[p], (cache_dil0_kv[p], cache_dil1_kv[p], cache_dil2_kv[p]),
                                           w_in_attn[p], attn_sinks[p], w_out_attn[p])
            st['swa_p'].append(swa_p); st['swa_s'].append(swa_s)
            for i in range(len(DIL_PATTERNS)):
                st['d%d_p' % i].append(dil_p[i]); st['d%d_s' % i].append(dil_s[i])
        hp = hp + rmsnorm(yp, g[1])
        hs = hs + rmsnorm(ys, g[1])
        hp = hp + rmsnorm(sqrelu_mlp(rmsnorm(hp, g[2]), mlp_w1[layer], mlp_w2[layer]), g[3])
        hs = hs + rmsnorm(sqrelu_mlp(rmsnorm(hs, g[2]), mlp_w1[layer], mlp_w2[layer]), g[3])
    return (hp, hs,
            jnp.stack(st['a_p']), jnp.stack(st['a_s']),
            jnp.stack(st['b_p']), jnp.stack(st['b_s']),
            jnp.stack(st['swa_p']), jnp.stack(st['swa_s']),
            jnp.stack(st['d0_p']), jnp.stack(st['d0_s']),
            jnp.stack(st['d1_p']), jnp.stack(st['d1_s']),
            jnp.stack(st['d2_p']), jnp.stack(st['d2_s']))
```

```python
import functools

import jax
import jax.numpy as jnp
from jax import lax
from jax.experimental import pallas as pl
from jax.experimental.pallas import tpu as pltpu

F32 = jnp.float32
BF16 = jnp.bfloat16

D_MODEL = 1024
BATCH = 8
SEQ = 2048
DEC_BATCH = 128
PAST_LEN = 8192
HEAD_DIM = 64
ROT_HALF = 8
ROPE_THETA = 500000.0
D_FF = 4 * D_MODEL
EPS = 1e-6
CONV_CH = 512
CONV_WIDTH = 31
SC_WIDTH = 3
SWA_HEADS = 8
SWA_KV = 2
DIL_HEADS = 4
DIL_PATTERNS = ((128, 1), (512, 4), (2048, 16))
Q_COLS = (SWA_HEADS + 3 * DIL_HEADS) * HEAD_DIM
KV_COLS = 2 * SWA_KV * HEAD_DIM + 3 * 2 * HEAD_DIM
ATTN_COLS = Q_COLS + KV_COLS
MIX_ATTN = (SWA_HEADS + DIL_HEADS) * HEAD_DIM
LANES = 128
BLK = 128

V7X_VMEM_BYTES = 64 * 1024 * 1024
VMEM_LIMIT = V7X_VMEM_BYTES - 8 * 1024 * 1024


def _params(*sem):
    return pltpu.CompilerParams(dimension_semantics=sem, vmem_limit_bytes=VMEM_LIMIT)


def _const_spec(shape):
    zeros = (0,) * len(shape)
    return pl.BlockSpec(shape, lambda *_: zeros)


def _rmsnorm(x, g):
    return x * lax.rsqrt(jnp.mean(x * x, axis=-1, keepdims=True) + EPS) * g


def _conv_in_body(x_ref, g_ref, w_ref, ga_ref, zb_ref, gb_ref):
    u = _rmsnorm(x_ref[...], g_ref[...]).astype(BF16)
    z = jnp.dot(u, w_ref[...], preferred_element_type=F32)
    c = CONV_CH
    ga_ref[...] = z[:, 0:c] * jax.nn.sigmoid(z[:, c:2 * c])
    zb_ref[...] = z[:, 4 * c:5 * c] * z[:, 2 * c:3 * c]
    gb_ref[...] = z[:, 3 * c:4 * c]


def _conv_in(x, g, w, tm):
    t = x.shape[0]
    row = lambda i: (i, 0)
    out = jax.ShapeDtypeStruct((t, CONV_CH), F32)
    return pl.pallas_call(
        _conv_in_body,
        grid=(t // tm,),
        in_specs=[pl.BlockSpec((tm, D_MODEL), row), _const_spec((1, D_MODEL)),
                  _const_spec((D_MODEL, 5 * CONV_CH))],
        out_specs=[pl.BlockSpec((tm, CONV_CH), row)] * 3,
        out_shape=[out] * 3,
        compiler_params=_params("arbitrary"),
        name="conv_in",
    )(x, g, w)


HALO_A = 32
HALO_B = 8
CONV_CHUNK = 32


def _layernorm_silu(c, g, b):
    mu = jnp.mean(c, axis=-1, keepdims=True)
    d = c - mu
    var = jnp.mean(d * d, axis=-1, keepdims=True)
    y = d * lax.rsqrt(var + EPS) * g + b
    return y * jax.nn.sigmoid(y)


def _conv_mix_body(ga_ref, zb_ref, gb_ref, h_ref, aw_ref, ab_ref, lng_ref, lnb_ref, bw_ref,
                   w_ref, g_ref, o_ref, ext_a, ext_b, conv_scr, *, tt):
    t = pl.program_id(1)

    @pl.when(t == 0)
    def _():
        ext_a[0:HALO_A, :] = jnp.zeros((HALO_A, CONV_CH), F32)
        ext_b[0:HALO_B, :] = jnp.zeros((HALO_B, CONV_CH), F32)

    @pl.when(t > 0)
    def _():
        ext_a[0:HALO_A, :] = ext_a[tt:tt + HALO_A, :]
        ext_b[0:HALO_B, :] = ext_b[tt:tt + HALO_B, :]

    ext_a[HALO_A:HALO_A + tt, :] = ga_ref[...]
    ext_b[HALO_B:HALO_B + tt, :] = zb_ref[...]

    off_a = HALO_A - (CONV_WIDTH - 1)

    for base in range(0, tt, CONV_CHUNK):
        acc = jnp.zeros((CONV_CHUNK, CONV_CH), F32)
        for j in range(CONV_WIDTH):
            acc = acc + aw_ref[j:j + 1, :] * ext_a[base + off_a + j:base + off_a + j + CONV_CHUNK, :]
        conv_scr[base:base + CONV_CHUNK, :] = acc

    ya = _layernorm_silu(conv_scr[...] + ab_ref[...], lng_ref[...], lnb_ref[...])
    off_b = HALO_B - (SC_WIDTH - 1)
    cb = bw_ref[0:1, :] * ext_b[off_b:off_b + tt, :]
    for j in range(1, SC_WIDTH):
        cb = cb + bw_ref[j:j + 1, :] * ext_b[off_b + j:off_b + j + tt, :]
    yb = gb_ref[...] * cb
    mix = jnp.concatenate([ya, yb], axis=-1).astype(BF16)
    y = jnp.dot(mix, w_ref[...], preferred_element_type=F32)
    o_ref[...] = h_ref[...] + _rmsnorm(y, g_ref[...])


def _conv_mix(ga, zb, gb, h, aw, ab, lng, lnb, bw, w, g, tt):
    nt = SEQ // tt
    row = lambda b, t: (b * nt + t, 0)
    act = pl.BlockSpec((tt, CONV_CH), row)
    return pl.pallas_call(
        functools.partial(_conv_mix_body, tt=tt),
        grid=(BATCH, nt),
        in_specs=[act, act, act, pl.BlockSpec((tt, D_MODEL), row),
                  _const_spec((CONV_WIDTH, CONV_CH)), _const_spec((1, CONV_CH)),
                  _const_spec((1, CONV_CH)), _const_spec((1, CONV_CH)),
                  _const_spec((SC_WIDTH, CONV_CH)), _const_spec((D_MODEL, D_MODEL)),
                  _const_spec((1, D_MODEL))],
        out_specs=pl.BlockSpec((tt, D_MODEL), row),
        out_shape=jax.ShapeDtypeStruct(h.shape, F32),
        scratch_shapes=[pltpu.VMEM((HALO_A + tt, CONV_CH), F32),
                        pltpu.VMEM((HALO_B + tt, CONV_CH), F32),
                        pltpu.VMEM((tt, CONV_CH), F32)],
        compiler_params=_params("arbitrary", "arbitrary"),
        name="conv_mix",
    )(ga, zb, gb, h, aw, ab, lng, lnb, bw, w, g)


def _conv_mix_dec_body(ga_ref, zb_ref, gb_ref, h_ref, sta_ref, sb0_ref, sb1_ref, aw_ref, ab_ref,
                       lng_ref, lnb_ref, bw_ref, w_ref, g_ref, o_ref, nsta_ref):
    ga = ga_ref[...]
    n_state = CONV_WIDTH - 1
    acc = aw_ref[n_state:n_state + 1, :] * ga
    for j in range(n_state):
        acc = acc + aw_ref[j:j + 1, :] * sta_ref[j]
    for j in range(n_state - 1):
        nsta_ref[j] = sta_ref[j + 1]
    nsta_ref[n_state - 1] = ga
    ya = _layernorm_silu(acc + ab_ref[...], lng_ref[...], lnb_ref[...])
    cb = bw_ref[0:1, :] * sb0_ref[...] + bw_ref[1:2, :] * sb1_ref[...] + bw_ref[2:3, :] * zb_ref[...]
    yb = gb_ref[...] * cb
    mix = jnp.concatenate([ya, yb], axis=-1).astype(BF16)
    y = jnp.dot(mix, w_ref[...], preferred_element_type=F32)
    o_ref[...] = h_ref[...] + _rmsnorm(y, g_ref[...])


def _conv_mix_dec(ga, zb, gb, h, sta, sb0, sb1, aw, ab, lng, lnb, bw, w, g):
    args = (ga, zb, gb, h, sta, sb0, sb1, aw, ab, lng, lnb, bw, w, g)
    return pl.pallas_call(
        _conv_mix_dec_body,
        grid=(1,),
        in_specs=[_const_spec(a.shape) for a in args],
        out_specs=[_const_spec(h.shape), _const_spec(sta.shape)],
        out_shape=[jax.ShapeDtypeStruct(h.shape, F32), jax.ShapeDtypeStruct(sta.shape, F32)],
        compiler_params=_params("arbitrary"),
        name="conv_mix_dec",
    )(*args)


FF_CHUNK = 1024


def _mlp_body(x_ref, g2_ref, w1_ref, w2_ref, g3_ref, o_ref):
    x = x_ref[...]
    u = _rmsnorm(x, g2_ref[...]).astype(BF16)
    acc = jnp.zeros(x.shape, F32)
    for c in range(D_FF // FF_CHUNK):
        sl = slice(c * FF_CHUNK, (c + 1) * FF_CHUNK)
        hid = jnp.dot(u, w1_ref[:, sl], preferred_element_type=F32)
        hid = jnp.square(jnp.maximum(hid, 0.0)).astype(BF16)
        acc = acc + jnp.dot(hid, w2_ref[sl, :], preferred_element_type=F32)
    o_ref[...] = x + _rmsnorm(acc, g3_ref[...])


def _mlp(x, g2, w1, w2, g3, tm):
    t = x.shape[0]
    row = lambda i: (i, 0)
    return pl.pallas_call(
        _mlp_body,
        grid=(t // tm,),
        in_specs=[pl.BlockSpec((tm, D_MODEL), row), _const_spec((1, D_MODEL)),
                  _const_spec((D_MODEL, D_FF)), _const_spec((D_FF, D_MODEL)),
                  _const_spec((1, D_MODEL))],
        out_specs=pl.BlockSpec((tm, D_MODEL), row),
        out_shape=jax.ShapeDtypeStruct(x.shape, F32),
        compiler_params=_params("arbitrary"),
        name="mlp",
    )(x, g2, w1, w2, g3)


def _rope_slab(z, c, s_lo, s_hi):
    return z * c + pltpu.roll(z, LANES - ROT_HALF, 1) * s_lo + pltpu.roll(z, ROT_HALF, 1) * s_hi


def _attn_in_body(x_ref, g_ref, w_ref, ca_ref, sla_ref, sha_ref, cb_ref, slb_ref, shb_ref,
                  q_ref, kv_ref):
    u = _rmsnorm(x_ref[...], g_ref[...]).astype(BF16)
    z = jnp.dot(u, w_ref[...], preferred_element_type=F32)
    ca, sla, sha = ca_ref[...], sla_ref[...], sha_ref[...]
    scale = HEAD_DIM ** -0.5
    for s in range(Q_COLS // LANES):
        sl = slice(s * LANES, (s + 1) * LANES)
        q_ref[:, sl] = (_rope_slab(z[:, sl], ca, sla, sha) * scale).astype(BF16)
    kv_ref[:, 0:LANES] = _rope_slab(z[:, Q_COLS:Q_COLS + LANES], ca, sla, sha)
    kv_ref[:, LANES:2 * LANES] = z[:, Q_COLS + LANES:Q_COLS + 2 * LANES]
    cb, slb, shb = cb_ref[...], slb_ref[...], shb_ref[...]
    for s in range(2, KV_COLS // LANES):
        sl = slice(Q_COLS + s * LANES, Q_COLS + (s + 1) * LANES)
        kv_ref[:, s * LANES:(s + 1) * LANES] = _rope_slab(z[:, sl], cb, slb, shb)


def _attn_in(x, g, w, tabs, tm):
    t = x.shape[0]
    row = lambda i: (i, 0)
    nper = SEQ // tm
    tab = pl.BlockSpec((tm, LANES), lambda i: (i % nper, 0))
    return pl.pallas_call(
        _attn_in_body,
        grid=(t // tm,),
        in_specs=[pl.BlockSpec((tm, D_MODEL), row), _const_spec((1, D_MODEL)),
                  _const_spec((D_MODEL, ATTN_COLS))] + [tab] * 6,
        out_specs=[pl.BlockSpec((tm, Q_COLS), row), pl.BlockSpec((tm, KV_COLS), row)],
        out_shape=[jax.ShapeDtypeStruct((t, Q_COLS), BF16), jax.ShapeDtypeStruct((t, KV_COLS), F32)],
        compiler_params=_params("arbitrary"),
        name="attn_in",
    )(x, g, w, *tabs)


def _band_attn_body(*refs, n_heads, n_kv, packed, has_sink, want_lse):
    it = iter(refs)
    q_ref, kc_ref, kp_ref = next(it), next(it), next(it)
    vc_ref, vp_ref = (kc_ref, kp_ref) if packed else (next(it), next(it))
    sink_ref = next(it) if has_sink else None
    o_ref = next(it)
    lse_ref = next(it) if want_lse else None

    i = pl.program_id(1)
    qi = lax.broadcasted_iota(jnp.int32, (BLK, BLK), 0)
    ki = lax.broadcasted_iota(jnp.int32, (BLK, BLK), 1)
    mask_c = ki <= qi
    mask_p = ki >= qi + jnp.where(i > 0, 0, BLK)
    nt = (((1,), (1,)), ((), ()))
    group = n_heads // n_kv
    for kvh in range(n_kv):
        ksl = slice(kvh * HEAD_DIM, (kvh + 1) * HEAD_DIM)
        vsl = slice(HEAD_DIM, 2 * HEAD_DIM) if packed else ksl
        kc, kp = kc_ref[:, ksl].astype(BF16), kp_ref[:, ksl].astype(BF16)
        vc, vp = vc_ref[:, vsl].astype(BF16), vp_ref[:, vsl].astype(BF16)
        for hh in range(group):
            h = kvh * group + hh
            hsl = slice(h * HEAD_DIM, (h + 1) * HEAD_DIM)
            qh = q_ref[:, hsl]
            s_c = lax.dot_general(qh, kc, nt, preferred_element_type=F32)
            s_p = lax.dot_general(qh, kp, nt, preferred_element_type=F32)
            s_c = jnp.where(mask_c, s_c, -jnp.inf)
            s_p = jnp.where(mask_p, s_p, -jnp.inf)
            m = jnp.maximum(jnp.max(s_c, axis=-1, keepdims=True), jnp.max(s_p, axis=-1, keepdims=True))
            if has_sink:
                sk = sink_ref[0:1, h:h + 1]
                m = jnp.maximum(m, sk)
            p_c = jnp.exp(s_c - m)
            p_p = jnp.exp(s_p - m)
            den = jnp.sum(p_c, axis=-1, keepdims=True) + jnp.sum(p_p, axis=-1, keepdims=True)
            if has_sink:
                den = den + jnp.exp(sk - m)
            o = (jnp.dot(p_c.astype(BF16), vc, preferred_element_type=F32)
                 + jnp.dot(p_p.astype(BF16), vp, preferred_element_type=F32))
            o_ref[:, hsl] = o / den
            if want_lse:
                lse_ref[:, hsl] = jnp.broadcast_to(m + jnp.log(den), (BLK, HEAD_DIM))


def _band_attn(q, kv, *, seq_len, n_heads, n_kv, q_col, k_col, v_col=None, sinks=None, want_lse):
    rows = q.shape[0]
    nb = seq_len // BLK
    n_seq = rows // seq_len
    packed = v_col is None
    wq = n_heads * HEAD_DIM
    wk = LANES if packed else n_kv * HEAD_DIM
    cur = lambda c: (lambda n, i: (n * nb + i, c))
    prev = lambda c: (lambda n, i: (n * nb + jnp.maximum(i - 1, 0), c))
    in_specs = [pl.BlockSpec((BLK, wq), cur(q_col)),
                pl.BlockSpec((BLK, wk), cur(k_col)), pl.BlockSpec((BLK, wk), prev(k_col))]
    args = [q, kv, kv]
    if not packed:
        in_specs += [pl.BlockSpec((BLK, wk), cur(v_col)), pl.BlockSpec((BLK, wk), prev(v_col))]
        args += [kv, kv]
    if sinks is not None:
        in_specs.append(_const_spec(sinks.shape))
        args.append(sinks)
    out_spec = pl.BlockSpec((BLK, wq), cur(0))
    out_sds = jax.ShapeDtypeStruct((rows, wq), F32)
    body = functools.partial(_band_attn_body, n_heads=n_heads, n_kv=n_kv, packed=packed,
                             has_sink=sinks is not None, want_lse=want_lse)
    return pl.pallas_call(
        body,
        grid=(n_seq, nb),
        in_specs=in_specs,
        out_specs=[out_spec, out_spec] if want_lse else out_spec,
        out_shape=[out_sds, out_sds] if want_lse else out_sds,
        compiler_params=_params("arbitrary", "arbitrary"),
        name="band_attn",
    )(*args)


def _attn_out_body(osw_ref, o0_ref, o1_ref, o2_ref, l0_ref, l1_ref, l2_ref, h_ref, w_ref, g_ref, o_ref):
    l0, l1, l2 = l0_ref[...], l1_ref[...], l2_ref[...]
    m = jnp.maximum(jnp.maximum(l0, l1), l2)
    e0, e1, e2 = jnp.exp(l0 - m), jnp.exp(l1 - m), jnp.exp(l2 - m)
    o_dil = (e0 * o0_ref[...] + e1 * o1_ref[...] + e2 * o2_ref[...]) / (e0 + e1 + e2)
    mix = jnp.concatenate([osw_ref[...], o_dil], axis=-1).astype(BF16)
    y = jnp.dot(mix, w_ref[...], preferred_element_type=F32)
    o_ref[...] = h_ref[...] + _rmsnorm(y, g_ref[...])


def _attn_out(osw, o_dil, lse_dil, h, w, g, tm):
    t = h.shape[0]
    row = lambda i: (i, 0)
    wd = DIL_HEADS * HEAD_DIM
    dspec = pl.BlockSpec((tm, wd), row)
    return pl.pallas_call(
        _attn_out_body,
        grid=(t // tm,),
        in_specs=[pl.BlockSpec((tm, SWA_HEADS * HEAD_DIM), row)] + [dspec] * 6
                 + [pl.BlockSpec((tm, D_MODEL), row), _const_spec((MIX_ATTN, D_MODEL)),
                    _const_spec((1, D_MODEL))],
        out_specs=pl.BlockSpec((tm, D_MODEL), row),
        out_shape=jax.ShapeDtypeStruct(h.shape, F32),
        compiler_params=_params("arbitrary"),
        name="attn_out",
    )(osw, *o_dil, *lse_dil, h, w, g)


def _attn_in_dec_body(x_ref, g_ref, wt_ref, cos_ref, sin_ref, zt_ref):
    u = _rmsnorm(x_ref[...], g_ref[...]).astype(BF16)
    zt = lax.dot_general(wt_ref[...], u, (((1,), (1,)), ((), ())), preferred_element_type=F32)
    cos, sin = cos_ref[...], sin_ref[...]
    scale = HEAD_DIM ** -0.5
    n_q_heads = Q_COLS // HEAD_DIM
    rot_rows = [hd * HEAD_DIM for hd in range(n_q_heads + SWA_KV)]
    rot_rows += [Q_COLS + (2 + g) * LANES for g in range(len(DIL_PATTERNS))]
    zt_ref[0:Q_COLS, :] = zt[0:Q_COLS, :] * scale
    zt_ref[Q_COLS:ATTN_COLS, :] = zt[Q_COLS:ATTN_COLS, :]
    for r0 in rot_rows:
        sc = scale if r0 < Q_COLS else 1.0
        x1 = zt[r0:r0 + ROT_HALF, :]
        x2 = zt[r0 + ROT_HALF:r0 + 2 * ROT_HALF, :]
        zt_ref[r0:r0 + ROT_HALF, :] = (x1 * cos - x2 * sin) * sc
        zt_ref[r0 + ROT_HALF:r0 + 2 * ROT_HALF, :] = (x2 * cos + x1 * sin) * sc


def _attn_in_dec(x, g, wt, cos, sin):
    args = (x, g, wt, cos, sin)
    return pl.pallas_call(
        _attn_in_dec_body,
        grid=(1,),
        in_specs=[_const_spec(a.shape) for a in args],
        out_specs=_const_spec((ATTN_COLS, DEC_BATCH)),
        out_shape=jax.ShapeDtypeStruct((ATTN_COLS, DEC_BATCH), F32),
        compiler_params=_params("arbitrary"),
        name="attn_in_dec",
    )(*args)


DEC_TILE = 4


def _shift_in(x, new_col):
    nt = x.shape[1] // LANES
    lane = lax.broadcasted_iota(jnp.int32, (1, LANES), 1)
    rolled = [pltpu.roll(x[:, t * LANES:(t + 1) * LANES], LANES - 1, 1) for t in range(nt)]
    tiles = []
    for t in range(nt):
        nxt = rolled[t + 1] if t + 1 < nt else new_col
        tiles.append(jnp.where(lane == LANES - 1, nxt, rolled[t]))
    return tiles[0] if nt == 1 else jnp.concatenate(tiles, axis=1)


def _attn_dec_body(zt_ref, sink_ref, csw_ref, cd0_ref, cd1_ref, cd2_ref,
                   nsw_ref, nd0_ref, nd1_ref, nd2_ref, mixt_ref):
    step = pl.program_id(0)

    @pl.when(step == 0)
    def _():
        mixt_ref[...] = jnp.zeros(mixt_ref.shape, F32)

    lane_b = lax.broadcasted_iota(jnp.int32, (1, DEC_BATCH), 1)
    cd_refs = (cd0_ref, cd1_ref, cd2_ref)
    nd_refs = (nd0_ref, nd1_ref, nd2_ref)
    hd = HEAD_DIM

    def per_seq(j, carry):
        onehot = lane_b == step * DEC_TILE + j
        col = jnp.sum(jnp.where(onehot, zt_ref[...], 0.0), axis=1, keepdims=True)

        def put(r0, o_col):
            mixt_ref[r0:r0 + hd, :] = jnp.where(onehot, o_col, mixt_ref[r0:r0 + hd, :])

        group = SWA_HEADS // SWA_KV
        for kvh in range(SWA_KV):
            k = csw_ref[j, kvh]
            v = csw_ref[j, SWA_KV + kvh]
            k_new = col[Q_COLS + kvh * hd:Q_COLS + (kvh + 1) * hd]
            v_new = col[Q_COLS + LANES + kvh * hd:Q_COLS + LANES + (kvh + 1) * hd]
            for hh in range(group):
                h = kvh * group + hh
                qc = col[h * hd:(h + 1) * hd]
                s = jnp.sum(k * qc, axis=0, keepdims=True)
                s_new = jnp.sum(k_new * qc, axis=0, keepdims=True)
                sk = sink_ref[0:1, h:h + 1]
                m = jnp.maximum(jnp.maximum(jnp.max(s, axis=1, keepdims=True), s_new), sk)
                p = jnp.exp(s - m)
                p_new = jnp.exp(s_new - m)
                den = jnp.sum(p, axis=1, keepdims=True) + p_new + jnp.exp(sk - m)
                o = (jnp.sum(v * p, axis=1, keepdims=True) + v_new * p_new) / den
                put(h * hd, o)
            nsw_ref[j, kvh] = _shift_in(k, k_new)
            nsw_ref[j, SWA_KV + kvh] = _shift_in(v, v_new)

        ks, vs, k_news, v_news, masks = [], [], [], [], []
        for gi, (_, dil) in enumerate(DIL_PATTERNS):
            k, v = cd_refs[gi][j, 0], cd_refs[gi][j, 1]
            r0 = Q_COLS + (2 + gi) * LANES
            k_new, v_new = col[r0:r0 + hd], col[r0 + hd:r0 + 2 * hd]
            pos = lax.broadcasted_iota(jnp.int32, (1, k.shape[1]), 1)
            ks.append(k), vs.append(v), k_news.append(k_new), v_news.append(v_new)
            masks.append((pos & (dil - 1)) == 0)
            nd_refs[gi][j, 0] = _shift_in(k, k_new)
            nd_refs[gi][j, 1] = _shift_in(v, v_new)
        for slot in range(DIL_HEADS):
            ss, s_news = [], []
            m = None
            for gi in range(len(DIL_PATTERNS)):
                r0 = SWA_HEADS * hd + gi * DIL_HEADS * hd + slot * hd
                qc = col[r0:r0 + hd]
                s = jnp.sum(ks[gi] * qc, axis=0, keepdims=True)
                s = jnp.where(masks[gi], s, -jnp.inf)
                s_new = jnp.sum(k_news[gi] * qc, axis=0, keepdims=True)
                mg = jnp.maximum(jnp.max(s, axis=1, keepdims=True), s_new)
                m = mg if m is None else jnp.maximum(m, mg)
                ss.append(s), s_news.append(s_new)
            den = jnp.zeros((1, 1), F32)
            acc = jnp.zeros((hd, 1), F32)
            for gi in range(len(DIL_PATTERNS)):
                p = jnp.exp(ss[gi] - m)
                p_new = jnp.exp(s_news[gi] - m)
                den = den + jnp.sum(p, axis=1, keepdims=True) + p_new
                acc = acc + jnp.sum(vs[gi] * p, axis=1, keepdims=True) + v_news[gi] * p_new
            put(SWA_HEADS * hd + slot * hd, acc / den)
        return carry

    lax.fori_loop(0, DEC_TILE, per_seq, 0)


def _attn_dec(zt, sinks, csw, cd0, cd1, cd2):
    caches = (csw, cd0, cd1, cd2)
    cspec = lambda c: pl.BlockSpec((DEC_TILE,) + c.shape[1:], lambda i: (i, 0, 0, 0))
    return pl.pallas_call(
        _attn_dec_body,
        grid=(DEC_BATCH // DEC_TILE,),
        in_specs=[_const_spec(zt.shape), _const_spec(sinks.shape)] + [cspec(c) for c in caches],
        out_specs=[cspec(c) for c in caches] + [_const_spec((MIX_ATTN, DEC_BATCH))],
        out_shape=[jax.ShapeDtypeStruct(c.shape, F32) for c in caches]
                  + [jax.ShapeDtypeStruct((MIX_ATTN, DEC_BATCH), F32)],
        compiler_params=_params("arbitrary"),
        name="attn_dec",
    )(zt, sinks, *caches)


def _attn_out_dec_body(mixt_ref, h_ref, w_ref, g_ref, o_ref):
    mix = mixt_ref[...].T.astype(BF16)
    y = jnp.dot(mix, w_ref[...], preferred_element_type=F32)
    o_ref[...] = h_ref[...] + _rmsnorm(y, g_ref[...])


def _attn_out_dec(mixt, h, w, g):
    args = (mixt, h, w, g)
    return pl.pallas_call(
        _attn_out_dec_body,
        grid=(1,),
        in_specs=[_const_spec(a.shape) for a in args],
        out_specs=_const_spec(h.shape),
        out_shape=jax.ShapeDtypeStruct(h.shape, F32),
        compiler_params=_params("arbitrary"),
        name="attn_out_dec",
    )(*args)


def _attn_weight_order(w):
    hd = HEAD_DIM
    nq, nkv = SWA_HEADS * hd, SWA_KV * hd
    base = nq + 2 * nkv
    per = (DIL_HEADS + 2) * hd
    qd = [w[:, base + g * per:base + g * per + DIL_HEADS * hd] for g in range(3)]
    kvd = [w[:, base + g * per + DIL_HEADS * hd:base + (g + 1) * per] for g in range(3)]
    return jnp.concatenate([w[:, :nq]] + qd + [w[:, nq:base]] + kvd, axis=1)


def _rope_angles(pos):
    inv = ROPE_THETA ** (-jnp.arange(ROT_HALF, dtype=F32) / ROT_HALF)
    ang = pos.astype(F32)[:, None] * inv[None, :]
    return jnp.cos(ang), jnp.sin(ang)


def _rope_tables(pos):
    cos, sin = _rope_angles(pos)
    n = pos.shape[0]
    ones = lambda w: jnp.ones((n, w), F32)
    zeros = lambda w: jnp.zeros((n, w), F32)
    rest = HEAD_DIM - 2 * ROT_HALF
    c_h = jnp.concatenate([cos, cos, ones(rest)], axis=1)
    lo_h = jnp.concatenate([-sin, zeros(ROT_HALF), zeros(rest)], axis=1)
    hi_h = jnp.concatenate([zeros(ROT_HALF), sin, zeros(rest)], axis=1)
    both = tuple(jnp.concatenate([t, t], axis=1) for t in (c_h, lo_h, hi_h))
    first = (jnp.concatenate([c_h, ones(HEAD_DIM)], axis=1),
             jnp.concatenate([lo_h, zeros(HEAD_DIM)], axis=1),
             jnp.concatenate([hi_h, zeros(HEAD_DIM)], axis=1))
    return both + first


def _cache_view(c):
    b, l, two, kv, hd = c.shape
    return jnp.transpose(c, (0, 2, 3, 4, 1)).reshape(b, two * kv, hd, l)


def _cache_unview(c, kv):
    b, _, hd, l = c.shape
    return jnp.transpose(c.reshape(b, 2, kv, hd, l), (0, 4, 1, 2, 3))


def _deinterleave(x, dil):
    if dil == 1:
        return x
    c = x.shape[1]
    return x.reshape(BATCH, SEQ // dil, dil, c).transpose(0, 2, 1, 3).reshape(BATCH * SEQ, c)


def _interleave(x, dil):
    if dil == 1:
        return x
    c = x.shape[1]
    return x.reshape(BATCH, dil, SEQ // dil, c).transpose(0, 2, 1, 3).reshape(BATCH * SEQ, c)


def kernel(x_prompt, x_sample, state_conv_a, state_conv_b, cache_swa_kv, cache_dil0_kv, cache_dil1_kv,
           cache_dil2_kv, norm_g, w_in_conv, conv_a_w, conv_a_b, conv_a_ln_g, conv_a_ln_b, conv_b_w,
           w_out_conv, w_in_attn, attn_sinks, w_out_attn, mlp_w1, mlp_w2):
    tm = 512
    hp = x_prompt.reshape(BATCH * SEQ, D_MODEL)
    hs = x_sample.reshape(DEC_BATCH, D_MODEL)
    g = lambda layer, i: norm_g[layer, i].reshape(1, D_MODEL)

    w_in0 = w_in_conv[0].astype(BF16)
    w_out0 = w_out_conv[0].astype(BF16)
    conv_small = (conv_a_w[0], conv_a_b, conv_a_ln_g, conv_a_ln_b, conv_b_w[0])
    ga_p, zb_p, gb_p = _conv_in(hp, g(0, 0), w_in0, tm)
    hp = _conv_mix(ga_p, zb_p, gb_p, hp, *conv_small, w_out0, g(0, 1), tt=256)
    ga_s, zb_s, gb_s = _conv_in(hs, g(0, 0), w_in0, DEC_BATCH)
    sta = jnp.transpose(state_conv_a[0], (1, 0, 2))
    sb0, sb1 = state_conv_b[0, :, 0], state_conv_b[0, :, 1]
    hs, new_sta = _conv_mix_dec(ga_s, zb_s, gb_s, hs, sta, sb0, sb1, *conv_small, w_out0, g(0, 1))

    w1, w2 = mlp_w1.astype(BF16), mlp_w2.astype(BF16)
    hp = _mlp(hp, g(0, 2), w1[0], w2[0], g(0, 3), tm)
    hs = _mlp(hs, g(0, 2), w1[0], w2[0], g(0, 3), DEC_BATCH)

    w_in1 = _attn_weight_order(w_in_attn[0]).astype(BF16)
    w_out1 = w_out_attn[0].astype(BF16)
    q_p, kv_p = _attn_in(hp, g(1, 0), w_in1, _rope_tables(jnp.arange(SEQ)), tm)

    o_swa = _band_attn(q_p, kv_p, seq_len=SEQ, n_heads=SWA_HEADS, n_kv=SWA_KV, q_col=0, k_col=0,
                       v_col=1, sinks=attn_sinks, want_lse=False)
    o_dil, lse_dil = [], []
    wd = DIL_HEADS * HEAD_DIM
    for gi, (_, dil) in enumerate(DIL_PATTERNS):
        if dil == 1:
            o, lse = _band_attn(q_p, kv_p, seq_len=SEQ, n_heads=DIL_HEADS, n_kv=1,
                                q_col=SWA_HEADS // DIL_HEADS + gi, k_col=2 + gi, want_lse=True)
        else:
            qd = _deinterleave(q_p[:, SWA_HEADS * HEAD_DIM + gi * wd:SWA_HEADS * HEAD_DIM + (gi + 1) * wd], dil)
            kvd = _deinterleave(kv_p[:, (2 + gi) * LANES:(3 + gi) * LANES], dil)
            o, lse = _band_attn(qd, kvd, seq_len=SEQ // dil, n_heads=DIL_HEADS, n_kv=1,
                                q_col=0, k_col=0, want_lse=True)
            o, lse = _interleave(o, dil), _interleave(lse, dil)
        o_dil.append(o)
        lse_dil.append(lse)
    hp = _attn_out(o_swa, o_dil, lse_dil, hp, w_out1, g(1, 1), tm)

    cos_s, sin_s = _rope_angles(jnp.full((DEC_BATCH,), PAST_LEN, jnp.int32))
    zt = _attn_in_dec(hs, g(1, 0), w_in1.T, cos_s.T, sin_s.T)
    caches = (cache_swa_kv[0], cache_dil0_kv[0], cache_dil1_kv[0], cache_dil2_kv[0])
    nsw, nd0, nd1, nd2, mixt = _attn_dec(zt, attn_sinks, *[_cache_view(c) for c in caches])
    hs = _attn_out_dec(mixt, hs, w_out1, g(1, 1))

    hp = _mlp(hp, g(1, 2), w1[1], w2[1], g(1, 3), tm)
    hs = _mlp(hs, g(1, 2), w1[1], w2[1], g(1, 3), DEC_BATCH)

    n_a, n_b = CONV_WIDTH - 1, SC_WIDTH - 1
    kv3 = kv_p.reshape(BATCH, SEQ, KV_COLS)
    swa_p = kv3[:, SEQ - BLK:, 0:2 * LANES].reshape(BATCH, BLK, 2, SWA_KV, HEAD_DIM)
    dil_p = [kv3[:, SEQ - min(w, SEQ):, (2 + gi) * LANES:(3 + gi) * LANES]
             .reshape(BATCH, min(w, SEQ), 2, 1, HEAD_DIM) for gi, (w, _) in enumerate(DIL_PATTERNS)]
    return (hp.reshape(BATCH, SEQ, D_MODEL), hs.reshape(DEC_BATCH, 1, D_MODEL),
            ga_p.reshape(BATCH, SEQ, CONV_CH)[None, :, SEQ - n_a:],
            jnp.transpose(new_sta, (1, 0, 2))[None],
            zb_p.reshape(BATCH, SEQ, CONV_CH)[None, :, SEQ - n_b:],
            jnp.stack([sb1, zb_s], axis=1)[None],
            swa_p[None], _cache_unview(nsw, SWA_KV)[None],
            dil_p[0][None], _cache_unview(nd0, 1)[None],
            dil_p[1][None], _cache_unview(nd1, 1)[None],
            dil_p[2][None], _cache_unview(nd2, 1)[None])
```

```python
import functools

import jax
import jax.numpy as jnp
from jax import lax
from jax.experimental import pallas as pl
from jax.experimental.pallas import tpu as pltpu

F32 = jnp.float32
BF16 = jnp.bfloat16

D_MODEL = 1024
BATCH = 8
SEQ = 2048
DEC_BATCH = 128
PAST_LEN = 8192
HEAD_DIM = 64
ROT_HALF = 8
ROPE_THETA = 500000.0
D_FF = 4 * D_MODEL
EPS = 1e-6
CONV_CH = 512
CONV_WIDTH = 31
SC_WIDTH = 3
SWA_HEADS = 8
SWA_KV = 2
DIL_HEADS = 4
DIL_PATTERNS = ((128, 1), (512, 4), (2048, 16))
Q_COLS = (SWA_HEADS + 3 * DIL_HEADS) * HEAD_DIM
KV_COLS = 2 * SWA_KV * HEAD_DIM + 3 * 2 * HEAD_DIM
ATTN_COLS = Q_COLS + KV_COLS
MIX_ATTN = (SWA_HEADS + DIL_HEADS) * HEAD_DIM
LANES = 128
BLK = 128

V7X_VMEM_BYTES = 64 * 1024 * 1024
VMEM_LIMIT = V7X_VMEM_BYTES - 8 * 1024 * 1024


def _params(*sem):
    return pltpu.CompilerParams(dimension_semantics=sem, vmem_limit_bytes=VMEM_LIMIT)


def _const_spec(shape):
    zeros = (0,) * len(shape)
    return pl.BlockSpec(shape, lambda *_: zeros)


def _rmsnorm(x, g):
    return x * lax.rsqrt(jnp.mean(x * x, axis=-1, keepdims=True) + EPS) * g


def _conv_in_body(x_ref, g_ref, w_ref, ga_ref, zb_ref, gb_ref):
    u = _rmsnorm(x_ref[...], g_ref[...]).astype(BF16)
    z = jnp.dot(u, w_ref[...], preferred_element_type=F32)
    c = CONV_CH
    ga_ref[...] = z[:, 0:c] * jax.nn.sigmoid(z[:, c:2 * c])
    zb_ref[...] = z[:, 4 * c:5 * c] * z[:, 2 * c:3 * c]
    gb_ref[...] = z[:, 3 * c:4 * c]


def _conv_in(x, g, w, tm):
    t = x.shape[0]
    row = lambda i: (i, 0)
    out = jax.ShapeDtypeStruct((t, CONV_CH), F32)
    return pl.pallas_call(
        _conv_in_body,
        grid=(t // tm,),
        in_specs=[pl.BlockSpec((tm, D_MODEL), row), _const_spec((1, D_MODEL)),
                  _const_spec((D_MODEL, 5 * CONV_CH))],
        out_specs=[pl.BlockSpec((tm, CONV_CH), row)] * 3,
        out_shape=[out] * 3,
        compiler_params=_params("arbitrary"),
        name="conv_in",
    )(x, g, w)


HALO_A = 32
HALO_B = 8
CONV_CHUNK = 32


def _layernorm_silu(c, g, b):
    mu = jnp.mean(c, axis=-1, keepdims=True)
    d = c - mu
    var = jnp.mean(d * d, axis=-1, keepdims=True)
    y = d * lax.rsqrt(var + EPS) * g + b
    return y * jax.nn.sigmoid(y)


def _conv_mix_body(ga_ref, zb_ref, gb_ref, h_ref, aw_ref, ab_ref, lng_ref, lnb_ref, bw_ref,
                   w_ref, g_ref, o_ref, ext_a, ext_b, conv_scr, *, tt):
    t = pl.program_id(1)

    @pl.when(t == 0)
    def _():
        ext_a[0:HALO_A, :] = jnp.zeros((HALO_A, CONV_CH), F32)
        ext_b[0:HALO_B, :] = jnp.zeros((HALO_B, CONV_CH), F32)

    @pl.when(t > 0)
    def _():
        ext_a[0:HALO_A, :] = ext_a[tt:tt + HALO_A, :]
        ext_b[0:HALO_B, :] = ext_b[tt:tt + HALO_B, :]

    ext_a[HALO_A:HALO_A + tt, :] = ga_ref[...]
    ext_b[HALO_B:HALO_B + tt, :] = zb_ref[...]

    off_a = HALO_A - (CONV_WIDTH - 1)

    for base in range(0, tt, CONV_CHUNK):
        acc = jnp.zeros((CONV_CHUNK, CONV_CH), F32)
        for j in range(CONV_WIDTH):
            acc = acc + aw_ref[j:j + 1, :] * ext_a[base + off_a + j:base + off_a + j + CONV_CHUNK, :]
        conv_scr[base:base + CONV_CHUNK, :] = acc

    ya = _layernorm_silu(conv_scr[...] + ab_ref[...], lng_ref[...], lnb_ref[...])
    off_b = HALO_B - (SC_WIDTH - 1)
    cb = bw_ref[0:1, :] * ext_b[off_b:off_b + tt, :]
    for j in range(1, SC_WIDTH):
        cb = cb + bw_ref[j:j + 1, :] * ext_b[off_b + j:off_b + j + tt, :]
    yb = gb_ref[...] * cb
    mix = jnp.concatenate([ya, yb], axis=-1).astype(BF16)
    y = jnp.dot(mix, w_ref[...], preferred_element_type=F32)
    o_ref[...] = h_ref[...] + _rmsnorm(y, g_ref[...])


def _conv_mix(ga, zb, gb, h, aw, ab, lng, lnb, bw, w, g, tt):
    nt = SEQ // tt
    row = lambda b, t: (b * nt + t, 0)
    act = pl.BlockSpec((tt, CONV_CH), row)
    return pl.pallas_call(
        functools.partial(_conv_mix_body, tt=tt),
        grid=(BATCH, nt),
        in_specs=[act, act, act, pl.BlockSpec((tt, D_MODEL), row),
                  _const_spec((CONV_WIDTH, CONV_CH)), _const_spec((1, CONV_CH)),
                  _const_spec((1, CONV_CH)), _const_spec((1, CONV_CH)),
                  _const_spec((SC_WIDTH, CONV_CH)), _const_spec((D_MODEL, D_MODEL)),
                  _const_spec((1, D_MODEL))],
        out_specs=pl.BlockSpec((tt, D_MODEL), row),
        out_shape=jax.ShapeDtypeStruct(h.shape, F32),
        scratch_shapes=[pltpu.VMEM((HALO_A + tt, CONV_CH), F32),
                        pltpu.VMEM((HALO_B + tt, CONV_CH), F32),
                        pltpu.VMEM((tt, CONV_CH), F32)],
        compiler_params=_params("arbitrary", "arbitrary"),
        name="conv_mix",
    )(ga, zb, gb, h, aw, ab, lng, lnb, bw, w, g)


def _conv_mix_dec_body(ga_ref, zb_ref, gb_ref, h_ref, sta_ref, sb0_ref, sb1_ref, aw_ref, ab_ref,
                       lng_ref, lnb_ref, bw_ref, w_ref, g_ref, o_ref, nsta_ref):
    ga = ga_ref[...]
    n_state = CONV_WIDTH - 1
    acc = aw_ref[n_state:n_state + 1, :] * ga
    for j in range(n_state):
        acc = acc + aw_ref[j:j + 1, :] * sta_ref[j]
    for j in range(n_state - 1):
        nsta_ref[j] = sta_ref[j + 1]
    nsta_ref[n_state - 1] = ga
    ya = _layernorm_silu(acc + ab_ref[...], lng_ref[...], lnb_ref[...])
    cb = bw_ref[0:1, :] * sb0_ref[...] + bw_ref[1:2, :] * sb1_ref[...] + bw_ref[2:3, :] * zb_ref[...]
    yb = gb_ref[...] * cb
    mix = jnp.concatenate([ya, yb], axis=-1).astype(BF16)
    y = jnp.dot(mix, w_ref[...], preferred_element_type=F32)
    o_ref[...] = h_ref[...] + _rmsnorm(y, g_ref[...])


def _conv_mix_dec(ga, zb, gb, h, sta, sb0, sb1, aw, ab, lng, lnb, bw, w, g):
    args = (ga, zb, gb, h, sta, sb0, sb1, aw, ab, lng, lnb, bw, w, g)
    return pl.pallas_call(
        _conv_mix_dec_body,
        grid=(1,),
        in_specs=[_const_spec(a.shape) for a in args],
        out_specs=[_const_spec(h.shape), _const_spec(sta.shape)],
        out_shape=[jax.ShapeDtypeStruct(h.shape, F32), jax.ShapeDtypeStruct(sta.shape, F32)],
        compiler_params=_params("arbitrary"),
        name="conv_mix_dec",
    )(*args)


FF_CHUNK = 1024


def _mlp_body(x_ref, g2_ref, w1_ref, w2_ref, g3_ref, o_ref):
    x = x_ref[...]
    u = _rmsnorm(x, g2_ref[...]).astype(BF16)
    acc = jnp.zeros(x.shape, F32)
    for c in range(D_FF // FF_CHUNK):
        sl = slice(c * FF_CHUNK, (c + 1) * FF_CHUNK)
        hid = jnp.dot(u, w1_ref[:, sl], preferred_element_type=F32)
        hid = jnp.square(jnp.maximum(hid, 0.0)).astype(BF16)
        acc = acc + jnp.dot(hid, w2_ref[sl, :], preferred_element_type=F32)
    o_ref[...] = x + _rmsnorm(acc, g3_ref[...])


def _mlp(x, g2, w1, w2, g3, tm):
    t = x.shape[0]
    row = lambda i: (i, 0)
    return pl.pallas_call(
        _mlp_body,
        grid=(t // tm,),
        in_specs=[pl.BlockSpec((tm, D_MODEL), row), _const_spec((1, D_MODEL)),
                  _const_spec((D_MODEL, D_FF)), _const_spec((D_FF, D_MODEL)),
                  _const_spec((1, D_MODEL))],
        out_specs=pl.BlockSpec((tm, D_MODEL), row),
        out_shape=jax.ShapeDtypeStruct(x.shape, F32),
        compiler_params=_params("arbitrary"),
        name="mlp",
    )(x, g2, w1, w2, g3)


def _rope_slab(z, c, s_lo, s_hi):
    return z * c + pltpu.roll(z, LANES - ROT_HALF, 1) * s_lo + pltpu.roll(z, ROT_HALF, 1) * s_hi


def _attn_in_body(x_ref, g_ref, w_ref, ca_ref, sla_ref, sha_ref, cb_ref, slb_ref, shb_ref,
                  q_ref, kv_ref):
    u = _rmsnorm(x_ref[...], g_ref[...]).astype(BF16)
    z = jnp.dot(u, w_ref[...], preferred_element_type=F32)
    ca, sla, sha = ca_ref[...], sla_ref[...], sha_ref[...]
    scale = HEAD_DIM ** -0.5
    for s in range(Q_COLS // LANES):
        sl = slice(s * LANES, (s + 1) * LANES)
        q_ref[:, sl] = (_rope_slab(z[:, sl], ca, sla, sha) * scale).astype(BF16)
    kv_ref[:, 0:LANES] = _rope_slab(z[:, Q_COLS:Q_COLS + LANES], ca, sla, sha)
    kv_ref[:, LANES:2 * LANES] = z[:, Q_COLS + LANES:Q_COLS + 2 * LANES]
    cb, slb, shb = cb_ref[...], slb_ref[...], shb_ref[...]
    for s in range(2, KV_COLS // LANES):
        sl = slice(Q_COLS + s * LANES, Q_COLS + (s + 1) * LANES)
        kv_ref[:, s * LANES:(s + 1) * LANES] = _rope_slab(z[:, sl], cb, slb, shb)


def _attn_in(x, g, w, tabs, tm):
    t = x.shape[0]
    row = lambda i: (i, 0)
    nper = tabs[0].shape[0] // tm
    tab = pl.BlockSpec((tm, LANES), lambda i: (i % nper, 0))
    return pl.pallas_call(
        _attn_in_body,
        grid=(t // tm,),
        in_specs=[pl.BlockSpec((tm, D_MODEL), row), _const_spec((1, D_MODEL)),
                  _const_spec((D_MODEL, ATTN_COLS))] + [tab] * 6,
        out_specs=[pl.BlockSpec((tm, Q_COLS), row), pl.BlockSpec((tm, KV_COLS), row)],
        out_shape=[jax.ShapeDtypeStruct((t, Q_COLS), BF16), jax.ShapeDtypeStruct((t, KV_COLS), F32)],
        compiler_params=_params("arbitrary"),
        name="attn_in",
    )(x, g, w, *tabs)


def _band_attn_body(*refs, n_heads, n_kv, packed, has_sink, want_lse, has_prev, steps_per_seq):
    it = iter(refs)
    q_ref, kc_ref = next(it), next(it)
    kp_ref = next(it) if has_prev else None
    if packed:
        vc_ref, vp_ref = kc_ref, kp_ref
    else:
        vc_ref = next(it)
        vp_ref = next(it) if has_prev else None
    sink_ref = next(it) if has_sink else None
    o_ref = next(it)
    lse_ref = next(it) if want_lse else None

    n_sub = q_ref.shape[0] // BLK
    n_keys = 2 * BLK if has_prev else BLK
    lane = lax.broadcasted_iota(jnp.int32, (BLK, LANES), 1)
    key = lax.broadcasted_iota(jnp.int32, (n_keys, BLK), 0)
    qry = lax.broadcasted_iota(jnp.int32, (n_keys, BLK), 1)
    if has_prev:
        first = pl.program_id(0) % steps_per_seq == 0
        no_prev = jnp.where(first, -4 * BLK, 0)
        valid = jnp.where(key < BLK, key - qry, qry - key + BLK) >= 0
        valid_first = jnp.where(key < BLK, key - qry + no_prev, qry - key + BLK) >= 0
    else:
        valid = valid_first = key <= qry
    nt = (((1,), (1,)), ((), ()))
    tn = (((0,), (0,)), ((), ()))
    slabs_per_kv = n_heads // 2 // n_kv

    def halves(x, in_hi):
        own = jnp.where(lane >= HEAD_DIM if in_hi else lane < HEAD_DIM, x, 0.0)
        other = pltpu.roll(own, HEAD_DIM, 1)
        return (other, own) if in_hi else (own, other)

    def padded(cur, prev, in_hi):
        lo_c, hi_c = halves(cur, in_hi)
        if prev is None:
            parts = [lo_c, hi_c]
        else:
            lo_p, hi_p = halves(prev, in_hi)
            parts = [lo_p, lo_c, hi_p, hi_c]
        return jnp.concatenate(parts, axis=0).astype(BF16)

    for j in range(n_sub):
        rows = slice(j * BLK, (j + 1) * BLK)
        prows = slice((j - 1) * BLK, j * BLK)
        vmask = valid_first if j == 0 else valid
        scores, kv_of = [], []
        vpads = []
        for kvh in range(n_kv):
            k_hi = (not packed) and kvh == 1
            v_hi = packed or kvh == 1
            k_cur, v_cur = kc_ref[rows, 0:LANES], vc_ref[rows, 0:LANES]
            if not has_prev:
                k_prev = v_prev = None
            elif j == 0:
                k_prev, v_prev = kp_ref[:, 0:LANES], vp_ref[:, 0:LANES]
            else:
                k_prev, v_prev = kc_ref[prows, 0:LANES], vc_ref[prows, 0:LANES]
            kpad = padded(k_cur, k_prev, k_hi)
            vpads.append(padded(v_cur, v_prev, v_hi))
            for s in range(slabs_per_kv):
                slab = kvh * slabs_per_kv + s
                qs = q_ref[rows, slab * LANES:(slab + 1) * LANES]
                scores.append(lax.dot_general(kpad, qs, nt, preferred_element_type=F32))
                kv_of.append(kvh)
        probs, dens, lses = [], [], []
        for slab, st in enumerate(scores):
            ps, ds, ls = [], [], []
            for half in range(2):
                sh = jnp.where(vmask, st[half * n_keys:(half + 1) * n_keys], -jnp.inf)
                m = jnp.max(sh, axis=0, keepdims=True)
                if has_sink:
                    sk = sink_ref[0:1, 2 * slab + half:2 * slab + half + 1]
                    m = jnp.maximum(m, sk)
                p = jnp.exp(sh - m)
                den = jnp.sum(p, axis=0, keepdims=True)
                if has_sink:
                    den = den + jnp.exp(sk - m)
                ps.append(p)
                ds.append(jnp.broadcast_to(den, (HEAD_DIM, BLK)))
                ls.append(jnp.broadcast_to(m + jnp.log(den), (HEAD_DIM, BLK)))
            probs.append(jnp.concatenate(ps, axis=0).astype(BF16))
            dens.append(jnp.concatenate(ds, axis=0))
            lses.append(jnp.concatenate(ls, axis=0))
        for slab, p in enumerate(probs):
            ot = lax.dot_general(vpads[kv_of[slab]], p, tn, preferred_element_type=F32)
            cols = slice(slab * LANES, (slab + 1) * LANES)
            o_ref[rows, cols] = (ot / dens[slab]).T
            if want_lse:
                lse_ref[rows, cols] = lses[slab].T


ATTN_ROWS = 4 * BLK


def _band_attn(q, kv, *, seq_len, n_heads, n_kv, q_col, k_col, v_col=None, sinks=None, want_lse):
    rows = q.shape[0]
    packed = v_col is None
    has_prev = seq_len > BLK
    assert seq_len == BLK or seq_len % ATTN_ROWS == 0
    wq = n_heads * HEAD_DIM
    sub = ATTN_ROWS // BLK
    cur = lambda c: (lambda s: (s, c))
    prev = lambda c: (lambda s: (jnp.maximum(s * sub - 1, 0), c))
    in_specs = [pl.BlockSpec((ATTN_ROWS, wq), cur(q_col)), pl.BlockSpec((ATTN_ROWS, LANES), cur(k_col))]
    args = [q, kv]
    if has_prev:
        in_specs.append(pl.BlockSpec((BLK, LANES), prev(k_col)))
        args.append(kv)
    if not packed:
        in_specs.append(pl.BlockSpec((ATTN_ROWS, LANES), cur(v_col)))
        args.append(kv)
        if has_prev:
            in_specs.append(pl.BlockSpec((BLK, LANES), prev(v_col)))
            args.append(kv)
    if sinks is not None:
        in_specs.append(_const_spec(sinks.shape))
        args.append(sinks)
    out_spec = pl.BlockSpec((ATTN_ROWS, wq), cur(0))
    out_sds = jax.ShapeDtypeStruct((rows, wq), F32)
    body = functools.partial(_band_attn_body, n_heads=n_heads, n_kv=n_kv, packed=packed,
                             has_sink=sinks is not None, want_lse=want_lse, has_prev=has_prev,
                             steps_per_seq=max(seq_len // ATTN_ROWS, 1))
    return pl.pallas_call(
        body,
        grid=(rows // ATTN_ROWS,),
        in_specs=in_specs,
        out_specs=[out_spec, out_spec] if want_lse else out_spec,
        out_shape=[out_sds, out_sds] if want_lse else out_sds,
        compiler_params=_params("arbitrary"),
        name="band_attn",
    )(*args)


def _attn_out_body(osw_ref, o0_ref, o1_ref, o2_ref, l0_ref, l1_ref, l2_ref, h_ref, w_ref, g_ref, o_ref):
    l0, l1, l2 = l0_ref[...], l1_ref[...], l2_ref[...]
    m = jnp.maximum(jnp.maximum(l0, l1), l2)
    e0, e1, e2 = jnp.exp(l0 - m), jnp.exp(l1 - m), jnp.exp(l2 - m)
    o_dil = (e0 * o0_ref[...] + e1 * o1_ref[...] + e2 * o2_ref[...]) / (e0 + e1 + e2)
    mix = jnp.concatenate([osw_ref[...], o_dil], axis=-1).astype(BF16)
    y = jnp.dot(mix, w_ref[...], preferred_element_type=F32)
    o_ref[...] = h_ref[...] + _rmsnorm(y, g_ref[...])


def _attn_out(osw, o_dil, lse_dil, h, w, g, tm):
    t = h.shape[0]
    row = lambda i: (i, 0)
    wd = DIL_HEADS * HEAD_DIM
    dspec = pl.BlockSpec((tm, wd), row)
    return pl.pallas_call(
        _attn_out_body,
        grid=(t // tm,),
        in_specs=[pl.BlockSpec((tm, SWA_HEADS * HEAD_DIM), row)] + [dspec] * 6
                 + [pl.BlockSpec((tm, D_MODEL), row), _const_spec((MIX_ATTN, D_MODEL)),
                    _const_spec((1, D_MODEL))],
        out_specs=pl.BlockSpec((tm, D_MODEL), row),
        out_shape=jax.ShapeDtypeStruct(h.shape, F32),
        compiler_params=_params("arbitrary"),
        name="attn_out",
    )(osw, *o_dil, *lse_dil, h, w, g)


DEC_TILE = 4
N_MIX_HEADS = SWA_HEADS + DIL_HEADS


def _attn_dec_body(q_ref, kv_ref, kv_all_ref, sink_ref, csw_ref, cd0_ref, cd1_ref, cd2_ref,
                   nsw_ref, nd0_ref, nd1_ref, nd2_ref, o_ref, kvt_scr):
    step = pl.program_id(0)
    bb, hd, grp = DEC_TILE, HEAD_DIM, DIL_HEADS
    n_rows = grp * bb

    @pl.when(step == 0)
    def _():
        kvt_scr[...] = kv_all_ref[...].T

    q = q_ref[0]
    kv = kv_ref[0]
    row_seq = lax.broadcasted_iota(jnp.int32, (n_rows, bb * hd), 0) & (bb - 1)
    col_seq = lax.broadcasted_iota(jnp.int32, (n_rows, bb * hd), 1) // hd
    diag = row_seq == col_seq
    lane = lax.broadcasted_iota(jnp.int32, (1, LANES), 1)
    nt = (((1,), (1,)), ((), ()))

    def q_rows(c0):
        return jnp.concatenate([q[:, c0 + s * hd:c0 + (s + 1) * hd] for s in range(grp)], axis=0)

    def per_row(x):
        return jnp.concatenate([x] * grp, axis=0)

    def block_diag(qr):
        return jnp.where(diag, jnp.concatenate([qr] * bb, axis=1), 0.0).astype(BF16)

    def take_diag(ob):
        ob = jnp.where(diag, ob, 0.0)
        out = ob[:, 0:hd]
        for b in range(1, bb):
            out = out + ob[:, b * hd:(b + 1) * hd]
        return out

    def shift_in(x, r0):
        slab = kvt_scr[r0:r0 + hd, :]
        new = jnp.concatenate([pltpu.roll(slab, LANES - 1 - step * bb - b, 1) for b in range(bb)], axis=0)
        n_t = x.shape[1] // LANES
        rolled = [pltpu.roll(x[:, t * LANES:(t + 1) * LANES], LANES - 1, 1) for t in range(n_t)]
        tiles = [jnp.where(lane == LANES - 1, rolled[t + 1] if t + 1 < n_t else new, rolled[t])
                 for t in range(n_t)]
        return tiles[0] if n_t == 1 else jnp.concatenate(tiles, axis=1)

    def stack(ref, idx):
        x = ref[:, idx]
        return x.reshape(bb * hd, x.shape[2])

    def unstack(x):
        return x.reshape(bb, hd, x.shape[1])

    for kvh in range(SWA_KV):
        k, v = stack(csw_ref, kvh), stack(csw_ref, SWA_KV + kvh)
        qr = q_rows(kvh * grp * hd)
        k_new = per_row(kv[:, kvh * hd:(kvh + 1) * hd])
        v_new = per_row(kv[:, LANES + kvh * hd:LANES + (kvh + 1) * hd])
        sk = jnp.concatenate([jnp.broadcast_to(sink_ref[0:1, kvh * grp + s:kvh * grp + s + 1], (bb, 1))
                              for s in range(grp)], axis=0)
        s = jnp.dot(block_diag(qr), k.astype(BF16), preferred_element_type=F32)
        s_new = jnp.sum(qr * k_new, axis=1, keepdims=True)
        m = jnp.maximum(jnp.maximum(jnp.max(s, axis=1, keepdims=True), s_new), sk)
        p = jnp.exp(s - m)
        p_new = jnp.exp(s_new - m)
        den = jnp.sum(p, axis=1, keepdims=True) + p_new + jnp.exp(sk - m)
        pv = lax.dot_general(p.astype(BF16), v.astype(BF16), nt, preferred_element_type=F32)
        o = (take_diag(pv) + p_new * v_new) / den
        for s_ in range(grp):
            o_ref[0, kvh * grp + s_] = o[s_ * bb:(s_ + 1) * bb]
        nsw_ref[:, kvh] = unstack(shift_in(k, kvh * hd))
        nsw_ref[:, SWA_KV + kvh] = unstack(shift_in(v, LANES + kvh * hd))

    cd_refs = (cd0_ref, cd1_ref, cd2_ref)
    nd_refs = (nd0_ref, nd1_ref, nd2_ref)
    scores, s_news, vs, v_news = [], [], [], []
    m = None
    for gi, (_, dil) in enumerate(DIL_PATTERNS):
        k, v = stack(cd_refs[gi], 0), stack(cd_refs[gi], 1)
        c0 = (2 + gi) * LANES
        qr = q_rows(SWA_HEADS * hd + gi * grp * hd)
        s = jnp.dot(block_diag(qr), k.astype(BF16), preferred_element_type=F32)
        pos = lax.broadcasted_iota(jnp.int32, (1, k.shape[1]), 1)
        s = jnp.where((pos & (dil - 1)) == 0, s, -jnp.inf)
        s_new = jnp.sum(qr * per_row(kv[:, c0:c0 + hd]), axis=1, keepdims=True)
        mg = jnp.maximum(jnp.max(s, axis=1, keepdims=True), s_new)
        m = mg if m is None else jnp.maximum(m, mg)
        scores.append(s), s_news.append(s_new), vs.append(v)
        v_news.append(per_row(kv[:, c0 + hd:c0 + 2 * hd]))
        nd_refs[gi][:, 0] = unstack(shift_in(k, c0))
        nd_refs[gi][:, 1] = unstack(shift_in(v, c0 + hd))
    den = jnp.zeros((n_rows, 1), F32)
    acc = jnp.zeros((n_rows, hd), F32)
    for gi in range(len(DIL_PATTERNS)):
        p = jnp.exp(scores[gi] - m)
        p_new = jnp.exp(s_news[gi] - m)
        den = den + jnp.sum(p, axis=1, keepdims=True) + p_new
        pv = lax.dot_general(p.astype(BF16), vs[gi].astype(BF16), nt, preferred_element_type=F32)
        acc = acc + take_diag(pv) + p_new * v_news[gi]
    o = acc / den
    for s_ in range(grp):
        o_ref[0, SWA_HEADS + s_] = o[s_ * bb:(s_ + 1) * bb]


def _attn_dec(q, kv, sinks, csw, cd0, cd1, cd2):
    caches = (csw, cd0, cd1, cd2)
    n_tiles = DEC_BATCH // DEC_TILE
    q3 = q.reshape(n_tiles, DEC_TILE, Q_COLS)
    kv3 = kv.reshape(n_tiles, DEC_TILE, KV_COLS)
    tile3 = lambda w: pl.BlockSpec((1, DEC_TILE, w), lambda i: (i, 0, 0))
    cspec = lambda c: pl.BlockSpec((DEC_TILE,) + c.shape[1:], lambda i: (i, 0, 0, 0))
    o_shape = (n_tiles, N_MIX_HEADS, DEC_TILE, HEAD_DIM)
    outs = pl.pallas_call(
        _attn_dec_body,
        grid=(n_tiles,),
        in_specs=[tile3(Q_COLS), tile3(KV_COLS), _const_spec(kv.shape), _const_spec(sinks.shape)]
                 + [cspec(c) for c in caches],
        out_specs=[cspec(c) for c in caches]
                  + [pl.BlockSpec((1,) + o_shape[1:], lambda i: (i, 0, 0, 0))],
        out_shape=[jax.ShapeDtypeStruct(c.shape, F32) for c in caches]
                  + [jax.ShapeDtypeStruct(o_shape, F32)],
        scratch_shapes=[pltpu.VMEM((KV_COLS, DEC_BATCH), F32)],
        compiler_params=_params("arbitrary"),
        name="attn_dec",
    )(q3, kv3, kv, sinks, *caches)
    mix = jnp.transpose(outs[4], (0, 2, 1, 3)).reshape(DEC_BATCH, MIX_ATTN)
    return outs[0], outs[1], outs[2], outs[3], mix


def _attn_out_dec_body(mix_ref, h_ref, w_ref, g_ref, o_ref):
    y = jnp.dot(mix_ref[...].astype(BF16), w_ref[...], preferred_element_type=F32)
    o_ref[...] = h_ref[...] + _rmsnorm(y, g_ref[...])


def _attn_out_dec(mixt, h, w, g):
    args = (mixt, h, w, g)
    return pl.pallas_call(
        _attn_out_dec_body,
        grid=(1,),
        in_specs=[_const_spec(a.shape) for a in args],
        out_specs=_const_spec(h.shape),
        out_shape=jax.ShapeDtypeStruct(h.shape, F32),
        compiler_params=_params("arbitrary"),
        name="attn_out_dec",
    )(*args)


def _attn_weight_order(w):
    hd = HEAD_DIM
    nq, nkv = SWA_HEADS * hd, SWA_KV * hd
    base = nq + 2 * nkv
    per = (DIL_HEADS + 2) * hd
    qd = [w[:, base + g * per:base + g * per + DIL_HEADS * hd] for g in range(3)]
    kvd = [w[:, base + g * per + DIL_HEADS * hd:base + (g + 1) * per] for g in range(3)]
    return jnp.concatenate([w[:, :nq]] + qd + [w[:, nq:base]] + kvd, axis=1)


def _rope_angles(pos):
    inv = ROPE_THETA ** (-jnp.arange(ROT_HALF, dtype=F32) / ROT_HALF)
    ang = pos.astype(F32)[:, None] * inv[None, :]
    return jnp.cos(ang), jnp.sin(ang)


def _rope_tables(pos):
    cos, sin = _rope_angles(pos)
    n = pos.shape[0]
    ones = lambda w: jnp.ones((n, w), F32)
    zeros = lambda w: jnp.zeros((n, w), F32)
    rest = HEAD_DIM - 2 * ROT_HALF
    c_h = jnp.concatenate([cos, cos, ones(rest)], axis=1)
    lo_h = jnp.concatenate([-sin, zeros(ROT_HALF), zeros(rest)], axis=1)
    hi_h = jnp.concatenate([zeros(ROT_HALF), sin, zeros(rest)], axis=1)
    both = tuple(jnp.concatenate([t, t], axis=1) for t in (c_h, lo_h, hi_h))
    first = (jnp.concatenate([c_h, ones(HEAD_DIM)], axis=1),
             jnp.concatenate([lo_h, zeros(HEAD_DIM)], axis=1),
             jnp.concatenate([hi_h, zeros(HEAD_DIM)], axis=1))
    return both + first


def _cache_view(c):
    b, l, two, kv, hd = c.shape
    return jnp.transpose(c, (0, 2, 3, 4, 1)).reshape(b, two * kv, hd, l)


def _cache_unview(c, kv):
    b, _, hd, l = c.shape
    return jnp.transpose(c.reshape(b, 2, kv, hd, l), (0, 4, 1, 2, 3))


def _deinterleave(x, dil):
    if dil == 1:
        return x
    c = x.shape[1]
    return x.reshape(BATCH, SEQ // dil, dil, c).transpose(0, 2, 1, 3).reshape(BATCH * SEQ, c)


def _interleave(x, dil):
    if dil == 1:
        return x
    c = x.shape[1]
    return x.reshape(BATCH, dil, SEQ // dil, c).transpose(0, 2, 1, 3).reshape(BATCH * SEQ, c)


def kernel(x_prompt, x_sample, state_conv_a, state_conv_b, cache_swa_kv, cache_dil0_kv, cache_dil1_kv,
           cache_dil2_kv, norm_g, w_in_conv, conv_a_w, conv_a_b, conv_a_ln_g, conv_a_ln_b, conv_b_w,
           w_out_conv, w_in_attn, attn_sinks, w_out_attn, mlp_w1, mlp_w2):
    tm = 512
    hp = x_prompt.reshape(BATCH * SEQ, D_MODEL)
    hs = x_sample.reshape(DEC_BATCH, D_MODEL)
    g = lambda layer, i: norm_g[layer, i].reshape(1, D_MODEL)

    w_in0 = w_in_conv[0].astype(BF16)
    w_out0 = w_out_conv[0].astype(BF16)
    conv_small = (conv_a_w[0], conv_a_b, conv_a_ln_g, conv_a_ln_b, conv_b_w[0])
    ga_p, zb_p, gb_p = _conv_in(hp, g(0, 0), w_in0, tm)
    hp = _conv_mix(ga_p, zb_p, gb_p, hp, *conv_small, w_out0, g(0, 1), tt=256)
    ga_s, zb_s, gb_s = _conv_in(hs, g(0, 0), w_in0, DEC_BATCH)
    sta = jnp.transpose(state_conv_a[0], (1, 0, 2))
    sb0, sb1 = state_conv_b[0, :, 0], state_conv_b[0, :, 1]
    hs, new_sta = _conv_mix_dec(ga_s, zb_s, gb_s, hs, sta, sb0, sb1, *conv_small, w_out0, g(0, 1))

    w1, w2 = mlp_w1.astype(BF16), mlp_w2.astype(BF16)
    hp = _mlp(hp, g(0, 2), w1[0], w2[0], g(0, 3), tm)
    hs = _mlp(hs, g(0, 2), w1[0], w2[0], g(0, 3), DEC_BATCH)

    w_in1 = _attn_weight_order(w_in_attn[0]).astype(BF16)
    w_out1 = w_out_attn[0].astype(BF16)
    q_p, kv_p = _attn_in(hp, g(1, 0), w_in1, _rope_tables(jnp.arange(SEQ)), tm)

    o_swa = _band_attn(q_p, kv_p, seq_len=SEQ, n_heads=SWA_HEADS, n_kv=SWA_KV, q_col=0, k_col=0,
                       v_col=1, sinks=attn_sinks, want_lse=False)
    o_dil, lse_dil = [], []
    wd = DIL_HEADS * HEAD_DIM
    for gi, (_, dil) in enumerate(DIL_PATTERNS):
        if dil == 1:
            o, lse = _band_attn(q_p, kv_p, seq_len=SEQ, n_heads=DIL_HEADS, n_kv=1,
                                q_col=SWA_HEADS // DIL_HEADS + gi, k_col=2 + gi, want_lse=True)
        else:
            qd = _deinterleave(q_p[:, SWA_HEADS * HEAD_DIM + gi * wd:SWA_HEADS * HEAD_DIM + (gi + 1) * wd], dil)
            kvd = _deinterleave(kv_p[:, (2 + gi) * LANES:(3 + gi) * LANES], dil)
            o, lse = _band_attn(qd, kvd, seq_len=SEQ // dil, n_heads=DIL_HEADS, n_kv=1,
                                q_col=0, k_col=0, want_lse=True)
            o, lse = _interleave(o, dil), _interleave(lse, dil)
        o_dil.append(o)
        lse_dil.append(lse)
    hp = _attn_out(o_swa, o_dil, lse_dil, hp, w_out1, g(1, 1), tm)

    tabs_s = _rope_tables(jnp.full((DEC_BATCH,), PAST_LEN, jnp.int32))
    q_s, kv_s = _attn_in(hs, g(1, 0), w_in1, tabs_s, DEC_BATCH)
    caches = (cache_swa_kv[0], cache_dil0_kv[0], cache_dil1_kv[0], cache_dil2_kv[0])
    nsw, nd0, nd1, nd2, mix_s = _attn_dec(q_s.astype(F32), kv_s, attn_sinks,
                                          *[_cache_view(c) for c in caches])
    hs = _attn_out_dec(mix_s, hs, w_out1, g(1, 1))

    hp = _mlp(hp, g(1, 2), w1[1], w2[1], g(1, 3), tm)
    hs = _mlp(hs, g(1, 2), w1[1], w2[1], g(1, 3), DEC_BATCH)

    n_a, n_b = CONV_WIDTH - 1, SC_WIDTH - 1
    kv3 = kv_p.reshape(BATCH, SEQ, KV_COLS)
    swa_p = kv3[:, SEQ - BLK:, 0:2 * LANES].reshape(BATCH, BLK, 2, SWA_KV, HEAD_DIM)
    dil_p = [kv3[:, SEQ - min(w, SEQ):, (2 + gi) * LANES:(3 + gi) * LANES]
             .reshape(BATCH, min(w, SEQ), 2, 1, HEAD_DIM) for gi, (w, _) in enumerate(DIL_PATTERNS)]
    return (hp.reshape(BATCH, SEQ, D_MODEL), hs.reshape(DEC_BATCH, 1, D_MODEL),
            ga_p.reshape(BATCH, SEQ, CONV_CH)[None, :, SEQ - n_a:],
            jnp.transpose(new_sta, (1, 0, 2))[None],
            zb_p.reshape(BATCH, SEQ, CONV_CH)[None, :, SEQ - n_b:],
            jnp.stack([sb1, zb_s], axis=1)[None],
            swa_p[None], _cache_unview(nsw, SWA_KV)[None],
            dil_p[0][None], _cache_unview(nd0, 1)[None],
            dil_p[1][None], _cache_unview(nd1, 1)[None],
            dil_p[2][None], _cache_unview(nd2, 1)[None])
```

```python
import jax
import jax.numpy as jnp
from jax import lax
from jax.experimental import pallas as pl
from jax.experimental.pallas import tpu as pltpu

F32 = jnp.float32
BF16 = jnp.bfloat16

D_MODEL = 1024
BATCH = 8
SEQ = 2048
DEC_BATCH = 128
PAST_LEN = 8192
HEAD_DIM = 64
ROT_HALF = 8
ROPE_THETA = 500000.0
D_FF = 4 * D_MODEL
EPS = 1e-6
CONV_CH = 512
CONV_WIDTH = 31
SC_WIDTH = 3
SWA_HEADS = 8
SWA_KV = 2
DIL_HEADS = 4
DIL_PATTERNS = ((128, 1), (512, 4), (2048, 16))
Q_COLS = (SWA_HEADS + 3 * DIL_HEADS) * HEAD_DIM
KV_COLS = 2 * SWA_KV * HEAD_DIM + 3 * 2 * HEAD_DIM
ATTN_COLS = Q_COLS + KV_COLS
MIX_ATTN = (SWA_HEADS + DIL_HEADS) * HEAD_DIM
LANES = 128
BLK = 128

V7X_VMEM_BYTES = 64 * 1024 * 1024
VMEM_LIMIT = V7X_VMEM_BYTES - 8 * 1024 * 1024


def _params(*sem):
    return pltpu.CompilerParams(dimension_semantics=sem, vmem_limit_bytes=VMEM_LIMIT)


def _const_spec(shape, single=False):
    zeros = (0,) * len(shape)
    if single:
        return pl.BlockSpec(shape, lambda *_: zeros, pipeline_mode=pl.Buffered(1))
    return pl.BlockSpec(shape, lambda *_: zeros)


def _rmsnorm(x, g):
    return x * lax.rsqrt(jnp.mean(x * x, axis=-1, keepdims=True) + EPS) * g


def _conv_in_body(x_ref, g_ref, w_ref, ga_ref, zb_ref, gb_ref):
    u = _rmsnorm(x_ref[...], g_ref[...]).astype(BF16)
    z = jnp.dot(u, w_ref[...], preferred_element_type=F32)
    c = CONV_CH
    ga_ref[...] = z[:, 0:c] * jax.nn.sigmoid(z[:, c:2 * c])
    zb_ref[...] = z[:, 4 * c:5 * c] * z[:, 2 * c:3 * c]
    gb_ref[...] = z[:, 3 * c:4 * c]


def _conv_in(x, g, w, tm):
    t = x.shape[0]
    row = lambda i: (i, 0)
    out = jax.ShapeDtypeStruct((t, CONV_CH), F32)
    return pl.pallas_call(
        _conv_in_body,
        grid=(t // tm,),
        in_specs=[pl.BlockSpec((tm, D_MODEL), row), _const_spec((1, D_MODEL)),
                  _const_spec((D_MODEL, 5 * CONV_CH))],
        out_specs=[pl.BlockSpec((tm, CONV_CH), row)] * 3,
        out_shape=[out] * 3,
        compiler_params=_params("arbitrary"),
        name="conv_in",
    )(x, g, w)


HALO_A = 32
HALO_B = 8
CONV_CHUNK = 128


def _layernorm_silu(c, g, b):
    mu = jnp.mean(c, axis=-1, keepdims=True)
    d = c - mu
    var = jnp.mean(d * d, axis=-1, keepdims=True)
    y = d * lax.rsqrt(var + EPS) * g + b
    return y * jax.nn.sigmoid(y)


SUBLANES = 8
CONV_TILE = 1024
CONV_SUB = 256


def _conv_a_slab(ext_a, aw_ref, r0, s):
    cols = slice(s * LANES, (s + 1) * LANES)
    out = None
    for b in range(SUBLANES):
        rows = CONV_CHUNK if b == 0 else CONV_CHUNK + SUBLANES
        yb = None
        for a in range((CONV_WIDTH + 1) // SUBLANES + 1):
            j = SUBLANES * a + b - (HALO_A - (CONV_WIDTH - 1))
            if 0 <= j < CONV_WIDTH:
                term = aw_ref[j:j + 1, cols] * ext_a[r0 + SUBLANES * a:r0 + SUBLANES * a + rows, cols]
                yb = term if yb is None else yb + term
        yb = yb[b:b + CONV_CHUNK]
        out = yb if out is None else out + yb
    return out


def _conv_layer_body(x_ref, g0_ref, win_ref, aw_ref, ab_ref, lng_ref, lnb_ref, bw_ref, wout_ref,
                     g1_ref, o_ref, sta_ref, stb_ref, a_scr, b_scr, gb_scr, mix_scr, u_scr, conv_scr):
    t = pl.program_id(1)
    sub, c = CONV_SUB, CONV_CH
    n_sub = CONV_TILE // sub

    @pl.when(t == 0)
    def _():
        a_scr[0, 0:HALO_A, :] = jnp.zeros((HALO_A, c), F32)
        b_scr[0, 0:HALO_B, :] = jnp.zeros((HALO_B, c), F32)

    def row_block(k):
        return pl.ds(k * sub, sub) if isinstance(k, int) else pl.ds(pl.multiple_of(k * sub, sub), sub)

    def project_steps(k):
        rows = row_block(k)
        nxt = (k + 1) % n_sub if isinstance(k, int) else jnp.where(k + 1 == n_sub, 0, k + 1)
        proj = lambda lo, hi: jnp.dot(u_scr[...], win_ref[:, lo * c:hi * c], preferred_element_type=F32)

        def norm():
            u_scr[...] = _rmsnorm(x_ref[rows, :], g0_ref[...]).astype(BF16)

        def mixer_a():
            za = proj(0, 2)
            ga = za[:, 0:c] * jax.nn.sigmoid(za[:, c:2 * c])
            a_scr[k, HALO_A:HALO_A + sub, :] = ga
            a_scr[nxt, 0:HALO_A, :] = ga[sub - HALO_A:sub]

        def mixer_b_in():
            zb = proj(4, 5) * proj(2, 3)
            b_scr[k, HALO_B:HALO_B + sub, :] = zb
            b_scr[nxt, 0:HALO_B, :] = zb[sub - HALO_B:sub]

        def mixer_b_gate():
            gb_scr[k] = proj(3, 4)

        return [norm, mixer_a, mixer_b_in, mixer_b_gate]

    def mix_steps(k):
        rows = row_block(k)
        a_buf, b_buf, gb_buf, mix_buf = a_scr.at[k], b_scr.at[k], gb_scr.at[k], mix_scr.at[k]
        off_b = HALO_B - (SC_WIDTH - 1)
        steps = []
        for r0 in range(0, sub, CONV_CHUNK):
            chunk = slice(r0, r0 + CONV_CHUNK)
            for s in range(c // LANES):
                def conv_slab(r0=r0, s=s, chunk=chunk):
                    conv_scr[chunk, s * LANES:(s + 1) * LANES] = _conv_a_slab(a_buf, aw_ref, r0, s)
                steps.append(conv_slab)

            def gate(r0=r0, chunk=chunk):
                ya = _layernorm_silu(conv_scr[chunk, :] + ab_ref[...], lng_ref[...], lnb_ref[...])
                cb = bw_ref[0:1, :] * b_buf[off_b + r0:off_b + r0 + CONV_CHUNK, :]
                for j in range(1, SC_WIDTH):
                    cb = cb + bw_ref[j:j + 1, :] * b_buf[off_b + r0 + j:off_b + r0 + j + CONV_CHUNK, :]
                yb = gb_buf[chunk, :] * cb
                mix_buf[chunk, :] = jnp.concatenate([ya, yb], axis=-1).astype(BF16)
            steps.append(gate)

        def out_proj():
            y = jnp.dot(mix_buf[...], wout_ref[...], preferred_element_type=F32)
            o_ref[rows, :] = x_ref[rows, :] + _rmsnorm(y, g1_ref[...])
        steps.append(out_proj)
        return steps

    def interleave(matmul_steps, vector_steps):
        per = -(-len(vector_steps) // max(len(matmul_steps), 1))
        while matmul_steps or vector_steps:
            if matmul_steps:
                matmul_steps.pop(0)()
            for _ in range(per):
                if vector_steps:
                    vector_steps.pop(0)()

    interleave(project_steps(0), [])

    def pair(k, carry):
        interleave(project_steps(k + 1), mix_steps(k))
        return carry

    lax.fori_loop(0, n_sub - 1, pair, 0)
    interleave([], mix_steps(n_sub - 1))

    sta_ref[0] = a_scr[0, 0:HALO_A, :]
    stb_ref[0] = b_scr[0, 0:HALO_B, :]


def _conv_layer(x, g0, w_in, aw, ab, lng, lnb, bw, w_out, g1):
    nt = SEQ // CONV_TILE
    n_sub = CONV_TILE // CONV_SUB
    row = lambda b, t: (b * nt + t, 0)
    seq = lambda b, t: (b, 0, 0)
    return pl.pallas_call(
        _conv_layer_body,
        grid=(BATCH, nt),
        in_specs=[pl.BlockSpec((CONV_TILE, D_MODEL), row), _const_spec((1, D_MODEL)),
                  _const_spec((D_MODEL, 5 * CONV_CH), single=True),
                  _const_spec((CONV_WIDTH, CONV_CH)), _const_spec((1, CONV_CH)),
                  _const_spec((1, CONV_CH)), _const_spec((1, CONV_CH)),
                  _const_spec((SC_WIDTH, CONV_CH)), _const_spec((D_MODEL, D_MODEL), single=True),
                  _const_spec((1, D_MODEL))],
        out_specs=[pl.BlockSpec((CONV_TILE, D_MODEL), row),
                   pl.BlockSpec((1, HALO_A, CONV_CH), seq), pl.BlockSpec((1, HALO_B, CONV_CH), seq)],
        out_shape=[jax.ShapeDtypeStruct(x.shape, F32),
                   jax.ShapeDtypeStruct((BATCH, HALO_A, CONV_CH), F32),
                   jax.ShapeDtypeStruct((BATCH, HALO_B, CONV_CH), F32)],
        scratch_shapes=[pltpu.VMEM((n_sub, halo + CONV_SUB, CONV_CH), F32) for halo in (HALO_A, HALO_B, 0)]
                       + [pltpu.VMEM((n_sub, CONV_SUB, 2 * CONV_CH), BF16),
                          pltpu.VMEM((CONV_SUB, D_MODEL), BF16), pltpu.VMEM((CONV_SUB, CONV_CH), F32)],
        compiler_params=_params("arbitrary", "arbitrary"),
        name="conv_layer",
    )(x, g0, w_in, aw, ab, lng, lnb, bw, w_out, g1)


def _conv_mix_dec_body(ga_ref, zb_ref, gb_ref, h_ref, sta_ref, sb0_ref, sb1_ref, aw_ref, ab_ref,
                       lng_ref, lnb_ref, bw_ref, w_ref, g_ref, o_ref, nsta_ref):
    ga = ga_ref[...]
    n_state = CONV_WIDTH - 1
    acc = aw_ref[n_state:n_state + 1, :] * ga
    for j in range(n_state):
        acc = acc + aw_ref[j:j + 1, :] * sta_ref[j]
    for j in range(n_state - 1):
        nsta_ref[j] = sta_ref[j + 1]
    nsta_ref[n_state - 1] = ga
    ya = _layernorm_silu(acc + ab_ref[...], lng_ref[...], lnb_ref[...])
    cb = bw_ref[0:1, :] * sb0_ref[...] + bw_ref[1:2, :] * sb1_ref[...] + bw_ref[2:3, :] * zb_ref[...]
    yb = gb_ref[...] * cb
    mix = jnp.concatenate([ya, yb], axis=-1).astype(BF16)
    y = jnp.dot(mix, w_ref[...], preferred_element_type=F32)
    o_ref[...] = h_ref[...] + _rmsnorm(y, g_ref[...])


def _conv_mix_dec(ga, zb, gb, h, sta, sb0, sb1, aw, ab, lng, lnb, bw, w, g):
    args = (ga, zb, gb, h, sta, sb0, sb1, aw, ab, lng, lnb, bw, w, g)
    return pl.pallas_call(
        _conv_mix_dec_body,
        grid=(1,),
        in_specs=[_const_spec(a.shape) for a in args],
        out_specs=[_const_spec(h.shape), _const_spec(sta.shape)],
        out_shape=[jax.ShapeDtypeStruct(h.shape, F32), jax.ShapeDtypeStruct(sta.shape, F32)],
        compiler_params=_params("arbitrary"),
        name="conv_mix_dec",
    )(*args)


FF_CHUNK = 1024
MLP_TILE = 1024


def _mlp_block(x, g2_ref, w1_ref, w2_ref, g3_ref):
    u = _rmsnorm(x, g2_ref[...]).astype(BF16)
    acc = jnp.zeros(x.shape, F32)
    for c in range(D_FF // FF_CHUNK):
        sl = slice(c * FF_CHUNK, (c + 1) * FF_CHUNK)
        hid = jnp.dot(u, w1_ref[:, sl], preferred_element_type=F32)
        hid = jnp.square(jnp.maximum(hid, 0.0)).astype(BF16)
        acc = acc + jnp.dot(hid, w2_ref[sl, :], preferred_element_type=F32)
    return x + _rmsnorm(acc, g3_ref[...])


def _mlp_body(x_ref, g2_ref, w1_ref, w2_ref, g3_ref, o_ref):
    o_ref[...] = _mlp_block(x_ref[...], g2_ref, w1_ref, w2_ref, g3_ref)


def _mix_mlp_body(mix_ref, h_ref, wo_ref, g1_ref, g2_ref, w1_ref, w2_ref, g3_ref, o_ref):
    y = jnp.dot(mix_ref[...].astype(BF16), wo_ref[...], preferred_element_type=F32)
    x = h_ref[...] + _rmsnorm(y, g1_ref[...])
    o_ref[...] = _mlp_block(x, g2_ref, w1_ref, w2_ref, g3_ref)


def _mix_mlp(mix, h, wo, g1, g2, w1, w2, g3, tm):
    t = h.shape[0]
    row = lambda i: (i, 0)
    vec = _const_spec((1, D_MODEL))
    return pl.pallas_call(
        _mix_mlp_body,
        grid=(t // tm,),
        in_specs=[pl.BlockSpec((tm, mix.shape[1]), row), pl.BlockSpec((tm, D_MODEL), row),
                  _const_spec(wo.shape, single=True), vec, vec,
                  _const_spec((D_MODEL, D_FF), single=True), _const_spec((D_FF, D_MODEL), single=True), vec],
        out_specs=pl.BlockSpec((tm, D_MODEL), row),
        out_shape=jax.ShapeDtypeStruct(h.shape, F32),
        compiler_params=_params("arbitrary"),
        name="mix_mlp",
    )(mix, h, wo, g1, g2, w1, w2, g3)


def _mlp(x, g2, w1, w2, g3, tm):
    t = x.shape[0]
    row = lambda i: (i, 0)
    return pl.pallas_call(
        _mlp_body,
        grid=(t // tm,),
        in_specs=[pl.BlockSpec((tm, D_MODEL), row), _const_spec((1, D_MODEL)),
                  _const_spec((D_MODEL, D_FF), single=True), _const_spec((D_FF, D_MODEL), single=True),
                  _const_spec((1, D_MODEL))],
        out_specs=pl.BlockSpec((tm, D_MODEL), row),
        out_shape=jax.ShapeDtypeStruct(x.shape, F32),
        compiler_params=_params("arbitrary"),
        name="mlp",
    )(x, g2, w1, w2, g3)


def _rope_slab(z, c, s_lo, s_hi):
    return z * c + pltpu.roll(z, LANES - ROT_HALF, 1) * s_lo + pltpu.roll(z, ROT_HALF, 1) * s_hi


def _attn_in_body(x_ref, g_ref, w_ref, ca_ref, sla_ref, sha_ref, cb_ref, slb_ref, shb_ref,
                  qn_ref, qd_ref, kv_ref):
    u = _rmsnorm(x_ref[...], g_ref[...]).astype(BF16)
    z = jnp.dot(u, w_ref[...], preferred_element_type=F32)
    ca, sla, sha = ca_ref[...], sla_ref[...], sha_ref[...]
    scale = HEAD_DIM ** -0.5
    for s in range(Q_COLS // LANES):
        sl = slice(s * LANES, (s + 1) * LANES)
        q = _rope_slab(z[:, sl], ca, sla, sha) * scale
        if s < QN_SLABS:
            qn_ref[:, sl] = q.astype(BF16)
        else:
            qd_ref[s - QN_SLABS] = q
    kv_ref[0] = _rope_slab(z[:, Q_COLS:Q_COLS + LANES], ca, sla, sha)
    kv_ref[1] = z[:, Q_COLS + LANES:Q_COLS + 2 * LANES]
    cb, slb, shb = cb_ref[...], slb_ref[...], shb_ref[...]
    for s in range(2, KV_SLABS):
        sl = slice(Q_COLS + s * LANES, Q_COLS + (s + 1) * LANES)
        kv_ref[s] = _rope_slab(z[:, sl], cb, slb, shb)


QN_SLABS = (SWA_HEADS + DIL_HEADS) * HEAD_DIM // LANES
QD_SLABS = Q_COLS // LANES - QN_SLABS
KV_SLABS = KV_COLS // LANES


def _attn_in(x, g, w, tabs, tm):
    t = x.shape[0]
    row = lambda i: (i, 0)
    slab = lambda i: (0, i, 0)
    nper = tabs[0].shape[0] // tm
    tab = pl.BlockSpec((tm, LANES), lambda i: (i % nper, 0))
    return pl.pallas_call(
        _attn_in_body,
        grid=(t // tm,),
        in_specs=[pl.BlockSpec((tm, D_MODEL), row), _const_spec((1, D_MODEL)),
                  _const_spec((D_MODEL, ATTN_COLS), single=True)] + [tab] * 6,
        out_specs=[pl.BlockSpec((tm, QN_SLABS * LANES), row), pl.BlockSpec((QD_SLABS, tm, LANES), slab),
                   pl.BlockSpec((KV_SLABS, tm, LANES), slab)],
        out_shape=[jax.ShapeDtypeStruct((t, QN_SLABS * LANES), BF16),
                   jax.ShapeDtypeStruct((QD_SLABS, t, LANES), F32),
                   jax.ShapeDtypeStruct((KV_SLABS, t, LANES), F32)],
        compiler_params=_params("arbitrary"),
        name="attn_in",
    )(x, g, w, *tabs)


ATTN_STEP = 4 * BLK
ATTN_LOOKAHEAD = 2
_NT = (((1,), (1,)), ((), ()))
_TN = (((0,), (0,)), ((), ()))


def _both_halves(x, in_hi):
    lane = lax.broadcasted_iota(jnp.int32, x.shape, 1)
    return jnp.where(lane >= HEAD_DIM if in_hi else lane < HEAD_DIM, x, pltpu.roll(x, HEAD_DIM, 1))


def _block_scores(q_slabs, k_tiles, k_hi, bias):
    in_a = lax.broadcasted_iota(jnp.int32, (BLK, LANES), 1) < HEAD_DIM
    kk = jnp.concatenate([_both_halves(t, k_hi) for t in k_tiles], axis=0).astype(BF16)
    zero = jnp.zeros((BLK, LANES), BF16)
    scores = []
    for qs in q_slabs:
        qq = jnp.concatenate([jnp.where(in_a, qs, zero), jnp.where(in_a, zero, qs)], axis=0)
        scores.append(lax.dot_general(kk, qq, _NT, preferred_element_type=F32) + bias)
    return scores


def _block_outputs(scores, v_tiles, v_hi, sinks):
    lane = lax.broadcasted_iota(jnp.int32, (BLK, LANES), 1)
    one_lane = 0 if v_hi else HEAD_DIM
    vv = jnp.concatenate([jnp.where(lane == one_lane, 1.0, t) for t in v_tiles], axis=0).astype(BF16)
    probs, maxes = [], []
    for slab, sh in enumerate(scores):
        m = jnp.max(sh, axis=0, keepdims=True)
        if sinks is not None:
            m = jnp.maximum(m, sinks[slab])
        probs.append(jnp.exp(sh - m).astype(BF16))
        maxes.append(m)
    v0 = HEAD_DIM if v_hi else 0
    outs = []
    for slab, p in enumerate(probs):
        ot = lax.dot_general(vv, p, _TN, preferred_element_type=F32)
        m = maxes[slab]
        den = ot[one_lane:one_lane + 1, :]
        if sinks is not None:
            den = den + jnp.exp(sinks[slab] - m)
        o_t = jnp.concatenate([ot[v0:v0 + HEAD_DIM, 0:BLK], ot[v0:v0 + HEAD_DIM, BLK:2 * BLK]], axis=0)
        tile = lambda row: jnp.concatenate([jnp.broadcast_to(row[:, 0:BLK], (HEAD_DIM, BLK)),
                                            jnp.broadcast_to(row[:, BLK:2 * BLK], (HEAD_DIM, BLK))], axis=0)
        outs.append(((o_t / tile(den)).T, tile(m + jnp.log(den)).T))
    return outs


N_RES = 4


def _attn_seq_body(qn_ref, qd_ref, kv_ref, sink_ref, mix_ref, qc1, kc1, qc2, kc2, res):
    key2 = lax.broadcasted_iota(jnp.int32, (2 * BLK, 2 * BLK), 0)
    qry2 = lax.broadcasted_iota(jnp.int32, (2 * BLK, 2 * BLK), 1) & (BLK - 1)
    slack = jnp.where(key2 < BLK, key2 - qry2, qry2 - key2 + BLK)
    in_prev = jnp.where(key2 < BLK, 4 * BLK, 0)
    neg_inf = jnp.float32(-jnp.inf)
    bias_both = jnp.where(slack >= 0, 0.0, neg_inf)
    bias_cur = bias_both[BLK:2 * BLK]

    for gi, (qc, kc) in ((1, (qc1, kc1)), (2, (qc2, kc2))):
        dil = DIL_PATTERNS[gi][1]
        n = SEQ // dil
        for r in range(dil):
            dst = slice(r * n, (r + 1) * n)
            for s in range(2):
                qc[dst, s * LANES:(s + 1) * LANES] = qd_ref[2 * (gi - 1) + s, pl.ds(r, n, stride=dil), :].astype(BF16)
            kc[dst, :] = kv_ref[2 + gi, pl.ds(r, n, stride=dil), :]

    def run(q_src, q_col0, n_slabs, k_src, v_src, n_kv, packed, chain, sinks, write):
        slabs_per_kv = n_slabs // n_kv

        def step(it, carry):
            base = pl.multiple_of(it * ATTN_STEP, ATTN_STEP)
            items = [(j, kvh) for j in range(ATTN_STEP // BLK) for kvh in range(n_kv)]

            def key_rows(j):
                r0 = pl.multiple_of(base + j * BLK, BLK)
                if chain == "block" or (chain == "step" and j == 0):
                    return r0, [pl.ds(r0, BLK)], bias_cur
                if chain == "seq" and j == 0:
                    prev = pl.ds(pl.multiple_of(jnp.maximum(r0 - BLK, 0), BLK), BLK)
                    gone = in_prev * jnp.where(it > 0, 0, 1)
                    return r0, [prev, pl.ds(r0, BLK)], jnp.where(slack - gone >= 0, 0.0, neg_inf)
                return r0, [pl.ds(pl.multiple_of(r0 - BLK, BLK), BLK), pl.ds(r0, BLK)], bias_both

            def scores_of(item):
                j, kvh = item
                r0, krows, bias = key_rows(j)
                slabs = range(kvh * slabs_per_kv, (kvh + 1) * slabs_per_kv)
                q_slabs = [q_src[pl.ds(r0, BLK), q_col0 + s * LANES:q_col0 + (s + 1) * LANES] for s in slabs]
                k_hi = (not packed) and kvh == 1
                return _block_scores(q_slabs, [k_src[r, :] for r in krows], k_hi, bias)

            def finish(item, scores):
                j, kvh = item
                r0, krows, _ = key_rows(j)
                slabs = range(kvh * slabs_per_kv, (kvh + 1) * slabs_per_kv)
                sk = None if sinks is None else [sinks[s] for s in slabs]
                v_hi = packed or kvh == 1
                outs = _block_outputs(scores, [v_src[r, :] for r in krows], v_hi, sk)
                for s, (o, lse) in zip(slabs, outs):
                    write(it, j, r0, s, o, lse)

            ahead = min(ATTN_LOOKAHEAD, len(items))
            pending = [scores_of(item) for item in items[:ahead]]
            for n, item in enumerate(items):
                if n + ahead < len(items):
                    pending.append(scores_of(items[n + ahead]))
                finish(item, pending.pop(0))
            return carry

        lax.fori_loop(0, SEQ // ATTN_STEP, step, 0)

    sinks = [jnp.concatenate([jnp.broadcast_to(sink_ref[0:1, 2 * s + half:2 * s + half + 1], (1, BLK))
                              for half in range(2)], axis=1) for s in range(SWA_HEADS // 2)]

    def write_swa(it, j, r0, s, o, lse):
        mix_ref[pl.ds(r0, BLK), s * LANES:(s + 1) * LANES] = o.astype(BF16)

    run(qn_ref, 0, SWA_HEADS // 2, kv_ref.at[0], kv_ref.at[1], SWA_KV, False, "seq", sinks, write_swa)

    def write_res(group, start_of, stride):
        def write(it, j, r0, s, o, lse):
            dst = pl.ds(start_of(it, j, r0), BLK) if stride == 1 else pl.ds(start_of(it, j, r0), BLK, stride=stride)
            res[group * N_RES + s, dst, :] = o
            res[group * N_RES + 2 + s, dst, :] = lse
        return write

    n_dil_slabs = DIL_HEADS // 2
    run(qn_ref, SWA_HEADS * HEAD_DIM, n_dil_slabs, kv_ref.at[2], kv_ref.at[2], 1, True, "seq", None,
        write_res(0, lambda it, j, r0: r0, 1))
    d1, d2 = DIL_PATTERNS[1][1], DIL_PATTERNS[2][1]
    run(qc1, 0, n_dil_slabs, kc1, kc1, 1, True, "step", None,
        write_res(1, lambda it, j, r0: d1 * j * BLK + it, d1))
    run(qc2, 0, n_dil_slabs, kc2, kc2, 1, True, "block", None,
        write_res(2, lambda it, j, r0: it * (ATTN_STEP // BLK) + j, d2))

    def merge(c, carry):
        rows = pl.ds(pl.multiple_of(c * ATTN_STEP, ATTN_STEP), ATTN_STEP)
        for s in range(n_dil_slabs):
            lse = [res[g * N_RES + 2 + s, rows, :] for g in range(3)]
            m = jnp.maximum(jnp.maximum(lse[0], lse[1]), lse[2])
            e = [jnp.exp(l - m) for l in lse]
            o = e[0] * res[s, rows, :] + e[1] * res[N_RES + s, rows, :] + e[2] * res[2 * N_RES + s, rows, :]
            col = SWA_HEADS * HEAD_DIM + s * LANES
            mix_ref[rows, col:col + LANES] = (o / (e[0] + e[1] + e[2])).astype(BF16)
        return carry

    lax.fori_loop(0, SEQ // ATTN_STEP, merge, 0)


def _attn_seq(qn, qd, kv, sinks):
    seq2 = lambda b: (b, 0)
    seq3 = lambda b: (0, b, 0)
    wd = DIL_HEADS * HEAD_DIM
    return pl.pallas_call(
        _attn_seq_body,
        grid=(BATCH,),
        in_specs=[pl.BlockSpec((SEQ, QN_SLABS * LANES), seq2), pl.BlockSpec((QD_SLABS, SEQ, LANES), seq3),
                  pl.BlockSpec((KV_SLABS, SEQ, LANES), seq3), _const_spec(sinks.shape)],
        out_specs=pl.BlockSpec((SEQ, MIX_ATTN), seq2),
        out_shape=jax.ShapeDtypeStruct((BATCH * SEQ, MIX_ATTN), BF16),
        scratch_shapes=[pltpu.VMEM((SEQ, wd), BF16), pltpu.VMEM((SEQ, LANES), F32),
                        pltpu.VMEM((SEQ, wd), BF16), pltpu.VMEM((SEQ, LANES), F32),
                        pltpu.VMEM((3 * N_RES, SEQ, LANES), F32)],
        compiler_params=_params("arbitrary"),
        name="attn_seq",
    )(qn, qd, kv, sinks)


DEC_TILE = 4
N_MIX_HEADS = SWA_HEADS + DIL_HEADS


def _attn_dec_body(q_ref, kv_ref, kv_all_ref, sink_ref, csw_ref, cd0_ref, cd1_ref, cd2_ref,
                   nsw_ref, nd0_ref, nd1_ref, nd2_ref, o_ref, kvt_scr):
    step = pl.program_id(0)
    bb, hd, grp = DEC_TILE, HEAD_DIM, DIL_HEADS
    n_rows = grp * bb

    @pl.when(step == 0)
    def _():
        kvt_scr[...] = kv_all_ref[...].T

    q = q_ref[0]
    kv = kv_ref[0]
    row_seq = lax.broadcasted_iota(jnp.int32, (n_rows, bb * hd), 0) & (bb - 1)
    col_seq = lax.broadcasted_iota(jnp.int32, (n_rows, bb * hd), 1) // hd
    diag = row_seq == col_seq
    lane = lax.broadcasted_iota(jnp.int32, (1, LANES), 1)
    nt = (((1,), (1,)), ((), ()))

    def q_rows(c0):
        return jnp.concatenate([q[:, c0 + s * hd:c0 + (s + 1) * hd] for s in range(grp)], axis=0)

    def per_row(x):
        return jnp.concatenate([x] * grp, axis=0)

    def block_diag(qr):
        return jnp.where(diag, jnp.concatenate([qr] * bb, axis=1), 0.0).astype(BF16)

    def take_diag(ob):
        ob = jnp.where(diag, ob, 0.0)
        out = ob[:, 0:hd]
        for b in range(1, bb):
            out = out + ob[:, b * hd:(b + 1) * hd]
        return out

    def shift_in(x, r0):
        slab = kvt_scr[r0:r0 + hd, :]
        new = jnp.concatenate([pltpu.roll(slab, LANES - 1 - step * bb - b, 1) for b in range(bb)], axis=0)
        n_t = x.shape[1] // LANES
        rolled = [pltpu.roll(x[:, t * LANES:(t + 1) * LANES], LANES - 1, 1) for t in range(n_t)]
        tiles = [jnp.where(lane == LANES - 1, rolled[t + 1] if t + 1 < n_t else new, rolled[t])
                 for t in range(n_t)]
        return tiles[0] if n_t == 1 else jnp.concatenate(tiles, axis=1)

    def stack(ref, idx):
        x = ref[:, idx]
        return x.reshape(bb * hd, x.shape[2])

    def unstack(x):
        return x.reshape(bb, hd, x.shape[1])

    for kvh in range(SWA_KV):
        k, v = stack(csw_ref, kvh), stack(csw_ref, SWA_KV + kvh)
        qr = q_rows(kvh * grp * hd)
        k_new = per_row(kv[:, kvh * hd:(kvh + 1) * hd])
        v_new = per_row(kv[:, LANES + kvh * hd:LANES + (kvh + 1) * hd])
        sk = jnp.concatenate([jnp.broadcast_to(sink_ref[0:1, kvh * grp + s:kvh * grp + s + 1], (bb, 1))
                              for s in range(grp)], axis=0)
        s = jnp.dot(block_diag(qr), k.astype(BF16), preferred_element_type=F32)
        s_new = jnp.sum(qr * k_new, axis=1, keepdims=True)
        m = jnp.maximum(jnp.maximum(jnp.max(s, axis=1, keepdims=True), s_new), sk)
        p = jnp.exp(s - m)
        p_new = jnp.exp(s_new - m)
        den = jnp.sum(p, axis=1, keepdims=True) + p_new + jnp.exp(sk - m)
        pv = lax.dot_general(p.astype(BF16), v.astype(BF16), nt, preferred_element_type=F32)
        o = (take_diag(pv) + p_new * v_new) / den
        for s_ in range(grp):
            o_ref[0, kvh * grp + s_] = o[s_ * bb:(s_ + 1) * bb]
        nsw_ref[:, kvh] = unstack(shift_in(k, kvh * hd))
        nsw_ref[:, SWA_KV + kvh] = unstack(shift_in(v, LANES + kvh * hd))

    cd_refs = (cd0_ref, cd1_ref, cd2_ref)
    nd_refs = (nd0_ref, nd1_ref, nd2_ref)
    scores, s_news, vs, v_news = [], [], [], []
    m = None
    for gi, (_, dil) in enumerate(DIL_PATTERNS):
        k, v = stack(cd_refs[gi], 0), stack(cd_refs[gi], 1)
        c0 = (2 + gi) * LANES
        qr = q_rows(SWA_HEADS * hd + gi * grp * hd)
        s = jnp.dot(block_diag(qr), k.astype(BF16), preferred_element_type=F32)
        pos = lax.broadcasted_iota(jnp.int32, (1, k.shape[1]), 1)
        s = jnp.where((pos & (dil - 1)) == 0, s, -jnp.inf)
        s_new = jnp.sum(qr * per_row(kv[:, c0:c0 + hd]), axis=1, keepdims=True)
        mg = jnp.maximum(jnp.max(s, axis=1, keepdims=True), s_new)
        m = mg if m is None else jnp.maximum(m, mg)
        scores.append(s), s_news.append(s_new), vs.append(v)
        v_news.append(per_row(kv[:, c0 + hd:c0 + 2 * hd]))
        nd_refs[gi][:, 0] = unstack(shift_in(k, c0))
        nd_refs[gi][:, 1] = unstack(shift_in(v, c0 + hd))
    den = jnp.zeros((n_rows, 1), F32)
    acc = jnp.zeros((n_rows, hd), F32)
    for gi in range(len(DIL_PATTERNS)):
        p = jnp.exp(scores[gi] - m)
        p_new = jnp.exp(s_news[gi] - m)
        den = den + jnp.sum(p, axis=1, keepdims=True) + p_new
        pv = lax.dot_general(p.astype(BF16), vs[gi].astype(BF16), nt, preferred_element_type=F32)
        acc = acc + take_diag(pv) + p_new * v_news[gi]
    o = acc / den
    for s_ in range(grp):
        o_ref[0, SWA_HEADS + s_] = o[s_ * bb:(s_ + 1) * bb]


def _attn_dec(q, kv, sinks, csw, cd0, cd1, cd2):
    caches = (csw, cd0, cd1, cd2)
    n_tiles = DEC_BATCH // DEC_TILE
    q3 = q.reshape(n_tiles, DEC_TILE, Q_COLS)
    kv3 = kv.reshape(n_tiles, DEC_TILE, KV_COLS)
    tile3 = lambda w: pl.BlockSpec((1, DEC_TILE, w), lambda i: (i, 0, 0))
    cspec = lambda c: pl.BlockSpec((DEC_TILE,) + c.shape[1:], lambda i: (i, 0, 0, 0))
    o_shape = (n_tiles, N_MIX_HEADS, DEC_TILE, HEAD_DIM)
    outs = pl.pallas_call(
        _attn_dec_body,
        grid=(n_tiles,),
        in_specs=[tile3(Q_COLS), tile3(KV_COLS), _const_spec(kv.shape), _const_spec(sinks.shape)]
                 + [cspec(c) for c in caches],
        out_specs=[cspec(c) for c in caches]
                  + [pl.BlockSpec((1,) + o_shape[1:], lambda i: (i, 0, 0, 0))],
        out_shape=[jax.ShapeDtypeStruct(c.shape, F32) for c in caches]
                  + [jax.ShapeDtypeStruct(o_shape, F32)],
        scratch_shapes=[pltpu.VMEM((KV_COLS, DEC_BATCH), F32)],
        compiler_params=_params("arbitrary"),
        name="attn_dec",
    )(q3, kv3, kv, sinks, *caches)
    mix = jnp.transpose(outs[4], (0, 2, 1, 3)).reshape(DEC_BATCH, MIX_ATTN)
    return outs[0], outs[1], outs[2], outs[3], mix


def _attn_weight_order(w):
    hd = HEAD_DIM
    nq, nkv = SWA_HEADS * hd, SWA_KV * hd
    base = nq + 2 * nkv
    per = (DIL_HEADS + 2) * hd
    qd = [w[:, base + g * per:base + g * per + DIL_HEADS * hd] for g in range(3)]
    kvd = [w[:, base + g * per + DIL_HEADS * hd:base + (g + 1) * per] for g in range(3)]
    return jnp.concatenate([w[:, :nq]] + qd + [w[:, nq:base]] + kvd, axis=1)


def _rope_angles(pos):
    inv = ROPE_THETA ** (-jnp.arange(ROT_HALF, dtype=F32) / ROT_HALF)
    ang = pos.astype(F32)[:, None] * inv[None, :]
    return jnp.cos(ang), jnp.sin(ang)


def _rope_tables(pos):
    cos, sin = _rope_angles(pos)
    n = pos.shape[0]
    ones = lambda w: jnp.ones((n, w), F32)
    zeros = lambda w: jnp.zeros((n, w), F32)
    rest = HEAD_DIM - 2 * ROT_HALF
    c_h = jnp.concatenate([cos, cos, ones(rest)], axis=1)
    lo_h = jnp.concatenate([-sin, zeros(ROT_HALF), zeros(rest)], axis=1)
    hi_h = jnp.concatenate([zeros(ROT_HALF), sin, zeros(rest)], axis=1)
    both = tuple(jnp.concatenate([t, t], axis=1) for t in (c_h, lo_h, hi_h))
    first = (jnp.concatenate([c_h, ones(HEAD_DIM)], axis=1),
             jnp.concatenate([lo_h, zeros(HEAD_DIM)], axis=1),
             jnp.concatenate([hi_h, zeros(HEAD_DIM)], axis=1))
    return both + first


def _cache_view(c):
    b, l, two, kv, hd = c.shape
    return jnp.transpose(c, (0, 2, 3, 4, 1)).reshape(b, two * kv, hd, l)


def _cache_unview(c, kv):
    b, _, hd, l = c.shape
    return jnp.transpose(c.reshape(b, 2, kv, hd, l), (0, 4, 1, 2, 3))


def kernel(x_prompt, x_sample, state_conv_a, state_conv_b, cache_swa_kv, cache_dil0_kv, cache_dil1_kv,
           cache_dil2_kv, norm_g, w_in_conv, conv_a_w, conv_a_b, conv_a_ln_g, conv_a_ln_b, conv_b_w,
           w_out_conv, w_in_attn, attn_sinks, w_out_attn, mlp_w1, mlp_w2):
    tm = 512
    hp = x_prompt.reshape(BATCH * SEQ, D_MODEL)
    hs = x_sample.reshape(DEC_BATCH, D_MODEL)
    g = lambda layer, i: norm_g[layer, i].reshape(1, D_MODEL)

    w_in0 = w_in_conv[0].astype(BF16)
    w_out0 = w_out_conv[0].astype(BF16)
    conv_small = (conv_a_w[0], conv_a_b, conv_a_ln_g, conv_a_ln_b, conv_b_w[0])
    hp, sta_p, stb_p = _conv_layer(hp, g(0, 0), w_in0, *conv_small, w_out0, g(0, 1))
    ga_s, zb_s, gb_s = _conv_in(hs, g(0, 0), w_in0, DEC_BATCH)
    sta = jnp.transpose(state_conv_a[0], (1, 0, 2))
    sb0, sb1 = state_conv_b[0, :, 0], state_conv_b[0, :, 1]
    hs, new_sta = _conv_mix_dec(ga_s, zb_s, gb_s, hs, sta, sb0, sb1, *conv_small, w_out0, g(0, 1))

    w1, w2 = mlp_w1.astype(BF16), mlp_w2.astype(BF16)
    hp = _mlp(hp, g(0, 2), w1[0], w2[0], g(0, 3), MLP_TILE)
    hs = _mlp(hs, g(0, 2), w1[0], w2[0], g(0, 3), DEC_BATCH)

    w_in1 = _attn_weight_order(w_in_attn[0]).astype(BF16)
    w_out1 = w_out_attn[0].astype(BF16)
    qn_p, qd_p, kv_p = _attn_in(hp, g(1, 0), w_in1, _rope_tables(jnp.arange(SEQ)), tm)
    mix_p = _attn_seq(qn_p, qd_p, kv_p, attn_sinks)

    tabs_s = _rope_tables(jnp.full((DEC_BATCH,), PAST_LEN, jnp.int32))
    qn_s, qd_s, kv_s = _attn_in(hs, g(1, 0), w_in1, tabs_s, DEC_BATCH)
    q_s = jnp.concatenate([qn_s.astype(F32)] + [qd_s[s] for s in range(QD_SLABS)], axis=1)
    kv_s = jnp.concatenate([kv_s[s] for s in range(KV_SLABS)], axis=1)
    caches = (cache_swa_kv[0], cache_dil0_kv[0], cache_dil1_kv[0], cache_dil2_kv[0])
    nsw, nd0, nd1, nd2, mix_s = _attn_dec(q_s, kv_s, attn_sinks, *[_cache_view(c) for c in caches])

    mlp1 = (g(1, 1), g(1, 2), w1[1], w2[1], g(1, 3))
    hp = _mix_mlp(mix_p, hp, w_out1, *mlp1, MLP_TILE)
    hs = _mix_mlp(mix_s, hs, w_out1, *mlp1, DEC_BATCH)

    n_a, n_b = CONV_WIDTH - 1, SC_WIDTH - 1
    kv4 = kv_p.reshape(KV_SLABS, BATCH, SEQ, LANES)
    swa_p = jnp.stack([kv4[0, :, SEQ - BLK:], kv4[1, :, SEQ - BLK:]], axis=2)
    swa_p = swa_p.reshape(BATCH, BLK, 2, SWA_KV, HEAD_DIM)
    dil_p = [kv4[2 + gi, :, SEQ - min(w, SEQ):].reshape(BATCH, min(w, SEQ), 2, 1, HEAD_DIM)
             for gi, (w, _) in enumerate(DIL_PATTERNS)]
    return (hp.reshape(BATCH, SEQ, D_MODEL), hs.reshape(DEC_BATCH, 1, D_MODEL),
            sta_p[None, :, HALO_A - n_a:],
            jnp.transpose(new_sta, (1, 0, 2))[None],
            stb_p[None, :, HALO_B - n_b:],
            jnp.stack([sb1, zb_s], axis=1)[None],
            swa_p[None], _cache_unview(nsw, SWA_KV)[None],
            dil_p[0][None], _cache_unview(nd0, 1)[None],
            dil_p[1][None], _cache_unview(nd1, 1)[None],
            dil_p[2][None], _cache_unview(nd2, 1)[None])
```

```python
import functools
import math

import jax
import jax.numpy as jnp
from jax import lax
from jax.experimental import pallas as pl
from jax.experimental.pallas import tpu as pltpu

F32 = jnp.float32
BF16 = jnp.bfloat16

D_MODEL = 1024
BATCH = 8
SEQ = 2048
DEC_BATCH = 128
PAST_LEN = 8192
HEAD_DIM = 64
ROT_HALF = 8
ROPE_THETA = 500000.0
D_FF = 4 * D_MODEL
EPS = 1e-6
CONV_CH = 512
CONV_WIDTH = 31
SC_WIDTH = 3
SWA_HEADS = 8
SWA_KV = 2
DIL_HEADS = 4
DIL_PATTERNS = ((128, 1), (512, 4), (2048, 16))
Q_COLS = (SWA_HEADS + 3 * DIL_HEADS) * HEAD_DIM
KV_COLS = 2 * SWA_KV * HEAD_DIM + 3 * 2 * HEAD_DIM
ATTN_COLS = Q_COLS + KV_COLS
MIX_ATTN = (SWA_HEADS + DIL_HEADS) * HEAD_DIM
LANES = 128
BLK = 128

V7X_VMEM_BYTES = 64 * 1024 * 1024
VMEM_LIMIT = V7X_VMEM_BYTES - 8 * 1024 * 1024


def _params(*sem):
    return pltpu.CompilerParams(dimension_semantics=sem, vmem_limit_bytes=VMEM_LIMIT)


def _const_spec(shape, single=False):
    zeros = (0,) * len(shape)
    if single:
        return pl.BlockSpec(shape, lambda *_: zeros, pipeline_mode=pl.Buffered(1))
    return pl.BlockSpec(shape, lambda *_: zeros)


def _rmsnorm(x, g):
    return x * lax.rsqrt(jnp.mean(x * x, axis=-1, keepdims=True) + EPS) * g


def _conv_in_body(x_ref, g_ref, w_ref, ga_ref, zb_ref, gb_ref):
    u = _rmsnorm(x_ref[...], g_ref[...]).astype(BF16)
    z = jnp.dot(u, w_ref[...], preferred_element_type=F32)
    c = CONV_CH
    ga_ref[...] = z[:, 0:c] * jax.nn.sigmoid(z[:, c:2 * c])
    zb_ref[...] = z[:, 4 * c:5 * c] * z[:, 2 * c:3 * c]
    gb_ref[...] = z[:, 3 * c:4 * c]


def _conv_in(x, g, w, tm):
    t = x.shape[0]
    row = lambda i: (i, 0)
    out = jax.ShapeDtypeStruct((t, CONV_CH), F32)
    return pl.pallas_call(
        _conv_in_body,
        grid=(t // tm,),
        in_specs=[pl.BlockSpec((tm, D_MODEL), row), _const_spec((1, D_MODEL)),
                  _const_spec((D_MODEL, 5 * CONV_CH))],
        out_specs=[pl.BlockSpec((tm, CONV_CH), row)] * 3,
        out_shape=[out] * 3,
        compiler_params=_params("arbitrary"),
        name="conv_in",
    )(x, g, w)


HALO_A = 32
HALO_B = 8
CONV_CHUNK = 128


def _layernorm_silu(c, g, b):
    mu = jnp.mean(c, axis=-1, keepdims=True)
    d = c - mu
    var = jnp.mean(d * d, axis=-1, keepdims=True)
    y = d * lax.rsqrt(var + EPS) * g + b
    return y * jax.nn.sigmoid(y)


SUBLANES = 8
CONV_TILE = 1024
CONV_SUB = 256


def _conv_a_slab(ext_a, aw_ref, r0, s):
    cols = slice(s * LANES, (s + 1) * LANES)
    out = None
    for b in range(SUBLANES):
        rows = CONV_CHUNK if b == 0 else CONV_CHUNK + SUBLANES
        yb = None
        for a in range((CONV_WIDTH + 1) // SUBLANES + 1):
            j = SUBLANES * a + b - (HALO_A - (CONV_WIDTH - 1))
            if 0 <= j < CONV_WIDTH:
                term = aw_ref[j:j + 1, cols] * ext_a[r0 + SUBLANES * a:r0 + SUBLANES * a + rows, cols]
                yb = term if yb is None else yb + term
        yb = yb[b:b + CONV_CHUNK]
        out = yb if out is None else out + yb
    return out


def _conv_layer_body(x_ref, g0_ref, win_ref, aw_ref, ab_ref, lng_ref, lnb_ref, bw_ref, wout_ref,
                     g1_ref, o_ref, sta_ref, stb_ref, a_scr, b_scr, gb_scr, mix_scr, u_scr, conv_scr):
    t = pl.program_id(1)
    sub, c = CONV_SUB, CONV_CH
    n_sub = CONV_TILE // sub

    @pl.when(t == 0)
    def _():
        a_scr[0, 0:HALO_A, :] = jnp.zeros((HALO_A, c), F32)
        b_scr[0, 0:HALO_B, :] = jnp.zeros((HALO_B, c), F32)

    def row_block(k):
        return pl.ds(k * sub, sub) if isinstance(k, int) else pl.ds(pl.multiple_of(k * sub, sub), sub)

    def project_steps(k):
        rows = row_block(k)
        nxt = (k + 1) % n_sub if isinstance(k, int) else jnp.where(k + 1 == n_sub, 0, k + 1)
        proj = lambda lo, hi: jnp.dot(u_scr[...], win_ref[:, lo * c:hi * c], preferred_element_type=F32)

        def norm():
            u_scr[...] = _rmsnorm(x_ref[rows, :], g0_ref[...]).astype(BF16)

        def mixer_a():
            za = proj(0, 2)
            ga = za[:, 0:c] * jax.nn.sigmoid(za[:, c:2 * c])
            a_scr[k, HALO_A:HALO_A + sub, :] = ga
            a_scr[nxt, 0:HALO_A, :] = ga[sub - HALO_A:sub]

        def mixer_b_in():
            zb = proj(4, 5) * proj(2, 3)
            b_scr[k, HALO_B:HALO_B + sub, :] = zb
            b_scr[nxt, 0:HALO_B, :] = zb[sub - HALO_B:sub]

        def mixer_b_gate():
            gb_scr[k] = proj(3, 4)

        return [norm, mixer_a, mixer_b_in, mixer_b_gate]

    def mix_steps(k):
        rows = row_block(k)
        a_buf, b_buf, gb_buf, mix_buf = a_scr.at[k], b_scr.at[k], gb_scr.at[k], mix_scr.at[k]
        off_b = HALO_B - (SC_WIDTH - 1)
        steps = []
        for r0 in range(0, sub, CONV_CHUNK):
            chunk = slice(r0, r0 + CONV_CHUNK)
            for s in range(c // LANES):
                def conv_slab(r0=r0, s=s, chunk=chunk):
                    conv_scr[chunk, s * LANES:(s + 1) * LANES] = _conv_a_slab(a_buf, aw_ref, r0, s)
                steps.append(conv_slab)

            def gate(r0=r0, chunk=chunk):
                ya = _layernorm_silu(conv_scr[chunk, :] + ab_ref[...], lng_ref[...], lnb_ref[...])
                cb = bw_ref[0:1, :] * b_buf[off_b + r0:off_b + r0 + CONV_CHUNK, :]
                for j in range(1, SC_WIDTH):
                    cb = cb + bw_ref[j:j + 1, :] * b_buf[off_b + r0 + j:off_b + r0 + j + CONV_CHUNK, :]
                yb = gb_buf[chunk, :] * cb
                mix_buf[chunk, :] = jnp.concatenate([ya, yb], axis=-1).astype(BF16)
            steps.append(gate)

        def out_proj():
            y = jnp.dot(mix_buf[...], wout_ref[...], preferred_element_type=F32)
            o_ref[rows, :] = x_ref[rows, :] + _rmsnorm(y, g1_ref[...])
        steps.append(out_proj)
        return steps

    def interleave(matmul_steps, vector_steps):
        per = -(-len(vector_steps) // max(len(matmul_steps), 1))
        while matmul_steps or vector_steps:
            if matmul_steps:
                matmul_steps.pop(0)()
            for _ in range(per):
                if vector_steps:
                    vector_steps.pop(0)()

    interleave(project_steps(0), [])
    for k in range(n_sub - 1):
        interleave(project_steps(k + 1), mix_steps(k))
    interleave([], mix_steps(n_sub - 1))

    sta_ref[0] = a_scr[0, 0:HALO_A, :]
    stb_ref[0] = b_scr[0, 0:HALO_B, :]


def _conv_layer(x, g0, w_in, aw, ab, lng, lnb, bw, w_out, g1):
    nt = SEQ // CONV_TILE
    n_sub = CONV_TILE // CONV_SUB
    row = lambda b, t: (b * nt + t, 0)
    seq = lambda b, t: (b, 0, 0)
    return pl.pallas_call(
        _conv_layer_body,
        grid=(BATCH, nt),
        in_specs=[pl.BlockSpec((CONV_TILE, D_MODEL), row), _const_spec((1, D_MODEL)),
                  _const_spec((D_MODEL, 5 * CONV_CH), single=True),
                  _const_spec((CONV_WIDTH, CONV_CH)), _const_spec((1, CONV_CH)),
                  _const_spec((1, CONV_CH)), _const_spec((1, CONV_CH)),
                  _const_spec((SC_WIDTH, CONV_CH)), _const_spec((D_MODEL, D_MODEL), single=True),
                  _const_spec((1, D_MODEL))],
        out_specs=[pl.BlockSpec((CONV_TILE, D_MODEL), row),
                   pl.BlockSpec((1, HALO_A, CONV_CH), seq), pl.BlockSpec((1, HALO_B, CONV_CH), seq)],
        out_shape=[jax.ShapeDtypeStruct(x.shape, F32),
                   jax.ShapeDtypeStruct((BATCH, HALO_A, CONV_CH), F32),
                   jax.ShapeDtypeStruct((BATCH, HALO_B, CONV_CH), F32)],
        scratch_shapes=[pltpu.VMEM((n_sub, halo + CONV_SUB, CONV_CH), F32) for halo in (HALO_A, HALO_B, 0)]
                       + [pltpu.VMEM((n_sub, CONV_SUB, 2 * CONV_CH), BF16),
                          pltpu.VMEM((CONV_SUB, D_MODEL), BF16), pltpu.VMEM((CONV_SUB, CONV_CH), F32)],
        compiler_params=_params("arbitrary", "arbitrary"),
        name="conv_layer",
    )(x, g0, w_in, aw, ab, lng, lnb, bw, w_out, g1)


def _conv_mix_dec_body(ga_ref, zb_ref, gb_ref, h_ref, sta_ref, sb0_ref, sb1_ref, aw_ref, ab_ref,
                       lng_ref, lnb_ref, bw_ref, w_ref, g_ref, o_ref, nsta_ref):
    ga = ga_ref[...]
    n_state = CONV_WIDTH - 1
    acc = aw_ref[n_state:n_state + 1, :] * ga
    for j in range(n_state):
        acc = acc + aw_ref[j:j + 1, :] * sta_ref[j]
    for j in range(n_state - 1):
        nsta_ref[j] = sta_ref[j + 1]
    nsta_ref[n_state - 1] = ga
    ya = _layernorm_silu(acc + ab_ref[...], lng_ref[...], lnb_ref[...])
    cb = bw_ref[0:1, :] * sb0_ref[...] + bw_ref[1:2, :] * sb1_ref[...] + bw_ref[2:3, :] * zb_ref[...]
    yb = gb_ref[...] * cb
    mix = jnp.concatenate([ya, yb], axis=-1).astype(BF16)
    y = jnp.dot(mix, w_ref[...], preferred_element_type=F32)
    o_ref[...] = h_ref[...] + _rmsnorm(y, g_ref[...])


def _conv_mix_dec(ga, zb, gb, h, sta, sb0, sb1, aw, ab, lng, lnb, bw, w, g):
    args = (ga, zb, gb, h, sta, sb0, sb1, aw, ab, lng, lnb, bw, w, g)
    return pl.pallas_call(
        _conv_mix_dec_body,
        grid=(1,),
        in_specs=[_const_spec(a.shape) for a in args],
        out_specs=[_const_spec(h.shape), _const_spec(sta.shape)],
        out_shape=[jax.ShapeDtypeStruct(h.shape, F32), jax.ShapeDtypeStruct(sta.shape, F32)],
        compiler_params=_params("arbitrary"),
        name="conv_mix_dec",
    )(*args)


FF_CHUNK = 1024
MLP_TILE = 1024


def _mlp_block(x, g2_ref, w1_ref, w2_ref, g3_ref):
    u = _rmsnorm(x, g2_ref[...]).astype(BF16)
    acc = jnp.zeros(x.shape, F32)
    for c in range(D_FF // FF_CHUNK):
        sl = slice(c * FF_CHUNK, (c + 1) * FF_CHUNK)
        hid = jnp.dot(u, w1_ref[:, sl], preferred_element_type=F32)
        hid = jnp.square(jnp.maximum(hid, 0.0)).astype(BF16)
        acc = acc + jnp.dot(hid, w2_ref[sl, :], preferred_element_type=F32)
    return x + _rmsnorm(acc, g3_ref[...])


def _mlp_body(xp_ref, xs_ref, g2_ref, w1_ref, w2_ref, g3_ref, op_ref, os_ref):
    i, n = pl.program_id(0), pl.num_programs(0) - 1

    @pl.when(i < n)
    def _():
        op_ref[...] = _mlp_block(xp_ref[...], g2_ref, w1_ref, w2_ref, g3_ref)

    @pl.when(i == n)
    def _():
        os_ref[...] = _mlp_block(xs_ref[...], g2_ref, w1_ref, w2_ref, g3_ref)


def _mix_mlp_body(mp_ref, ms_ref, hp_ref, hs_ref, wo_ref, g1_ref, g2_ref, w1_ref, w2_ref, g3_ref,
                  op_ref, os_ref):
    i, n = pl.program_id(0), pl.num_programs(0) - 1

    def block(mix_ref, h_ref):
        y = jnp.dot(mix_ref[...].astype(BF16), wo_ref[...], preferred_element_type=F32)
        x = h_ref[...] + _rmsnorm(y, g1_ref[...])
        return _mlp_block(x, g2_ref, w1_ref, w2_ref, g3_ref)

    @pl.when(i < n)
    def _():
        op_ref[...] = block(mp_ref, hp_ref)

    @pl.when(i == n)
    def _():
        os_ref[...] = block(ms_ref, hs_ref)


def _two_group_specs(xp, xs, tm):
    n = xp.shape[0] // tm
    prompt = pl.BlockSpec((tm, xp.shape[1]), lambda i: (jnp.minimum(i, n - 1), 0))
    return n, prompt, _const_spec(xs.shape)


def _mix_mlp(mix_p, mix_s, hp, hs, wo, g1, g2, w1, w2, g3, tm):
    n, mp_spec, ms_spec = _two_group_specs(mix_p, mix_s, tm)
    _, hp_spec, hs_spec = _two_group_specs(hp, hs, tm)
    vec = _const_spec((1, D_MODEL))
    return pl.pallas_call(
        _mix_mlp_body,
        grid=(n + 1,),
        in_specs=[mp_spec, ms_spec, hp_spec, hs_spec, _const_spec(wo.shape, single=True), vec, vec,
                  _const_spec((D_MODEL, D_FF), single=True), _const_spec((D_FF, D_MODEL), single=True), vec],
        out_specs=[hp_spec, hs_spec],
        out_shape=[jax.ShapeDtypeStruct(hp.shape, F32), jax.ShapeDtypeStruct(hs.shape, F32)],
        compiler_params=_params("arbitrary"),
        name="mix_mlp",
    )(mix_p, mix_s, hp, hs, wo, g1, g2, w1, w2, g3)


def _mlp(xp, xs, g2, w1, w2, g3, tm):
    n, p_spec, s_spec = _two_group_specs(xp, xs, tm)
    vec = _const_spec((1, D_MODEL))
    return pl.pallas_call(
        _mlp_body,
        grid=(n + 1,),
        in_specs=[p_spec, s_spec, vec,
                  _const_spec((D_MODEL, D_FF), single=True), _const_spec((D_FF, D_MODEL), single=True), vec],
        out_specs=[p_spec, s_spec],
        out_shape=[jax.ShapeDtypeStruct(xp.shape, F32), jax.ShapeDtypeStruct(xs.shape, F32)],
        compiler_params=_params("arbitrary"),
        name="mlp",
    )(xp, xs, g2, w1, w2, g3)


def _rope_slab(z, c, s_lo, s_hi):
    return z * c + pltpu.roll(z, LANES - ROT_HALF, 1) * s_lo + pltpu.roll(z, ROT_HALF, 1) * s_hi


def _attn_in_body(x_ref, g_ref, w_ref, ca_ref, sla_ref, sha_ref, cb_ref, slb_ref, shb_ref,
                  qn_ref, qd_ref, kv_ref, *, scale):
    u = _rmsnorm(x_ref[...], g_ref[...]).astype(BF16)
    z = jnp.dot(u, w_ref[...], preferred_element_type=F32)
    ca, sla, sha = ca_ref[...], sla_ref[...], sha_ref[...]
    for s in range(Q_COLS // LANES):
        sl = slice(s * LANES, (s + 1) * LANES)
        q = _rope_slab(z[:, sl], ca, sla, sha) * scale
        if s < QN_SLABS:
            qn_ref[:, sl] = q.astype(BF16)
        else:
            qd_ref[s - QN_SLABS] = q
    kv_ref[0] = _rope_slab(z[:, Q_COLS:Q_COLS + LANES], ca, sla, sha)
    kv_ref[1] = z[:, Q_COLS + LANES:Q_COLS + 2 * LANES]
    cb, slb, shb = cb_ref[...], slb_ref[...], shb_ref[...]
    for s in range(2, KV_SLABS):
        sl = slice(Q_COLS + s * LANES, Q_COLS + (s + 1) * LANES)
        kv_ref[s] = _rope_slab(z[:, sl], cb, slb, shb)


QN_SLABS = (SWA_HEADS + DIL_HEADS) * HEAD_DIM // LANES
QD_SLABS = Q_COLS // LANES - QN_SLABS
KV_SLABS = KV_COLS // LANES


def _attn_in(x, g, w, tabs, tm, scale):
    t = x.shape[0]
    row = lambda i: (i, 0)
    slab = lambda i: (0, i, 0)
    nper = tabs[0].shape[0] // tm
    tab = pl.BlockSpec((tm, LANES), lambda i: (i % nper, 0))
    return pl.pallas_call(
        functools.partial(_attn_in_body, scale=scale),
        grid=(t // tm,),
        in_specs=[pl.BlockSpec((tm, D_MODEL), row), _const_spec((1, D_MODEL)),
                  _const_spec((D_MODEL, ATTN_COLS), single=True)] + [tab] * 6,
        out_specs=[pl.BlockSpec((tm, QN_SLABS * LANES), row), pl.BlockSpec((QD_SLABS, tm, LANES), slab),
                   pl.BlockSpec((KV_SLABS, tm, LANES), slab)],
        out_shape=[jax.ShapeDtypeStruct((t, QN_SLABS * LANES), BF16),
                   jax.ShapeDtypeStruct((QD_SLABS, t, LANES), F32),
                   jax.ShapeDtypeStruct((KV_SLABS, t, LANES), F32)],
        compiler_params=_params("arbitrary"),
        name="attn_in",
    )(x, g, w, *tabs)


LOG2E = math.log2(math.e)
ATTN_STEP = 4 * BLK
ATTN_LOOKAHEAD = 2
_NT = (((1,), (1,)), ((), ()))
_TN = (((0,), (0,)), ((), ()))


def _both_halves(x, in_hi):
    lane = lax.broadcasted_iota(jnp.int32, x.shape, 1)
    return jnp.where(lane >= HEAD_DIM if in_hi else lane < HEAD_DIM, x, pltpu.roll(x, HEAD_DIM, 1))


def _block_scores(q_slabs, k_tiles, k_hi, bias):
    in_a = lax.broadcasted_iota(jnp.int32, (BLK, LANES), 1) < HEAD_DIM
    kk = jnp.concatenate([_both_halves(t, k_hi) for t in k_tiles], axis=0).astype(BF16)
    zero = jnp.zeros((BLK, LANES), BF16)
    scores = []
    for qs in q_slabs:
        qq = jnp.concatenate([jnp.where(in_a, qs, zero), jnp.where(in_a, zero, qs)], axis=0)
        scores.append(lax.dot_general(kk, qq, _NT, preferred_element_type=F32) + bias)
    return scores


def _block_outputs(scores, v_tiles, v_hi, sinks):
    lane = lax.broadcasted_iota(jnp.int32, (BLK, LANES), 1)
    one_lane = 0 if v_hi else HEAD_DIM
    vv = jnp.concatenate([jnp.where(lane == one_lane, 1.0, t) for t in v_tiles], axis=0).astype(BF16)
    probs, maxes = [], []
    for slab, sh in enumerate(scores):
        m = jnp.max(sh, axis=0, keepdims=True)
        if sinks is not None:
            m = jnp.maximum(m, sinks[slab])
        probs.append(jnp.exp2(sh - m).astype(BF16))
        maxes.append(m)
    v0 = HEAD_DIM if v_hi else 0
    outs = []
    for slab, p in enumerate(probs):
        ot = lax.dot_general(vv, p, _TN, preferred_element_type=F32)
        m = maxes[slab]
        den = ot[one_lane:one_lane + 1, :]
        if sinks is not None:
            den = den + jnp.exp2(sinks[slab] - m)
        o_t = jnp.concatenate([ot[v0:v0 + HEAD_DIM, 0:BLK], ot[v0:v0 + HEAD_DIM, BLK:2 * BLK]], axis=0)
        tile = lambda row: jnp.concatenate([jnp.broadcast_to(row[:, 0:BLK], (HEAD_DIM, BLK)),
                                            jnp.broadcast_to(row[:, BLK:2 * BLK], (HEAD_DIM, BLK))], axis=0)
        outs.append(((o_t / tile(den)).T, tile(m + jnp.log2(den)).T))
    return outs


N_RES = 4


def _attn_seq_body(qn_ref, qd_ref, kv_ref, sink_ref, mix_ref, qc1, kc1, qc2, kc2, res):
    key2 = lax.broadcasted_iota(jnp.int32, (2 * BLK, 2 * BLK), 0)
    qry2 = lax.broadcasted_iota(jnp.int32, (2 * BLK, 2 * BLK), 1) & (BLK - 1)
    slack = jnp.where(key2 < BLK, key2 - qry2, qry2 - key2 + BLK)
    in_prev = jnp.where(key2 < BLK, 4 * BLK, 0)
    neg_inf = jnp.float32(-jnp.inf)
    bias_both = jnp.where(slack >= 0, 0.0, neg_inf)
    bias_cur = bias_both[BLK:2 * BLK]

    for gi, (qc, kc) in ((1, (qc1, kc1)), (2, (qc2, kc2))):
        dil = DIL_PATTERNS[gi][1]
        n = SEQ // dil
        for r in range(dil):
            dst = slice(r * n, (r + 1) * n)
            for s in range(2):
                qc[dst, s * LANES:(s + 1) * LANES] = qd_ref[2 * (gi - 1) + s, pl.ds(r, n, stride=dil), :].astype(BF16)
            kc[dst, :] = kv_ref[2 + gi, pl.ds(r, n, stride=dil), :]

    def run(q_src, q_col0, n_slabs, k_src, v_src, n_kv, packed, chain, sinks, write):
        slabs_per_kv = n_slabs // n_kv

        def step(it, carry):
            base = pl.multiple_of(it * ATTN_STEP, ATTN_STEP)
            items = [(j, kvh) for j in range(ATTN_STEP // BLK) for kvh in range(n_kv)]

            def key_rows(j):
                r0 = pl.multiple_of(base + j * BLK, BLK)
                if chain == "block" or (chain == "step" and j == 0):
                    return r0, [pl.ds(r0, BLK)], bias_cur
                if chain == "seq" and j == 0:
                    prev = pl.ds(pl.multiple_of(jnp.maximum(r0 - BLK, 0), BLK), BLK)
                    gone = in_prev * jnp.where(it > 0, 0, 1)
                    return r0, [prev, pl.ds(r0, BLK)], jnp.where(slack - gone >= 0, 0.0, neg_inf)
                return r0, [pl.ds(pl.multiple_of(r0 - BLK, BLK), BLK), pl.ds(r0, BLK)], bias_both

            def scores_of(item):
                j, kvh = item
                r0, krows, bias = key_rows(j)
                slabs = range(kvh * slabs_per_kv, (kvh + 1) * slabs_per_kv)
                q_slabs = [q_src[pl.ds(r0, BLK), q_col0 + s * LANES:q_col0 + (s + 1) * LANES] for s in slabs]
                k_hi = (not packed) and kvh == 1
                return _block_scores(q_slabs, [k_src[r, :] for r in krows], k_hi, bias)

            def finish(item, scores):
                j, kvh = item
                r0, krows, _ = key_rows(j)
                slabs = range(kvh * slabs_per_kv, (kvh + 1) * slabs_per_kv)
                sk = None if sinks is None else [sinks[s] for s in slabs]
                v_hi = packed or kvh == 1
                outs = _block_outputs(scores, [v_src[r, :] for r in krows], v_hi, sk)
                for s, (o, lse) in zip(slabs, outs):
                    write(it, j, r0, s, o, lse)

            ahead = min(ATTN_LOOKAHEAD, len(items))
            pending = [scores_of(item) for item in items[:ahead]]
            for n, item in enumerate(items):
                if n + ahead < len(items):
                    pending.append(scores_of(items[n + ahead]))
                finish(item, pending.pop(0))
            return carry

        lax.fori_loop(0, SEQ // ATTN_STEP, step, 0)

    sinks = [jnp.concatenate([jnp.broadcast_to(sink_ref[0:1, 2 * s + half:2 * s + half + 1] * LOG2E, (1, BLK))
                              for half in range(2)], axis=1) for s in range(SWA_HEADS // 2)]

    def write_swa(it, j, r0, s, o, lse):
        mix_ref[pl.ds(r0, BLK), s * LANES:(s + 1) * LANES] = o.astype(BF16)

    run(qn_ref, 0, SWA_HEADS // 2, kv_ref.at[0], kv_ref.at[1], SWA_KV, False, "seq", sinks, write_swa)

    def write_res(group, start_of, stride):
        def write(it, j, r0, s, o, lse):
            dst = pl.ds(start_of(it, j, r0), BLK) if stride == 1 else pl.ds(start_of(it, j, r0), BLK, stride=stride)
            res[group * N_RES + s, dst, :] = o
            res[group * N_RES + 2 + s, dst, :] = lse
        return write

    n_dil_slabs = DIL_HEADS // 2
    run(qn_ref, SWA_HEADS * HEAD_DIM, n_dil_slabs, kv_ref.at[2], kv_ref.at[2], 1, True, "seq", None,
        write_res(0, lambda it, j, r0: r0, 1))
    d1, d2 = DIL_PATTERNS[1][1], DIL_PATTERNS[2][1]
    run(qc1, 0, n_dil_slabs, kc1, kc1, 1, True, "step", None,
        write_res(1, lambda it, j, r0: d1 * j * BLK + it, d1))
    run(qc2, 0, n_dil_slabs, kc2, kc2, 1, True, "block", None,
        write_res(2, lambda it, j, r0: it * (ATTN_STEP // BLK) + j, d2))

    def merge(c, carry):
        rows = pl.ds(pl.multiple_of(c * ATTN_STEP, ATTN_STEP), ATTN_STEP)
        for s in range(n_dil_slabs):
            lse = [res[g * N_RES + 2 + s, rows, :] for g in range(3)]
            m = jnp.maximum(jnp.maximum(lse[0], lse[1]), lse[2])
            e = [jnp.exp2(l - m) for l in lse]
            o = e[0] * res[s, rows, :] + e[1] * res[N_RES + s, rows, :] + e[2] * res[2 * N_RES + s, rows, :]
            col = SWA_HEADS * HEAD_DIM + s * LANES
            mix_ref[rows, col:col + LANES] = (o / (e[0] + e[1] + e[2])).astype(BF16)
        return carry

    lax.fori_loop(0, SEQ // ATTN_STEP, merge, 0)


def _attn_seq(qn, qd, kv, sinks):
    seq2 = lambda b: (b, 0)
    seq3 = lambda b: (0, b, 0)
    wd = DIL_HEADS * HEAD_DIM
    return pl.pallas_call(
        _attn_seq_body,
        grid=(BATCH,),
        in_specs=[pl.BlockSpec((SEQ, QN_SLABS * LANES), seq2), pl.BlockSpec((QD_SLABS, SEQ, LANES), seq3),
                  pl.BlockSpec((KV_SLABS, SEQ, LANES), seq3), _const_spec(sinks.shape)],
        out_specs=pl.BlockSpec((SEQ, MIX_ATTN), seq2),
        out_shape=jax.ShapeDtypeStruct((BATCH * SEQ, MIX_ATTN), BF16),
        scratch_shapes=[pltpu.VMEM((SEQ, wd), BF16), pltpu.VMEM((SEQ, LANES), F32),
                        pltpu.VMEM((SEQ, wd), BF16), pltpu.VMEM((SEQ, LANES), F32),
                        pltpu.VMEM((3 * N_RES, SEQ, LANES), F32)],
        compiler_params=_params("arbitrary"),
        name="attn_seq",
    )(qn, qd, kv, sinks)


DEC_TILE = 4
N_MIX_HEADS = SWA_HEADS + DIL_HEADS


def _attn_dec_body(q_ref, kv_ref, kv_all_ref, sink_ref, csw_ref, cd0_ref, cd1_ref, cd2_ref,
                   nsw_ref, nd0_ref, nd1_ref, nd2_ref, o_ref, kvt_scr):
    step = pl.program_id(0)
    bb, hd, grp = DEC_TILE, HEAD_DIM, DIL_HEADS
    n_rows = grp * bb

    @pl.when(step == 0)
    def _():
        kvt_scr[...] = kv_all_ref[...].T

    q = q_ref[0]
    kv = kv_ref[0]
    row_seq = lax.broadcasted_iota(jnp.int32, (n_rows, bb * hd), 0) & (bb - 1)
    col_seq = lax.broadcasted_iota(jnp.int32, (n_rows, bb * hd), 1) // hd
    diag = row_seq == col_seq
    lane = lax.broadcasted_iota(jnp.int32, (1, LANES), 1)
    nt = (((1,), (1,)), ((), ()))

    def q_rows(c0):
        return jnp.concatenate([q[:, c0 + s * hd:c0 + (s + 1) * hd] for s in range(grp)], axis=0)

    def per_row(x):
        return jnp.concatenate([x] * grp, axis=0)

    def block_diag(qr):
        return jnp.where(diag, jnp.concatenate([qr] * bb, axis=1), 0.0).astype(BF16)

    def take_diag(ob):
        ob = jnp.where(diag, ob, 0.0)
        out = ob[:, 0:hd]
        for b in range(1, bb):
            out = out + ob[:, b * hd:(b + 1) * hd]
        return out

    def shift_in(x, r0):
        slab = kvt_scr[r0:r0 + hd, :]
        new = jnp.concatenate([pltpu.roll(slab, LANES - 1 - step * bb - b, 1) for b in range(bb)], axis=0)
        n_t = x.shape[1] // LANES
        rolled = [pltpu.roll(x[:, t * LANES:(t + 1) * LANES], LANES - 1, 1) for t in range(n_t)]
        tiles = [jnp.where(lane == LANES - 1, rolled[t + 1] if t + 1 < n_t else new, rolled[t])
                 for t in range(n_t)]
        return tiles[0] if n_t == 1 else jnp.concatenate(tiles, axis=1)

    def stack(ref, idx):
        x = ref[:, idx]
        return x.reshape(bb * hd, x.shape[2])

    def unstack(x):
        return x.reshape(bb, hd, x.shape[1])

    cd_refs = (cd0_ref, cd1_ref, cd2_ref)
    nd_refs = (nd0_ref, nd1_ref, nd2_ref)
    n_dil = len(DIL_PATTERNS)
    jobs = [dict(src=csw_ref, dst=nsw_ref, ki=kvh, vi=SWA_KV + kvh, q0=kvh * grp * hd,
                 kc=kvh * hd, vc=LANES + kvh * hd, dil=1) for kvh in range(SWA_KV)]
    jobs += [dict(src=cd_refs[gi], dst=nd_refs[gi], ki=0, vi=1, q0=SWA_HEADS * hd + gi * grp * hd,
                  kc=(2 + gi) * LANES, vc=(2 + gi) * LANES + hd, dil=DIL_PATTERNS[gi][1]) for gi in range(n_dil)]

    for jb in jobs:
        k = stack(jb["src"], jb["ki"])
        qr = q_rows(jb["q0"])
        s = jnp.dot(block_diag(qr), k.astype(BF16), preferred_element_type=F32)
        if jb["dil"] > 1:
            pos = lax.broadcasted_iota(jnp.int32, (1, k.shape[1]), 1)
            s = jnp.where((pos & (jb["dil"] - 1)) == 0, s, -jnp.inf)
        jb["s"] = s
        jb["s_new"] = jnp.sum(qr * per_row(kv[:, jb["kc"]:jb["kc"] + hd]), axis=1, keepdims=True)
        jb["v_new"] = per_row(kv[:, jb["vc"]:jb["vc"] + hd])
        jb["m"] = jnp.maximum(jnp.max(s, axis=1, keepdims=True), jb["s_new"])

    m_dil = jobs[SWA_KV]["m"]
    for jb in jobs[SWA_KV + 1:]:
        m_dil = jnp.maximum(m_dil, jb["m"])
    for kvh, jb in enumerate(jobs):
        if kvh < SWA_KV:
            sk = jnp.concatenate([jnp.broadcast_to(sink_ref[0:1, kvh * grp + s:kvh * grp + s + 1], (bb, 1))
                                  for s in range(grp)], axis=0)
            m = jnp.maximum(jb["m"], sk)
            jb["extra"] = jnp.exp(sk - m)
        else:
            m = m_dil
            jb["extra"] = 0.0
        p = jnp.exp(jb["s"] - m)
        jb["p_new"] = jnp.exp(jb["s_new"] - m)
        jb["den"] = jnp.sum(p, axis=1, keepdims=True) + jb["p_new"] + jb["extra"]
        jb["p"] = p.astype(BF16)

    for jb in jobs:
        v = stack(jb["src"], jb["vi"])
        pv = lax.dot_general(jb["p"], v.astype(BF16), nt, preferred_element_type=F32)
        jb["acc"] = take_diag(pv) + jb["p_new"] * jb["v_new"]

    for kvh in range(SWA_KV):
        o = jobs[kvh]["acc"] / jobs[kvh]["den"]
        for s_ in range(grp):
            o_ref[0, kvh * grp + s_] = o[s_ * bb:(s_ + 1) * bb]
    dil_jobs = jobs[SWA_KV:]
    o = sum(jb["acc"] for jb in dil_jobs[1:]) + dil_jobs[0]["acc"]
    o = o / (sum(jb["den"] for jb in dil_jobs[1:]) + dil_jobs[0]["den"])
    for s_ in range(grp):
        o_ref[0, SWA_HEADS + s_] = o[s_ * bb:(s_ + 1) * bb]

    for jb in jobs:
        jb["dst"][:, jb["ki"]] = unstack(shift_in(stack(jb["src"], jb["ki"]), jb["kc"]))
        jb["dst"][:, jb["vi"]] = unstack(shift_in(stack(jb["src"], jb["vi"]), jb["vc"]))


def _attn_dec(q, kv, sinks, csw, cd0, cd1, cd2):
    caches = (csw, cd0, cd1, cd2)
    n_tiles = DEC_BATCH // DEC_TILE
    q3 = q.reshape(n_tiles, DEC_TILE, Q_COLS)
    kv3 = kv.reshape(n_tiles, DEC_TILE, KV_COLS)
    tile3 = lambda w: pl.BlockSpec((1, DEC_TILE, w), lambda i: (i, 0, 0))
    cspec = lambda c: pl.BlockSpec((DEC_TILE,) + c.shape[1:], lambda i: (i, 0, 0, 0))
    o_shape = (n_tiles, N_MIX_HEADS, DEC_TILE, HEAD_DIM)
    outs = pl.pallas_call(
        _attn_dec_body,
        grid=(n_tiles,),
        in_specs=[tile3(Q_COLS), tile3(KV_COLS), _const_spec(kv.shape), _const_spec(sinks.shape)]
                 + [cspec(c) for c in caches],
        out_specs=[cspec(c) for c in caches]
                  + [pl.BlockSpec((1,) + o_shape[1:], lambda i: (i, 0, 0, 0))],
        out_shape=[jax.ShapeDtypeStruct(c.shape, F32) for c in caches]
                  + [jax.ShapeDtypeStruct(o_shape, F32)],
        scratch_shapes=[pltpu.VMEM((KV_COLS, DEC_BATCH), F32)],
        compiler_params=_params("arbitrary"),
        name="attn_dec",
    )(q3, kv3, kv, sinks, *caches)
    mix = jnp.transpose(outs[4], (0, 2, 1, 3)).reshape(DEC_BATCH, MIX_ATTN)
    return outs[0], outs[1], outs[2], outs[3], mix


def _attn_weight_order(w):
    hd = HEAD_DIM
    nq, nkv = SWA_HEADS * hd, SWA_KV * hd
    base = nq + 2 * nkv
    per = (DIL_HEADS + 2) * hd
    qd = [w[:, base + g * per:base + g * per + DIL_HEADS * hd] for g in range(3)]
    kvd = [w[:, base + g * per + DIL_HEADS * hd:base + (g + 1) * per] for g in range(3)]
    return jnp.concatenate([w[:, :nq]] + qd + [w[:, nq:base]] + kvd, axis=1)


def _rope_angles(pos):
    inv = ROPE_THETA ** (-jnp.arange(ROT_HALF, dtype=F32) / ROT_HALF)
    ang = pos.astype(F32)[:, None] * inv[None, :]
    return jnp.cos(ang), jnp.sin(ang)


def _rope_tables(pos):
    cos, sin = _rope_angles(pos)
    n = pos.shape[0]
    ones = lambda w: jnp.ones((n, w), F32)
    zeros = lambda w: jnp.zeros((n, w), F32)
    rest = HEAD_DIM - 2 * ROT_HALF
    c_h = jnp.concatenate([cos, cos, ones(rest)], axis=1)
    lo_h = jnp.concatenate([-sin, zeros(ROT_HALF), zeros(rest)], axis=1)
    hi_h = jnp.concatenate([zeros(ROT_HALF), sin, zeros(rest)], axis=1)
    both = tuple(jnp.concatenate([t, t], axis=1) for t in (c_h, lo_h, hi_h))
    first = (jnp.concatenate([c_h, ones(HEAD_DIM)], axis=1),
             jnp.concatenate([lo_h, zeros(HEAD_DIM)], axis=1),
             jnp.concatenate([hi_h, zeros(HEAD_DIM)], axis=1))
    return both + first


def _cache_view(c):
    b, l, two, kv, hd = c.shape
    return jnp.transpose(c, (0, 2, 3, 4, 1)).reshape(b, two * kv, hd, l)


def _cache_unview(c, kv):
    b, _, hd, l = c.shape
    return jnp.transpose(c.reshape(b, 2, kv, hd, l), (0, 4, 1, 2, 3))


def kernel(x_prompt, x_sample, state_conv_a, state_conv_b, cache_swa_kv, cache_dil0_kv, cache_dil1_kv,
           cache_dil2_kv, norm_g, w_in_conv, conv_a_w, conv_a_b, conv_a_ln_g, conv_a_ln_b, conv_b_w,
           w_out_conv, w_in_attn, attn_sinks, w_out_attn, mlp_w1, mlp_w2):
    tm = 512
    hp = x_prompt.reshape(BATCH * SEQ, D_MODEL)
    hs = x_sample.reshape(DEC_BATCH, D_MODEL)
    g = lambda layer, i: norm_g[layer, i].reshape(1, D_MODEL)

    w_in0 = w_in_conv[0].astype(BF16)
    w_out0 = w_out_conv[0].astype(BF16)
    conv_small = (conv_a_w[0], conv_a_b, conv_a_ln_g, conv_a_ln_b, conv_b_w[0])
    hp, sta_p, stb_p = _conv_layer(hp, g(0, 0), w_in0, *conv_small, w_out0, g(0, 1))
    ga_s, zb_s, gb_s = _conv_in(hs, g(0, 0), w_in0, DEC_BATCH)
    sta = jnp.transpose(state_conv_a[0], (1, 0, 2))
    sb0, sb1 = state_conv_b[0, :, 0], state_conv_b[0, :, 1]
    hs, new_sta = _conv_mix_dec(ga_s, zb_s, gb_s, hs, sta, sb0, sb1, *conv_small, w_out0, g(0, 1))

    w1, w2 = mlp_w1.astype(BF16), mlp_w2.astype(BF16)
    hp, hs = _mlp(hp, hs, g(0, 2), w1[0], w2[0], g(0, 3), MLP_TILE)

    w_in1 = _attn_weight_order(w_in_attn[0]).astype(BF16)
    w_out1 = w_out_attn[0].astype(BF16)
    softmax_scale = HEAD_DIM ** -0.5
    qn_p, qd_p, kv_p = _attn_in(hp, g(1, 0), w_in1, _rope_tables(jnp.arange(SEQ)), tm,
                                softmax_scale * LOG2E)
    mix_p = _attn_seq(qn_p, qd_p, kv_p, attn_sinks)

    tabs_s = _rope_tables(jnp.full((DEC_BATCH,), PAST_LEN, jnp.int32))
    qn_s, qd_s, kv_s = _attn_in(hs, g(1, 0), w_in1, tabs_s, DEC_BATCH, softmax_scale)
    q_s = jnp.concatenate([qn_s.astype(F32)] + [qd_s[s] for s in range(QD_SLABS)], axis=1)
    kv_s = jnp.concatenate([kv_s[s] for s in range(KV_SLABS)], axis=1)
    caches = (cache_swa_kv[0], cache_dil0_kv[0], cache_dil1_kv[0], cache_dil2_kv[0])
    nsw, nd0, nd1, nd2, mix_s = _attn_dec(q_s, kv_s, attn_sinks, *[_cache_view(c) for c in caches])

    hp, hs = _mix_mlp(mix_p, mix_s, hp, hs, w_out1, g(1, 1), g(1, 2), w1[1], w2[1], g(1, 3), MLP_TILE)

    n_a, n_b = CONV_WIDTH - 1, SC_WIDTH - 1
    kv4 = kv_p.reshape(KV_SLABS, BATCH, SEQ, LANES)
    swa_p = jnp.stack([kv4[0, :, SEQ - BLK:], kv4[1, :, SEQ - BLK:]], axis=2)
    swa_p = swa_p.reshape(BATCH, BLK, 2, SWA_KV, HEAD_DIM)
    dil_p = [kv4[2 + gi, :, SEQ - min(w, SEQ):].reshape(BATCH, min(w, SEQ), 2, 1, HEAD_DIM)
             for gi, (w, _) in enumerate(DIL_PATTERNS)]
    return (hp.reshape(BATCH, SEQ, D_MODEL), hs.reshape(DEC_BATCH, 1, D_MODEL),
            sta_p[None, :, HALO_A - n_a:],
            jnp.transpose(new_sta, (1, 0, 2))[None],
            stb_p[None, :, HALO_B - n_b:],
            jnp.stack([sb1, zb_s], axis=1)[None],
            swa_p[None], _cache_unview(nsw, SWA_KV)[None],
            dil_p[0][None], _cache_unview(nd0, 1)[None],
            dil_p[1][None], _cache_unview(nd1, 1)[None],
            dil_p[2][None], _cache_unview(nd2, 1)[None])
```

```python
import functools
import math

import jax
import jax.numpy as jnp
from jax import lax
from jax.experimental import pallas as pl
from jax.experimental.pallas import tpu as pltpu

F32 = jnp.float32
BF16 = jnp.bfloat16

D_MODEL = 1024
BATCH = 8
SEQ = 2048
DEC_BATCH = 128
PAST_LEN = 8192
HEAD_DIM = 64
ROT_HALF = 8
ROPE_THETA = 500000.0
D_FF = 4 * D_MODEL
EPS = 1e-6
CONV_CH = 512
CONV_WIDTH = 31
SC_WIDTH = 3
SWA_HEADS = 8
SWA_KV = 2
DIL_HEADS = 4
DIL_PATTERNS = ((128, 1), (512, 4), (2048, 16))
Q_COLS = (SWA_HEADS + 3 * DIL_HEADS) * HEAD_DIM
KV_COLS = 2 * SWA_KV * HEAD_DIM + 3 * 2 * HEAD_DIM
ATTN_COLS = Q_COLS + KV_COLS
MIX_ATTN = (SWA_HEADS + DIL_HEADS) * HEAD_DIM
LANES = 128
BLK = 128

V7X_VMEM_BYTES = 64 * 1024 * 1024
VMEM_LIMIT = V7X_VMEM_BYTES - 8 * 1024 * 1024


def _params(*sem):
    return pltpu.CompilerParams(dimension_semantics=sem, vmem_limit_bytes=VMEM_LIMIT)


def _const_spec(shape, single=False):
    zeros = (0,) * len(shape)
    if single:
        return pl.BlockSpec(shape, lambda *_: zeros, pipeline_mode=pl.Buffered(1))
    return pl.BlockSpec(shape, lambda *_: zeros)


def _rmsnorm(x, g):
    return x * lax.rsqrt(jnp.mean(x * x, axis=-1, keepdims=True) + EPS) * g


def _conv_in_body(x_ref, g_ref, w_ref, ga_ref, zb_ref, gb_ref):
    u = _rmsnorm(x_ref[...], g_ref[...]).astype(BF16)
    z = jnp.dot(u, w_ref[...], preferred_element_type=F32)
    c = CONV_CH
    ga_ref[...] = z[:, 0:c] * jax.nn.sigmoid(z[:, c:2 * c])
    zb_ref[...] = z[:, 4 * c:5 * c] * z[:, 2 * c:3 * c]
    gb_ref[...] = z[:, 3 * c:4 * c]


def _conv_in(x, g, w, tm):
    t = x.shape[0]
    row = lambda i: (i, 0)
    out = jax.ShapeDtypeStruct((t, CONV_CH), F32)
    return pl.pallas_call(
        _conv_in_body,
        grid=(t // tm,),
        in_specs=[pl.BlockSpec((tm, D_MODEL), row), _const_spec((1, D_MODEL)),
                  _const_spec((D_MODEL, 5 * CONV_CH))],
        out_specs=[pl.BlockSpec((tm, CONV_CH), row)] * 3,
        out_shape=[out] * 3,
        compiler_params=_params("arbitrary"),
        name="conv_in",
    )(x, g, w)


HALO_A = 32
HALO_B = 8
CONV_CHUNK = 128


def _layernorm_silu(c, g, b):
    mu = jnp.mean(c, axis=-1, keepdims=True)
    d = c - mu
    var = jnp.mean(d * d, axis=-1, keepdims=True)
    y = d * lax.rsqrt(var + EPS) * g + b
    return y * jax.nn.sigmoid(y)


SUBLANES = 8
CONV_TILE = 1024
CONV_SUB = 256


def _conv_a_slab(ext_a, aw_ref, r0, s):
    cols = slice(s * LANES, (s + 1) * LANES)
    out = None
    for b in range(SUBLANES):
        rows = CONV_CHUNK if b == 0 else CONV_CHUNK + SUBLANES
        yb = None
        for a in range((CONV_WIDTH + 1) // SUBLANES + 1):
            j = SUBLANES * a + b - (HALO_A - (CONV_WIDTH - 1))
            if 0 <= j < CONV_WIDTH:
                term = aw_ref[j:j + 1, cols] * ext_a[r0 + SUBLANES * a:r0 + SUBLANES * a + rows, cols]
                yb = term if yb is None else yb + term
        yb = yb[b:b + CONV_CHUNK]
        out = yb if out is None else out + yb
    return out


def _conv_layer_body(x_ref, g0_ref, win_ref, aw_ref, ab_ref, lng_ref, lnb_ref, bw_ref, wout_ref,
                     g1_ref, o_ref, sta_ref, stb_ref, a_scr, b_scr, gb_scr, mix_scr, u_scr, conv_scr):
    t = pl.program_id(1)
    sub, c = CONV_SUB, CONV_CH
    n_sub = CONV_TILE // sub

    @pl.when(t == 0)
    def _():
        a_scr[0, 0:HALO_A, :] = jnp.zeros((HALO_A, c), F32)
        b_scr[0, 0:HALO_B, :] = jnp.zeros((HALO_B, c), F32)

    def row_block(k):
        return pl.ds(k * sub, sub) if isinstance(k, int) else pl.ds(pl.multiple_of(k * sub, sub), sub)

    def project_steps(k):
        rows = row_block(k)
        nxt = (k + 1) % n_sub if isinstance(k, int) else jnp.where(k + 1 == n_sub, 0, k + 1)
        proj = lambda lo, hi: jnp.dot(u_scr[...], win_ref[:, lo * c:hi * c], preferred_element_type=F32)

        def norm():
            u_scr[...] = _rmsnorm(x_ref[rows, :], g0_ref[...]).astype(BF16)

        def mixer_a():
            za = proj(0, 2)
            ga = za[:, 0:c] * jax.nn.sigmoid(za[:, c:2 * c])
            a_scr[k, HALO_A:HALO_A + sub, :] = ga
            a_scr[nxt, 0:HALO_A, :] = ga[sub - HALO_A:sub]

        def mixer_b_in():
            zb = proj(4, 5) * proj(2, 3)
            b_scr[k, HALO_B:HALO_B + sub, :] = zb
            b_scr[nxt, 0:HALO_B, :] = zb[sub - HALO_B:sub]

        def mixer_b_gate():
            gb_scr[k] = proj(3, 4)

        return [norm, mixer_a, mixer_b_in, mixer_b_gate]

    def mix_steps(k):
        rows = row_block(k)
        a_buf, b_buf, gb_buf, mix_buf = a_scr.at[k], b_scr.at[k], gb_scr.at[k], mix_scr.at[k]
        off_b = HALO_B - (SC_WIDTH - 1)
        steps = []
        for r0 in range(0, sub, CONV_CHUNK):
            chunk = slice(r0, r0 + CONV_CHUNK)
            for s in range(c // LANES):
                def conv_slab(r0=r0, s=s, chunk=chunk):
                    conv_scr[chunk, s * LANES:(s + 1) * LANES] = _conv_a_slab(a_buf, aw_ref, r0, s)
                steps.append(conv_slab)

            def gate(r0=r0, chunk=chunk):
                ya = _layernorm_silu(conv_scr[chunk, :] + ab_ref[...], lng_ref[...], lnb_ref[...])
                cb = bw_ref[0:1, :] * b_buf[off_b + r0:off_b + r0 + CONV_CHUNK, :]
                for j in range(1, SC_WIDTH):
                    cb = cb + bw_ref[j:j + 1, :] * b_buf[off_b + r0 + j:off_b + r0 + j + CONV_CHUNK, :]
                yb = gb_buf[chunk, :] * cb
                mix_buf[chunk, :] = jnp.concatenate([ya, yb], axis=-1).astype(BF16)
            steps.append(gate)

        def out_proj():
            y = jnp.dot(mix_buf[...], wout_ref[...], preferred_element_type=F32)
            o_ref[rows, :] = x_ref[rows, :] + _rmsnorm(y, g1_ref[...])
        steps.append(out_proj)
        return steps

    def interleave(matmul_steps, vector_steps):
        per = -(-len(vector_steps) // max(len(matmul_steps), 1))
        while matmul_steps or vector_steps:
            if matmul_steps:
                matmul_steps.pop(0)()
            for _ in range(per):
                if vector_steps:
                    vector_steps.pop(0)()

    interleave(project_steps(0), [])
    for k in range(n_sub - 1):
        interleave(project_steps(k + 1), mix_steps(k))
    interleave([], mix_steps(n_sub - 1))

    sta_ref[0] = a_scr[0, 0:HALO_A, :]
    stb_ref[0] = b_scr[0, 0:HALO_B, :]


def _conv_layer(x, g0, w_in, aw, ab, lng, lnb, bw, w_out, g1):
    nt = SEQ // CONV_TILE
    n_sub = CONV_TILE // CONV_SUB
    row = lambda b, t: (b * nt + t, 0)
    seq = lambda b, t: (b, 0, 0)
    return pl.pallas_call(
        _conv_layer_body,
        grid=(BATCH, nt),
        in_specs=[pl.BlockSpec((CONV_TILE, D_MODEL), row), _const_spec((1, D_MODEL)),
                  _const_spec((D_MODEL, 5 * CONV_CH), single=True),
                  _const_spec((CONV_WIDTH, CONV_CH)), _const_spec((1, CONV_CH)),
                  _const_spec((1, CONV_CH)), _const_spec((1, CONV_CH)),
                  _const_spec((SC_WIDTH, CONV_CH)), _const_spec((D_MODEL, D_MODEL), single=True),
                  _const_spec((1, D_MODEL))],
        out_specs=[pl.BlockSpec((CONV_TILE, D_MODEL), row),
                   pl.BlockSpec((1, HALO_A, CONV_CH), seq), pl.BlockSpec((1, HALO_B, CONV_CH), seq)],
        out_shape=[jax.ShapeDtypeStruct(x.shape, F32),
                   jax.ShapeDtypeStruct((BATCH, HALO_A, CONV_CH), F32),
                   jax.ShapeDtypeStruct((BATCH, HALO_B, CONV_CH), F32)],
        scratch_shapes=[pltpu.VMEM((n_sub, halo + CONV_SUB, CONV_CH), F32) for halo in (HALO_A, HALO_B, 0)]
                       + [pltpu.VMEM((n_sub, CONV_SUB, 2 * CONV_CH), BF16),
                          pltpu.VMEM((CONV_SUB, D_MODEL), BF16), pltpu.VMEM((CONV_SUB, CONV_CH), F32)],
        compiler_params=_params("arbitrary", "arbitrary"),
        name="conv_layer",
    )(x, g0, w_in, aw, ab, lng, lnb, bw, w_out, g1)


def _conv_mix_dec_body(ga_ref, zb_ref, gb_ref, h_ref, sta_ref, sb0_ref, sb1_ref, aw_ref, ab_ref,
                       lng_ref, lnb_ref, bw_ref, w_ref, g_ref, o_ref, nsta_ref):
    ga = ga_ref[...]
    n_state = CONV_WIDTH - 1
    acc = aw_ref[n_state:n_state + 1, :] * ga
    for j in range(n_state):
        acc = acc + aw_ref[j:j + 1, :] * sta_ref[j]
    for j in range(n_state - 1):
        nsta_ref[j] = sta_ref[j + 1]
    nsta_ref[n_state - 1] = ga
    ya = _layernorm_silu(acc + ab_ref[...], lng_ref[...], lnb_ref[...])
    cb = bw_ref[0:1, :] * sb0_ref[...] + bw_ref[1:2, :] * sb1_ref[...] + bw_ref[2:3, :] * zb_ref[...]
    yb = gb_ref[...] * cb
    mix = jnp.concatenate([ya, yb], axis=-1).astype(BF16)
    y = jnp.dot(mix, w_ref[...], preferred_element_type=F32)
    o_ref[...] = h_ref[...] + _rmsnorm(y, g_ref[...])


def _conv_mix_dec(ga, zb, gb, h, sta, sb0, sb1, aw, ab, lng, lnb, bw, w, g):
    args = (ga, zb, gb, h, sta, sb0, sb1, aw, ab, lng, lnb, bw, w, g)
    return pl.pallas_call(
        _conv_mix_dec_body,
        grid=(1,),
        in_specs=[_const_spec(a.shape) for a in args],
        out_specs=[_const_spec(h.shape), _const_spec(sta.shape)],
        out_shape=[jax.ShapeDtypeStruct(h.shape, F32), jax.ShapeDtypeStruct(sta.shape, F32)],
        compiler_params=_params("arbitrary"),
        name="conv_mix_dec",
    )(*args)


FF_CHUNK = 1024
MLP_TILE = 1024
MIX_MLP_TILE = 512


def _shift_in_lanes(x, new_cols, first_seq, n_seq):
    lane = lax.broadcasted_iota(jnp.int32, (1, LANES), 1)
    new = jnp.concatenate([pltpu.roll(new_cols, LANES - 1 - first_seq - b, 1) for b in range(n_seq)], axis=0)
    n_t = x.shape[1] // LANES
    rolled = [pltpu.roll(x[:, t * LANES:(t + 1) * LANES], LANES - 1, 1) for t in range(n_t)]
    tiles = [jnp.where(lane == LANES - 1, rolled[t + 1] if t + 1 < n_t else new, rolled[t])
             for t in range(n_t)]
    return tiles[0] if n_t == 1 else jnp.concatenate(tiles, axis=1)


def _mlp_block(x, g2_ref, w1_ref, w2_ref, g3_ref):
    u = _rmsnorm(x, g2_ref[...]).astype(BF16)
    acc = jnp.zeros(x.shape, F32)
    for c in range(D_FF // FF_CHUNK):
        sl = slice(c * FF_CHUNK, (c + 1) * FF_CHUNK)
        hid = jnp.dot(u, w1_ref[:, sl], preferred_element_type=F32)
        hid = jnp.square(jnp.maximum(hid, 0.0)).astype(BF16)
        acc = acc + jnp.dot(hid, w2_ref[sl, :], preferred_element_type=F32)
    return x + _rmsnorm(acc, g3_ref[...])


def _mlp_body(xp_ref, xs_ref, g2_ref, w1_ref, w2_ref, g3_ref, op_ref, os_ref):
    i, n = pl.program_id(0), pl.num_programs(0) - 1

    @pl.when(i < n)
    def _():
        op_ref[...] = _mlp_block(xp_ref[...], g2_ref, w1_ref, w2_ref, g3_ref)

    @pl.when(i == n)
    def _():
        os_ref[...] = _mlp_block(xs_ref[...], g2_ref, w1_ref, w2_ref, g3_ref)


def _mix_mlp_body(mp_ref, ms_ref, hp_ref, hs_ref, wo_ref, g1_ref, g2_ref, w1_ref, w2_ref, g3_ref,
                  cache_ref, kvt_ref, op_ref, os_ref, ncache_ref, *, cache_rows):
    i, n = pl.program_id(0), pl.num_programs(0) - 1
    n_seq = cache_ref.shape[0]

    def block(mix_ref, h_ref):
        y = jnp.dot(mix_ref[...].astype(BF16), wo_ref[...], preferred_element_type=F32)
        x = h_ref[...] + _rmsnorm(y, g1_ref[...])
        return _mlp_block(x, g2_ref, w1_ref, w2_ref, g3_ref)

    @pl.when(i < n)
    def _():
        for part in range(2):
            x = cache_ref[:, part]
            r0 = cache_rows + part * HEAD_DIM
            new = _shift_in_lanes(x.reshape(n_seq * HEAD_DIM, x.shape[2]), kvt_ref[r0:r0 + HEAD_DIM, :],
                                  i * n_seq, n_seq)
            ncache_ref[:, part] = new.reshape(x.shape)
        op_ref[...] = block(mp_ref, hp_ref)

    @pl.when(i == n)
    def _():
        os_ref[...] = block(ms_ref, hs_ref)


def _two_group_specs(xp, xs, tm):
    n = xp.shape[0] // tm
    prompt = pl.BlockSpec((tm, xp.shape[1]), lambda i: (jnp.minimum(i, n - 1), 0))
    return n, prompt, _const_spec(xs.shape)


def _mix_mlp(mix_p, mix_s, hp, hs, wo, g1, g2, w1, w2, g3, cache, kvt, cache_rows, tm):
    n, mp_spec, ms_spec = _two_group_specs(mix_p, mix_s, tm)
    _, hp_spec, hs_spec = _two_group_specs(hp, hs, tm)
    n_seq = cache.shape[0] // n
    assert n_seq * n == cache.shape[0]
    c_spec = pl.BlockSpec((n_seq,) + cache.shape[1:], lambda i: (jnp.minimum(i, n - 1), 0, 0, 0))
    vec = _const_spec((1, D_MODEL))
    return pl.pallas_call(
        functools.partial(_mix_mlp_body, cache_rows=cache_rows),
        grid=(n + 1,),
        in_specs=[mp_spec, ms_spec, hp_spec, hs_spec, _const_spec(wo.shape, single=True), vec, vec,
                  _const_spec((D_MODEL, D_FF), single=True), _const_spec((D_FF, D_MODEL), single=True), vec,
                  c_spec, _const_spec(kvt.shape)],
        out_specs=[hp_spec, hs_spec, c_spec],
        out_shape=[jax.ShapeDtypeStruct(hp.shape, F32), jax.ShapeDtypeStruct(hs.shape, F32),
                   jax.ShapeDtypeStruct(cache.shape, F32)],
        compiler_params=_params("arbitrary"),
        name="mix_mlp",
    )(mix_p, mix_s, hp, hs, wo, g1, g2, w1, w2, g3, cache, kvt)


def _mlp(xp, xs, g2, w1, w2, g3, tm):
    n, p_spec, s_spec = _two_group_specs(xp, xs, tm)
    vec = _const_spec((1, D_MODEL))
    return pl.pallas_call(
        _mlp_body,
        grid=(n + 1,),
        in_specs=[p_spec, s_spec, vec,
                  _const_spec((D_MODEL, D_FF), single=True), _const_spec((D_FF, D_MODEL), single=True), vec],
        out_specs=[p_spec, s_spec],
        out_shape=[jax.ShapeDtypeStruct(xp.shape, F32), jax.ShapeDtypeStruct(xs.shape, F32)],
        compiler_params=_params("arbitrary"),
        name="mlp",
    )(xp, xs, g2, w1, w2, g3)


def _rope_slab(z, c, s_lo, s_hi):
    return z * c + pltpu.roll(z, LANES - ROT_HALF, 1) * s_lo + pltpu.roll(z, ROT_HALF, 1) * s_hi


def _attn_in_body(x_ref, g_ref, w_ref, ca_ref, sla_ref, sha_ref, cb_ref, slb_ref, shb_ref,
                  qn_ref, qd_ref, kv_ref, *, scale):
    u = _rmsnorm(x_ref[...], g_ref[...]).astype(BF16)
    z = jnp.dot(u, w_ref[...], preferred_element_type=F32)
    ca, sla, sha = ca_ref[...], sla_ref[...], sha_ref[...]
    for s in range(Q_COLS // LANES):
        sl = slice(s * LANES, (s + 1) * LANES)
        q = _rope_slab(z[:, sl], ca, sla, sha) * scale
        if s < QN_SLABS:
            qn_ref[:, sl] = q.astype(BF16)
        else:
            qd_ref[s - QN_SLABS] = q
    kv_ref[0] = _rope_slab(z[:, Q_COLS:Q_COLS + LANES], ca, sla, sha)
    kv_ref[1] = z[:, Q_COLS + LANES:Q_COLS + 2 * LANES]
    cb, slb, shb = cb_ref[...], slb_ref[...], shb_ref[...]
    for s in range(2, KV_SLABS):
        sl = slice(Q_COLS + s * LANES, Q_COLS + (s + 1) * LANES)
        kv_ref[s] = _rope_slab(z[:, sl], cb, slb, shb)


QN_SLABS = (SWA_HEADS + DIL_HEADS) * HEAD_DIM // LANES
QD_SLABS = Q_COLS // LANES - QN_SLABS
KV_SLABS = KV_COLS // LANES


def _attn_in(x, g, w, tabs, tm, scale):
    t = x.shape[0]
    row = lambda i: (i, 0)
    slab = lambda i: (0, i, 0)
    nper = tabs[0].shape[0] // tm
    tab = pl.BlockSpec((tm, LANES), lambda i: (i % nper, 0))
    return pl.pallas_call(
        functools.partial(_attn_in_body, scale=scale),
        grid=(t // tm,),
        in_specs=[pl.BlockSpec((tm, D_MODEL), row), _const_spec((1, D_MODEL)),
                  _const_spec((D_MODEL, ATTN_COLS), single=True)] + [tab] * 6,
        out_specs=[pl.BlockSpec((tm, QN_SLABS * LANES), row), pl.BlockSpec((QD_SLABS, tm, LANES), slab),
                   pl.BlockSpec((KV_SLABS, tm, LANES), slab)],
        out_shape=[jax.ShapeDtypeStruct((t, QN_SLABS * LANES), BF16),
                   jax.ShapeDtypeStruct((QD_SLABS, t, LANES), F32),
                   jax.ShapeDtypeStruct((KV_SLABS, t, LANES), F32)],
        compiler_params=_params("arbitrary"),
        name="attn_in",
    )(x, g, w, *tabs)


LOG2E = math.log2(math.e)
ATTN_STEP = 4 * BLK
ATTN_LOOKAHEAD = 2
_NT = (((1,), (1,)), ((), ()))
_TN = (((0,), (0,)), ((), ()))


def _both_halves(x, in_hi):
    lane = lax.broadcasted_iota(jnp.int32, x.shape, 1)
    return jnp.where(lane >= HEAD_DIM if in_hi else lane < HEAD_DIM, x, pltpu.roll(x, HEAD_DIM, 1))


def _block_scores(q_slabs, k_tiles, k_hi, bias):
    in_a = lax.broadcasted_iota(jnp.int32, (BLK, LANES), 1) < HEAD_DIM
    kk = jnp.concatenate([_both_halves(t, k_hi) for t in k_tiles], axis=0).astype(BF16)
    zero = jnp.zeros((BLK, LANES), BF16)
    scores = []
    for qs in q_slabs:
        qq = jnp.concatenate([jnp.where(in_a, qs, zero), jnp.where(in_a, zero, qs)], axis=0)
        scores.append(lax.dot_general(kk, qq, _NT, preferred_element_type=F32) + bias)
    return scores


def _block_outputs(scores, v_tiles, v_hi, sinks):
    lane = lax.broadcasted_iota(jnp.int32, (BLK, LANES), 1)
    one_lane = 0 if v_hi else HEAD_DIM
    vv = jnp.concatenate([jnp.where(lane == one_lane, 1.0, t) for t in v_tiles], axis=0).astype(BF16)
    probs, maxes = [], []
    for slab, sh in enumerate(scores):
        m = jnp.max(sh, axis=0, keepdims=True)
        if sinks is not None:
            m = jnp.maximum(m, sinks[slab])
        probs.append(jnp.exp2(sh - m).astype(BF16))
        maxes.append(m)
    v0 = HEAD_DIM if v_hi else 0
    outs = []
    for slab, p in enumerate(probs):
        ot = lax.dot_general(vv, p, _TN, preferred_element_type=F32)
        m = maxes[slab]
        den = ot[one_lane:one_lane + 1, :]
        if sinks is not None:
            den = den + jnp.exp2(sinks[slab] - m)
        o_t = jnp.concatenate([ot[v0:v0 + HEAD_DIM, 0:BLK], ot[v0:v0 + HEAD_DIM, BLK:2 * BLK]], axis=0)
        tile = lambda row: jnp.concatenate([jnp.broadcast_to(row[:, 0:BLK], (HEAD_DIM, BLK)),
                                            jnp.broadcast_to(row[:, BLK:2 * BLK], (HEAD_DIM, BLK))], axis=0)
        outs.append(((o_t / tile(den)).T, tile(m + jnp.log2(den)).T))
    return outs


N_RES = 4


def _attn_seq_body(qn_ref, qd_ref, kv_ref, sink_ref, mix_ref, qc1, kc1, qc2, kc2, res):
    key2 = lax.broadcasted_iota(jnp.int32, (2 * BLK, 2 * BLK), 0)
    qry2 = lax.broadcasted_iota(jnp.int32, (2 * BLK, 2 * BLK), 1) & (BLK - 1)
    slack = jnp.where(key2 < BLK, key2 - qry2, qry2 - key2 + BLK)
    in_prev = jnp.where(key2 < BLK, 4 * BLK, 0)
    neg_inf = jnp.float32(-jnp.inf)
    bias_both = jnp.where(slack >= 0, 0.0, neg_inf)
    bias_cur = bias_both[BLK:2 * BLK]

    for gi, (qc, kc) in ((1, (qc1, kc1)), (2, (qc2, kc2))):
        dil = DIL_PATTERNS[gi][1]
        n = SEQ // dil
        for r in range(dil):
            dst = slice(r * n, (r + 1) * n)
            for s in range(2):
                qc[dst, s * LANES:(s + 1) * LANES] = qd_ref[2 * (gi - 1) + s, pl.ds(r, n, stride=dil), :].astype(BF16)
            kc[dst, :] = kv_ref[2 + gi, pl.ds(r, n, stride=dil), :]

    def run(q_src, q_col0, n_slabs, k_src, v_src, n_kv, packed, chain, sinks, write):
        slabs_per_kv = n_slabs // n_kv

        def step(it, carry):
            base = pl.multiple_of(it * ATTN_STEP, ATTN_STEP)
            items = [(j, kvh) for j in range(ATTN_STEP // BLK) for kvh in range(n_kv)]

            def key_rows(j):
                r0 = pl.multiple_of(base + j * BLK, BLK)
                if chain == "block" or (chain == "step" and j == 0):
                    return r0, [pl.ds(r0, BLK)], bias_cur
                if chain == "seq" and j == 0:
                    prev = pl.ds(pl.multiple_of(jnp.maximum(r0 - BLK, 0), BLK), BLK)
                    gone = in_prev * jnp.where(it > 0, 0, 1)
                    return r0, [prev, pl.ds(r0, BLK)], jnp.where(slack - gone >= 0, 0.0, neg_inf)
                return r0, [pl.ds(pl.multiple_of(r0 - BLK, BLK), BLK), pl.ds(r0, BLK)], bias_both

            def scores_of(item):
                j, kvh = item
                r0, krows, bias = key_rows(j)
                slabs = range(kvh * slabs_per_kv, (kvh + 1) * slabs_per_kv)
                q_slabs = [q_src[pl.ds(r0, BLK), q_col0 + s * LANES:q_col0 + (s + 1) * LANES] for s in slabs]
                k_hi = (not packed) and kvh == 1
                return _block_scores(q_slabs, [k_src[r, :] for r in krows], k_hi, bias)

            def finish(item, scores):
                j, kvh = item
                r0, krows, _ = key_rows(j)
                slabs = range(kvh * slabs_per_kv, (kvh + 1) * slabs_per_kv)
                sk = None if sinks is None else [sinks[s] for s in slabs]
                v_hi = packed or kvh == 1
                outs = _block_outputs(scores, [v_src[r, :] for r in krows], v_hi, sk)
                for s, (o, lse) in zip(slabs, outs):
                    write(it, j, r0, s, o, lse)

            ahead = min(ATTN_LOOKAHEAD, len(items))
            pending = [scores_of(item) for item in items[:ahead]]
            for n, item in enumerate(items):
                if n + ahead < len(items):
                    pending.append(scores_of(items[n + ahead]))
                finish(item, pending.pop(0))
            return carry

        lax.fori_loop(0, SEQ // ATTN_STEP, step, 0)

    sinks = [jnp.concatenate([jnp.broadcast_to(sink_ref[0:1, 2 * s + half:2 * s + half + 1] * LOG2E, (1, BLK))
                              for half in range(2)], axis=1) for s in range(SWA_HEADS // 2)]

    def write_swa(it, j, r0, s, o, lse):
        mix_ref[pl.ds(r0, BLK), s * LANES:(s + 1) * LANES] = o.astype(BF16)

    run(qn_ref, 0, SWA_HEADS // 2, kv_ref.at[0], kv_ref.at[1], SWA_KV, False, "seq", sinks, write_swa)

    def write_res(group, start_of, stride):
        def write(it, j, r0, s, o, lse):
            dst = pl.ds(start_of(it, j, r0), BLK) if stride == 1 else pl.ds(start_of(it, j, r0), BLK, stride=stride)
            res[group * N_RES + s, dst, :] = o
            res[group * N_RES + 2 + s, dst, :] = lse
        return write

    n_dil_slabs = DIL_HEADS // 2
    run(qn_ref, SWA_HEADS * HEAD_DIM, n_dil_slabs, kv_ref.at[2], kv_ref.at[2], 1, True, "seq", None,
        write_res(0, lambda it, j, r0: r0, 1))
    d1, d2 = DIL_PATTERNS[1][1], DIL_PATTERNS[2][1]
    run(qc1, 0, n_dil_slabs, kc1, kc1, 1, True, "step", None,
        write_res(1, lambda it, j, r0: d1 * j * BLK + it, d1))
    run(qc2, 0, n_dil_slabs, kc2, kc2, 1, True, "block", None,
        write_res(2, lambda it, j, r0: it * (ATTN_STEP // BLK) + j, d2))

    def merge(c, carry):
        rows = pl.ds(pl.multiple_of(c * ATTN_STEP, ATTN_STEP), ATTN_STEP)
        for s in range(n_dil_slabs):
            lse = [res[g * N_RES + 2 + s, rows, :] for g in range(3)]
            m = jnp.maximum(jnp.maximum(lse[0], lse[1]), lse[2])
            e = [jnp.exp2(l - m) for l in lse]
            o = e[0] * res[s, rows, :] + e[1] * res[N_RES + s, rows, :] + e[2] * res[2 * N_RES + s, rows, :]
            col = SWA_HEADS * HEAD_DIM + s * LANES
            mix_ref[rows, col:col + LANES] = (o / (e[0] + e[1] + e[2])).astype(BF16)
        return carry

    lax.fori_loop(0, SEQ // ATTN_STEP, merge, 0)


def _attn_seq(qn, qd, kv, sinks):
    seq2 = lambda b: (b, 0)
    seq3 = lambda b: (0, b, 0)
    wd = DIL_HEADS * HEAD_DIM
    return pl.pallas_call(
        _attn_seq_body,
        grid=(BATCH,),
        in_specs=[pl.BlockSpec((SEQ, QN_SLABS * LANES), seq2), pl.BlockSpec((QD_SLABS, SEQ, LANES), seq3),
                  pl.BlockSpec((KV_SLABS, SEQ, LANES), seq3), _const_spec(sinks.shape)],
        out_specs=pl.BlockSpec((SEQ, MIX_ATTN), seq2),
        out_shape=jax.ShapeDtypeStruct((BATCH * SEQ, MIX_ATTN), BF16),
        scratch_shapes=[pltpu.VMEM((SEQ, wd), BF16), pltpu.VMEM((SEQ, LANES), F32),
                        pltpu.VMEM((SEQ, wd), BF16), pltpu.VMEM((SEQ, LANES), F32),
                        pltpu.VMEM((3 * N_RES, SEQ, LANES), F32)],
        compiler_params=_params("arbitrary"),
        name="attn_seq",
    )(qn, qd, kv, sinks)


DEC_TILE = 4
N_MIX_HEADS = SWA_HEADS + DIL_HEADS


def _attn_dec_body(q_ref, kv_ref, kv_all_ref, sink_ref, csw_ref, cd0_ref, cd1_ref, cd2_ref,
                   nsw_ref, nd0_ref, nd1_ref, o_ref, kvt_ref):
    step = pl.program_id(0)
    bb, hd, grp = DEC_TILE, HEAD_DIM, DIL_HEADS
    n_rows = grp * bb

    @pl.when(step == 0)
    def _():
        kvt_ref[...] = kv_all_ref[...].T

    q = q_ref[0]
    kv = kv_ref[0]
    row_seq = lax.broadcasted_iota(jnp.int32, (n_rows, bb * hd), 0) & (bb - 1)
    col_seq = lax.broadcasted_iota(jnp.int32, (n_rows, bb * hd), 1) // hd
    diag = row_seq == col_seq
    lane = lax.broadcasted_iota(jnp.int32, (1, LANES), 1)
    nt = (((1,), (1,)), ((), ()))

    def q_rows(c0):
        return jnp.concatenate([q[:, c0 + s * hd:c0 + (s + 1) * hd] for s in range(grp)], axis=0)

    def per_row(x):
        return jnp.concatenate([x] * grp, axis=0)

    def block_diag(qr):
        return jnp.where(diag, jnp.concatenate([qr] * bb, axis=1), 0.0).astype(BF16)

    def take_diag(ob):
        ob = jnp.where(diag, ob, 0.0)
        out = ob[:, 0:hd]
        for b in range(1, bb):
            out = out + ob[:, b * hd:(b + 1) * hd]
        return out

    def shift_in(x, r0):
        return _shift_in_lanes(x, kvt_ref[r0:r0 + hd, :], step * bb, bb)

    def stack(ref, idx):
        x = ref[:, idx]
        return x.reshape(bb * hd, x.shape[2])

    def unstack(x):
        return x.reshape(bb, hd, x.shape[1])

    cd_refs = (cd0_ref, cd1_ref, cd2_ref)
    nd_refs = (nd0_ref, nd1_ref, None)
    n_dil = len(DIL_PATTERNS)
    jobs = [dict(src=csw_ref, dst=nsw_ref, ki=kvh, vi=SWA_KV + kvh, q0=kvh * grp * hd,
                 kc=kvh * hd, vc=LANES + kvh * hd, dil=1) for kvh in range(SWA_KV)]
    jobs += [dict(src=cd_refs[gi], dst=nd_refs[gi], ki=0, vi=1, q0=SWA_HEADS * hd + gi * grp * hd,
                  kc=(2 + gi) * LANES, vc=(2 + gi) * LANES + hd, dil=DIL_PATTERNS[gi][1]) for gi in range(n_dil)]

    for jb in jobs:
        k = stack(jb["src"], jb["ki"])
        qr = q_rows(jb["q0"])
        s = jnp.dot(block_diag(qr), k.astype(BF16), preferred_element_type=F32)
        if jb["dil"] > 1:
            pos = lax.broadcasted_iota(jnp.int32, (1, k.shape[1]), 1)
            s = jnp.where((pos & (jb["dil"] - 1)) == 0, s, -jnp.inf)
        jb["s"] = s
        jb["s_new"] = jnp.sum(qr * per_row(kv[:, jb["kc"]:jb["kc"] + hd]), axis=1, keepdims=True)
        jb["v_new"] = per_row(kv[:, jb["vc"]:jb["vc"] + hd])
        jb["m"] = jnp.maximum(jnp.max(s, axis=1, keepdims=True), jb["s_new"])

    m_dil = jobs[SWA_KV]["m"]
    for jb in jobs[SWA_KV + 1:]:
        m_dil = jnp.maximum(m_dil, jb["m"])
    for kvh, jb in enumerate(jobs):
        if kvh < SWA_KV:
            sk = jnp.concatenate([jnp.broadcast_to(sink_ref[0:1, kvh * grp + s:kvh * grp + s + 1], (bb, 1))
                                  for s in range(grp)], axis=0)
            m = jnp.maximum(jb["m"], sk)
            jb["extra"] = jnp.exp(sk - m)
        else:
            m = m_dil
            jb["extra"] = 0.0
        p = jnp.exp(jb["s"] - m)
        jb["p_new"] = jnp.exp(jb["s_new"] - m)
        jb["den"] = jnp.sum(p, axis=1, keepdims=True) + jb["p_new"] + jb["extra"]
        jb["p"] = p.astype(BF16)

    for jb in jobs:
        v = stack(jb["src"], jb["vi"])
        pv = lax.dot_general(jb["p"], v.astype(BF16), nt, preferred_element_type=F32)
        jb["acc"] = take_diag(pv) + jb["p_new"] * jb["v_new"]

    for kvh in range(SWA_KV):
        o = jobs[kvh]["acc"] / jobs[kvh]["den"]
        for s_ in range(grp):
            o_ref[0, kvh * grp + s_] = o[s_ * bb:(s_ + 1) * bb]
    dil_jobs = jobs[SWA_KV:]
    o = sum(jb["acc"] for jb in dil_jobs[1:]) + dil_jobs[0]["acc"]
    o = o / (sum(jb["den"] for jb in dil_jobs[1:]) + dil_jobs[0]["den"])
    for s_ in range(grp):
        o_ref[0, SWA_HEADS + s_] = o[s_ * bb:(s_ + 1) * bb]

    for jb in jobs:
        if jb["dst"] is not None:
            jb["dst"][:, jb["ki"]] = unstack(shift_in(stack(jb["src"], jb["ki"]), jb["kc"]))
            jb["dst"][:, jb["vi"]] = unstack(shift_in(stack(jb["src"], jb["vi"]), jb["vc"]))


def _attn_dec(q, kv, sinks, csw, cd0, cd1, cd2):
    caches = (csw, cd0, cd1, cd2)
    n_tiles = DEC_BATCH // DEC_TILE
    q3 = q.reshape(n_tiles, DEC_TILE, Q_COLS)
    kv3 = kv.reshape(n_tiles, DEC_TILE, KV_COLS)
    tile3 = lambda w: pl.BlockSpec((1, DEC_TILE, w), lambda i: (i, 0, 0))
    cspec = lambda c: pl.BlockSpec((DEC_TILE,) + c.shape[1:], lambda i: (i, 0, 0, 0))
    o_shape = (n_tiles, N_MIX_HEADS, DEC_TILE, HEAD_DIM)
    shifted = caches[:3]
    outs = pl.pallas_call(
        _attn_dec_body,
        grid=(n_tiles,),
        in_specs=[tile3(Q_COLS), tile3(KV_COLS), _const_spec(kv.shape), _const_spec(sinks.shape)]
                 + [cspec(c) for c in caches],
        out_specs=[cspec(c) for c in shifted]
                  + [pl.BlockSpec((1,) + o_shape[1:], lambda i: (i, 0, 0, 0)),
                     _const_spec((KV_COLS, DEC_BATCH))],
        out_shape=[jax.ShapeDtypeStruct(c.shape, F32) for c in shifted]
                  + [jax.ShapeDtypeStruct(o_shape, F32), jax.ShapeDtypeStruct((KV_COLS, DEC_BATCH), F32)],
        compiler_params=_params("arbitrary"),
        name="attn_dec",
    )(q3, kv3, kv, sinks, *caches)
    mix = jnp.transpose(outs[3], (0, 2, 1, 3)).reshape(DEC_BATCH, MIX_ATTN)
    return outs[0], outs[1], outs[2], mix, outs[4]


def _attn_weight_order(w):
    hd = HEAD_DIM
    nq, nkv = SWA_HEADS * hd, SWA_KV * hd
    base = nq + 2 * nkv
    per = (DIL_HEADS + 2) * hd
    qd = [w[:, base + g * per:base + g * per + DIL_HEADS * hd] for g in range(3)]
    kvd = [w[:, base + g * per + DIL_HEADS * hd:base + (g + 1) * per] for g in range(3)]
    return jnp.concatenate([w[:, :nq]] + qd + [w[:, nq:base]] + kvd, axis=1)


def _rope_angles(pos):
    inv = ROPE_THETA ** (-jnp.arange(ROT_HALF, dtype=F32) / ROT_HALF)
    ang = pos.astype(F32)[:, None] * inv[None, :]
    return jnp.cos(ang), jnp.sin(ang)


def _rope_tables(pos):
    cos, sin = _rope_angles(pos)
    n = pos.shape[0]
    ones = lambda w: jnp.ones((n, w), F32)
    zeros = lambda w: jnp.zeros((n, w), F32)
    rest = HEAD_DIM - 2 * ROT_HALF
    c_h = jnp.concatenate([cos, cos, ones(rest)], axis=1)
    lo_h = jnp.concatenate([-sin, zeros(ROT_HALF), zeros(rest)], axis=1)
    hi_h = jnp.concatenate([zeros(ROT_HALF), sin, zeros(rest)], axis=1)
    both = tuple(jnp.concatenate([t, t], axis=1) for t in (c_h, lo_h, hi_h))
    first = (jnp.concatenate([c_h, ones(HEAD_DIM)], axis=1),
             jnp.concatenate([lo_h, zeros(HEAD_DIM)], axis=1),
             jnp.concatenate([hi_h, zeros(HEAD_DIM)], axis=1))
    return both + first


def _cache_view(c):
    b, l, two, kv, hd = c.shape
    return jnp.transpose(c, (0, 2, 3, 4, 1)).reshape(b, two * kv, hd, l)


def _cache_unview(c, kv):
    b, _, hd, l = c.shape
    return jnp.transpose(c.reshape(b, 2, kv, hd, l), (0, 4, 1, 2, 3))


def kernel(x_prompt, x_sample, state_conv_a, state_conv_b, cache_swa_kv, cache_dil0_kv, cache_dil1_kv,
           cache_dil2_kv, norm_g, w_in_conv, conv_a_w, conv_a_b, conv_a_ln_g, conv_a_ln_b, conv_b_w,
           w_out_conv, w_in_attn, attn_sinks, w_out_attn, mlp_w1, mlp_w2):
    tm = 512
    hp = x_prompt.reshape(BATCH * SEQ, D_MODEL)
    hs = x_sample.reshape(DEC_BATCH, D_MODEL)
    g = lambda layer, i: norm_g[layer, i].reshape(1, D_MODEL)

    w_in0 = w_in_conv[0].astype(BF16)
    w_out0 = w_out_conv[0].astype(BF16)
    conv_small = (conv_a_w[0], conv_a_b, conv_a_ln_g, conv_a_ln_b, conv_b_w[0])
    hp, sta_p, stb_p = _conv_layer(hp, g(0, 0), w_in0, *conv_small, w_out0, g(0, 1))
    ga_s, zb_s, gb_s = _conv_in(hs, g(0, 0), w_in0, DEC_BATCH)
    sta = jnp.transpose(state_conv_a[0], (1, 0, 2))
    sb0, sb1 = state_conv_b[0, :, 0], state_conv_b[0, :, 1]
    hs, new_sta = _conv_mix_dec(ga_s, zb_s, gb_s, hs, sta, sb0, sb1, *conv_small, w_out0, g(0, 1))

    w1, w2 = mlp_w1.astype(BF16), mlp_w2.astype(BF16)
    hp, hs = _mlp(hp, hs, g(0, 2), w1[0], w2[0], g(0, 3), MLP_TILE)

    w_in1 = _attn_weight_order(w_in_attn[0]).astype(BF16)
    w_out1 = w_out_attn[0].astype(BF16)
    softmax_scale = HEAD_DIM ** -0.5
    qn_p, qd_p, kv_p = _attn_in(hp, g(1, 0), w_in1, _rope_tables(jnp.arange(SEQ)), tm,
                                softmax_scale * LOG2E)
    mix_p = _attn_seq(qn_p, qd_p, kv_p, attn_sinks)

    tabs_s = _rope_tables(jnp.full((DEC_BATCH,), PAST_LEN, jnp.int32))
    qn_s, qd_s, kv_s = _attn_in(hs, g(1, 0), w_in1, tabs_s, DEC_BATCH, softmax_scale)
    q_s = jnp.concatenate([qn_s.astype(F32)] + [qd_s[s] for s in range(QD_SLABS)], axis=1)
    kv_s = jnp.concatenate([kv_s[s] for s in range(KV_SLABS)], axis=1)
    caches = (cache_swa_kv[0], cache_dil0_kv[0], cache_dil1_kv[0], cache_dil2_kv[0])
    views = [_cache_view(c) for c in caches]
    nsw, nd0, nd1, mix_s, kvt_s = _attn_dec(q_s, kv_s, attn_sinks, *views)

    hp, hs, nd2 = _mix_mlp(mix_p, mix_s, hp, hs, w_out1, g(1, 1), g(1, 2), w1[1], w2[1], g(1, 3),
                           views[3], kvt_s, (KV_SLABS - 1) * LANES, MIX_MLP_TILE)

    n_a, n_b = CONV_WIDTH - 1, SC_WIDTH - 1
    kv4 = kv_p.reshape(KV_SLABS, BATCH, SEQ, LANES)
    swa_p = jnp.stack([kv4[0, :, SEQ - BLK:], kv4[1, :, SEQ - BLK:]], axis=2)
    swa_p = swa_p.reshape(BATCH, BLK, 2, SWA_KV, HEAD_DIM)
    dil_p = [kv4[2 + gi, :, SEQ - min(w, SEQ):].reshape(BATCH, min(w, SEQ), 2, 1, HEAD_DIM)
             for gi, (w, _) in enumerate(DIL_PATTERNS)]
    return (hp.reshape(BATCH, SEQ, D_MODEL), hs.reshape(DEC_BATCH, 1, D_MODEL),
            sta_p[None, :, HALO_A - n_a:],
            jnp.transpose(new_sta, (1, 0, 2))[None],
            stb_p[None, :, HALO_B - n_b:],
            jnp.stack([sb1, zb_s], axis=1)[None],
            swa_p[None], _cache_unview(nsw, SWA_KV)[None],
            dil_p[0][None], _cache_unview(nd0, 1)[None],
            dil_p[1][None], _cache_unview(nd1, 1)[None],
            dil_p[2][None], _cache_unview(nd2, 1)[None])
```

```python
import functools
import math

import jax
import jax.numpy as jnp
from jax import lax
from jax.experimental import pallas as pl
from jax.experimental.pallas import tpu as pltpu

F32 = jnp.float32
BF16 = jnp.bfloat16

D_MODEL = 1024
BATCH = 8
SEQ = 2048
DEC_BATCH = 128
PAST_LEN = 8192
HEAD_DIM = 64
ROT_HALF = 8
ROPE_THETA = 500000.0
D_FF = 4 * D_MODEL
EPS = 1e-6
CONV_CH = 512
CONV_WIDTH = 31
SC_WIDTH = 3
SWA_HEADS = 8
SWA_KV = 2
DIL_HEADS = 4
DIL_PATTERNS = ((128, 1), (512, 4), (2048, 16))
Q_COLS = (SWA_HEADS + 3 * DIL_HEADS) * HEAD_DIM
KV_COLS = 2 * SWA_KV * HEAD_DIM + 3 * 2 * HEAD_DIM
ATTN_COLS = Q_COLS + KV_COLS
MIX_ATTN = (SWA_HEADS + DIL_HEADS) * HEAD_DIM
LANES = 128
BLK = 128

V7X_VMEM_BYTES = 64 * 1024 * 1024
VMEM_LIMIT = V7X_VMEM_BYTES - 8 * 1024 * 1024


def _params(*sem):
    return pltpu.CompilerParams(dimension_semantics=sem, vmem_limit_bytes=VMEM_LIMIT)


def _const_spec(shape, single=False):
    zeros = (0,) * len(shape)
    if single:
        return pl.BlockSpec(shape, lambda *_: zeros, pipeline_mode=pl.Buffered(1))
    return pl.BlockSpec(shape, lambda *_: zeros)


def _rmsnorm(x, g):
    return x * lax.rsqrt(jnp.mean(x * x, axis=-1, keepdims=True) + EPS) * g


def _conv_in_body(x_ref, g_ref, w_ref, ga_ref, zb_ref, gb_ref):
    u = _rmsnorm(x_ref[...], g_ref[...]).astype(BF16)
    z = jnp.dot(u, w_ref[...], preferred_element_type=F32)
    c = CONV_CH
    ga_ref[...] = z[:, 0:c] * jax.nn.sigmoid(z[:, c:2 * c])
    zb_ref[...] = z[:, 4 * c:5 * c] * z[:, 2 * c:3 * c]
    gb_ref[...] = z[:, 3 * c:4 * c]


def _conv_in(x, g, w, tm):
    t = x.shape[0]
    row = lambda i: (i, 0)
    out = jax.ShapeDtypeStruct((t, CONV_CH), F32)
    return pl.pallas_call(
        _conv_in_body,
        grid=(t // tm,),
        in_specs=[pl.BlockSpec((tm, D_MODEL), row), _const_spec((1, D_MODEL)),
                  _const_spec((D_MODEL, 5 * CONV_CH))],
        out_specs=[pl.BlockSpec((tm, CONV_CH), row)] * 3,
        out_shape=[out] * 3,
        compiler_params=_params("arbitrary"),
        name="conv_in",
    )(x, g, w)


HALO_A = 32
HALO_B = 8
CONV_CHUNK = 128


def _layernorm_silu(c, g, b):
    mu = jnp.mean(c, axis=-1, keepdims=True)
    d = c - mu
    var = jnp.mean(d * d, axis=-1, keepdims=True)
    y = d * lax.rsqrt(var + EPS) * g + b
    return y * jax.nn.sigmoid(y)


SUBLANES = 8
CONV_TILE = 1024
CONV_SUB = 256


def _conv_a_slab(ext_a, aw_ref, r0, s):
    cols = slice(s * LANES, (s + 1) * LANES)
    out = None
    for b in range(SUBLANES):
        rows = CONV_CHUNK if b == 0 else CONV_CHUNK + SUBLANES
        yb = None
        for a in range((CONV_WIDTH + 1) // SUBLANES + 1):
            j = SUBLANES * a + b - (HALO_A - (CONV_WIDTH - 1))
            if 0 <= j < CONV_WIDTH:
                term = aw_ref[j:j + 1, cols] * ext_a[r0 + SUBLANES * a:r0 + SUBLANES * a + rows, cols]
                yb = term if yb is None else yb + term
        yb = yb[b:b + CONV_CHUNK]
        out = yb if out is None else out + yb
    return out


def _conv_layer_body(x_ref, g0_ref, win_ref, aw_ref, ab_ref, lng_ref, lnb_ref, bw_ref, wout_ref,
                     g1_ref, o_ref, sta_ref, stb_ref, a_scr, b_scr, gb_scr, mix_scr, u_scr, conv_scr):
    t = pl.program_id(1)
    sub, c = CONV_SUB, CONV_CH
    n_sub = CONV_TILE // sub

    @pl.when(t == 0)
    def _():
        a_scr[0, 0:HALO_A, :] = jnp.zeros((HALO_A, c), F32)
        b_scr[0, 0:HALO_B, :] = jnp.zeros((HALO_B, c), F32)

    def row_block(k):
        return pl.ds(k * sub, sub) if isinstance(k, int) else pl.ds(pl.multiple_of(k * sub, sub), sub)

    def project_steps(k):
        rows = row_block(k)
        nxt = (k + 1) % n_sub if isinstance(k, int) else jnp.where(k + 1 == n_sub, 0, k + 1)
        proj = lambda lo, hi: jnp.dot(u_scr[...], win_ref[:, lo * c:hi * c], preferred_element_type=F32)

        def norm():
            u_scr[...] = _rmsnorm(x_ref[rows, :], g0_ref[...]).astype(BF16)

        def mixer_a():
            za = proj(0, 2)
            ga = za[:, 0:c] * jax.nn.sigmoid(za[:, c:2 * c])
            a_scr[k, HALO_A:HALO_A + sub, :] = ga
            a_scr[nxt, 0:HALO_A, :] = ga[sub - HALO_A:sub]

        def mixer_b_in():
            zb = proj(4, 5) * proj(2, 3)
            b_scr[k, HALO_B:HALO_B + sub, :] = zb
            b_scr[nxt, 0:HALO_B, :] = zb[sub - HALO_B:sub]

        def mixer_b_gate():
            gb_scr[k] = proj(3, 4)

        return [norm, mixer_a, mixer_b_in, mixer_b_gate]

    def mix_steps(k):
        rows = row_block(k)
        a_buf, b_buf, gb_buf, mix_buf = a_scr.at[k], b_scr.at[k], gb_scr.at[k], mix_scr.at[k]
        off_b = HALO_B - (SC_WIDTH - 1)
        steps = []
        for r0 in range(0, sub, CONV_CHUNK):
            chunk = slice(r0, r0 + CONV_CHUNK)
            for s in range(c // LANES):
                def conv_slab(r0=r0, s=s, chunk=chunk):
                    conv_scr[chunk, s * LANES:(s + 1) * LANES] = _conv_a_slab(a_buf, aw_ref, r0, s)
                steps.append(conv_slab)

            def gate(r0=r0, chunk=chunk):
                ya = _layernorm_silu(conv_scr[chunk, :] + ab_ref[...], lng_ref[...], lnb_ref[...])
                cb = bw_ref[0:1, :] * b_buf[off_b + r0:off_b + r0 + CONV_CHUNK, :]
                for j in range(1, SC_WIDTH):
                    cb = cb + bw_ref[j:j + 1, :] * b_buf[off_b + r0 + j:off_b + r0 + j + CONV_CHUNK, :]
                yb = gb_buf[chunk, :] * cb
                mix_buf[chunk, :] = jnp.concatenate([ya, yb], axis=-1).astype(BF16)
            steps.append(gate)

        def out_proj():
            y = jnp.dot(mix_buf[...], wout_ref[...], preferred_element_type=F32)
            o_ref[rows, :] = x_ref[rows, :] + _rmsnorm(y, g1_ref[...])
        steps.append(out_proj)
        return steps

    def interleave(matmul_steps, vector_steps):
        per = -(-len(vector_steps) // max(len(matmul_steps), 1))
        while matmul_steps or vector_steps:
            if matmul_steps:
                matmul_steps.pop(0)()
            for _ in range(per):
                if vector_steps:
                    vector_steps.pop(0)()

    interleave(project_steps(0), [])
    for k in range(n_sub - 1):
        interleave(project_steps(k + 1), mix_steps(k))
    interleave([], mix_steps(n_sub - 1))

    sta_ref[0] = a_scr[0, 0:HALO_A, :]
    stb_ref[0] = b_scr[0, 0:HALO_B, :]


def _conv_layer(x, g0, w_in, aw, ab, lng, lnb, bw, w_out, g1):
    nt = SEQ // CONV_TILE
    n_sub = CONV_TILE // CONV_SUB
    row = lambda b, t: (b * nt + t, 0)
    seq = lambda b, t: (b, 0, 0)
    return pl.pallas_call(
        _conv_layer_body,
        grid=(BATCH, nt),
        in_specs=[pl.BlockSpec((CONV_TILE, D_MODEL), row), _const_spec((1, D_MODEL)),
                  _const_spec((D_MODEL, 5 * CONV_CH), single=True),
                  _const_spec((CONV_WIDTH, CONV_CH)), _const_spec((1, CONV_CH)),
                  _const_spec((1, CONV_CH)), _const_spec((1, CONV_CH)),
                  _const_spec((SC_WIDTH, CONV_CH)), _const_spec((D_MODEL, D_MODEL), single=True),
                  _const_spec((1, D_MODEL))],
        out_specs=[pl.BlockSpec((CONV_TILE, D_MODEL), row),
                   pl.BlockSpec((1, HALO_A, CONV_CH), seq), pl.BlockSpec((1, HALO_B, CONV_CH), seq)],
        out_shape=[jax.ShapeDtypeStruct(x.shape, F32),
                   jax.ShapeDtypeStruct((BATCH, HALO_A, CONV_CH), F32),
                   jax.ShapeDtypeStruct((BATCH, HALO_B, CONV_CH), F32)],
        scratch_shapes=[pltpu.VMEM((n_sub, halo + CONV_SUB, CONV_CH), F32) for halo in (HALO_A, HALO_B, 0)]
                       + [pltpu.VMEM((n_sub, CONV_SUB, 2 * CONV_CH), BF16),
                          pltpu.VMEM((CONV_SUB, D_MODEL), BF16), pltpu.VMEM((CONV_SUB, CONV_CH), F32)],
        compiler_params=_params("arbitrary", "arbitrary"),
        name="conv_layer",
    )(x, g0, w_in, aw, ab, lng, lnb, bw, w_out, g1)


def _conv_mix_dec_body(ga_ref, zb_ref, gb_ref, h_ref, sta_ref, sb0_ref, sb1_ref, aw_ref, ab_ref,
                       lng_ref, lnb_ref, bw_ref, w_ref, g_ref, o_ref, nsta_ref):
    ga = ga_ref[...]
    n_state = CONV_WIDTH - 1
    acc = aw_ref[n_state:n_state + 1, :] * ga
    for j in range(n_state):
        acc = acc + aw_ref[j:j + 1, :] * sta_ref[j]
    for j in range(n_state - 1):
        nsta_ref[j] = sta_ref[j + 1]
    nsta_ref[n_state - 1] = ga
    ya = _layernorm_silu(acc + ab_ref[...], lng_ref[...], lnb_ref[...])
    cb = bw_ref[0:1, :] * sb0_ref[...] + bw_ref[1:2, :] * sb1_ref[...] + bw_ref[2:3, :] * zb_ref[...]
    yb = gb_ref[...] * cb
    mix = jnp.concatenate([ya, yb], axis=-1).astype(BF16)
    y = jnp.dot(mix, w_ref[...], preferred_element_type=F32)
    o_ref[...] = h_ref[...] + _rmsnorm(y, g_ref[...])


def _conv_mix_dec(ga, zb, gb, h, sta, sb0, sb1, aw, ab, lng, lnb, bw, w, g):
    args = (ga, zb, gb, h, sta, sb0, sb1, aw, ab, lng, lnb, bw, w, g)
    return pl.pallas_call(
        _conv_mix_dec_body,
        grid=(1,),
        in_specs=[_const_spec(a.shape) for a in args],
        out_specs=[_const_spec(h.shape), _const_spec(sta.shape)],
        out_shape=[jax.ShapeDtypeStruct(h.shape, F32), jax.ShapeDtypeStruct(sta.shape, F32)],
        compiler_params=_params("arbitrary"),
        name="conv_mix_dec",
    )(*args)


FF_CHUNK = 1024
MLP_TILE = 1024
MIX_MLP_TILE = 512


def _shift_in_lanes(x, new_cols, first_seq, n_seq):
    lane = lax.broadcasted_iota(jnp.int32, (1, LANES), 1)
    new = jnp.concatenate([pltpu.roll(new_cols, LANES - 1 - first_seq - b, 1) for b in range(n_seq)], axis=0)
    n_t = x.shape[1] // LANES
    rolled = [pltpu.roll(x[:, t * LANES:(t + 1) * LANES], LANES - 1, 1) for t in range(n_t)]
    tiles = [jnp.where(lane == LANES - 1, rolled[t + 1] if t + 1 < n_t else new, rolled[t])
             for t in range(n_t)]
    return tiles[0] if n_t == 1 else jnp.concatenate(tiles, axis=1)


def _mlp_block(x, g2_ref, w1_ref, w2_ref, g3_ref):
    u = _rmsnorm(x, g2_ref[...]).astype(BF16)
    acc = jnp.zeros(x.shape, F32)
    for c in range(D_FF // FF_CHUNK):
        sl = slice(c * FF_CHUNK, (c + 1) * FF_CHUNK)
        hid = jnp.dot(u, w1_ref[:, sl], preferred_element_type=F32)
        hid = jnp.square(jnp.maximum(hid, 0.0)).astype(BF16)
        acc = acc + jnp.dot(hid, w2_ref[sl, :], preferred_element_type=F32)
    return x + _rmsnorm(acc, g3_ref[...])


def _mlp_body(xp_ref, xs_ref, g2_ref, w1_ref, w2_ref, g3_ref, *rest):
    n_cast = (len(rest) - 2) // 2
    cast_in, (op_ref, os_ref), cast_out = rest[:n_cast], rest[n_cast:n_cast + 2], rest[n_cast + 2:]
    i, n = pl.program_id(0), pl.num_programs(0) - 1

    @pl.when(i < n)
    def _():
        for src, dst in zip(cast_in, cast_out):
            dst[...] = src[...].astype(BF16)
        op_ref[...] = _mlp_block(xp_ref[...], g2_ref, w1_ref, w2_ref, g3_ref)

    @pl.when(i == n)
    def _():
        os_ref[...] = _mlp_block(xs_ref[...], g2_ref, w1_ref, w2_ref, g3_ref)


def _mix_mlp_body(mp_ref, ms_ref, hp_ref, hs_ref, wo_ref, g1_ref, g2_ref, w1_ref, w2_ref, g3_ref,
                  cache_ref, kvt_ref, op_ref, os_ref, ncache_ref, *, cache_rows):
    i, n = pl.program_id(0), pl.num_programs(0) - 1
    n_seq = cache_ref.shape[0]

    def block(mix_ref, h_ref):
        y = jnp.dot(mix_ref[...].astype(BF16), wo_ref[...], preferred_element_type=F32)
        x = h_ref[...] + _rmsnorm(y, g1_ref[...])
        return _mlp_block(x, g2_ref, w1_ref, w2_ref, g3_ref)

    @pl.when(i < n)
    def _():
        for part in range(2):
            x = cache_ref[:, part]
            r0 = cache_rows + part * HEAD_DIM
            new = _shift_in_lanes(x.reshape(n_seq * HEAD_DIM, x.shape[2]), kvt_ref[r0:r0 + HEAD_DIM, :],
                                  i * n_seq, n_seq)
            ncache_ref[:, part] = new.reshape(x.shape)
        op_ref[...] = block(mp_ref, hp_ref)

    @pl.when(i == n)
    def _():
        os_ref[...] = block(ms_ref, hs_ref)


def _two_group_specs(xp, xs, tm):
    n = xp.shape[0] // tm
    prompt = pl.BlockSpec((tm, xp.shape[1]), lambda i: (jnp.minimum(i, n - 1), 0))
    return n, prompt, _const_spec(xs.shape)


def _mix_mlp(mix_p, mix_s, hp, hs, wo, g1, g2, w1, w2, g3, cache, kvt, cache_rows, tm):
    n, mp_spec, ms_spec = _two_group_specs(mix_p, mix_s, tm)
    _, hp_spec, hs_spec = _two_group_specs(hp, hs, tm)
    n_seq = cache.shape[0] // n
    assert n_seq * n == cache.shape[0]
    c_spec = pl.BlockSpec((n_seq,) + cache.shape[1:], lambda i: (jnp.minimum(i, n - 1), 0, 0, 0))
    vec = _const_spec((1, D_MODEL))
    return pl.pallas_call(
        functools.partial(_mix_mlp_body, cache_rows=cache_rows),
        grid=(n + 1,),
        in_specs=[mp_spec, ms_spec, hp_spec, hs_spec, _const_spec(wo.shape, single=True), vec, vec,
                  _const_spec((D_MODEL, D_FF), single=True), _const_spec((D_FF, D_MODEL), single=True), vec,
                  c_spec, _const_spec(kvt.shape)],
        out_specs=[hp_spec, hs_spec, c_spec],
        out_shape=[jax.ShapeDtypeStruct(hp.shape, F32), jax.ShapeDtypeStruct(hs.shape, F32),
                   jax.ShapeDtypeStruct(cache.shape, F32)],
        compiler_params=_params("arbitrary"),
        name="mix_mlp",
    )(mix_p, mix_s, hp, hs, wo, g1, g2, w1, w2, g3, cache, kvt)


def _mlp(xp, xs, g2, w1, w2, g3, tm, to_bf16=()):
    n, p_spec, s_spec = _two_group_specs(xp, xs, tm)
    vec = _const_spec((1, D_MODEL))
    cast_in, cast_out, cast_shapes = [], [], []
    for w, idx in to_bf16:
        rows, cols = w.shape[1] // n, w.shape[2]
        assert rows * n == w.shape[1] and rows % 16 == 0
        step_rows = lambda i: jnp.minimum(i, n - 1)
        cast_in.append(pl.BlockSpec((None, rows, cols), lambda i, idx=idx: (idx, step_rows(i), 0)))
        cast_out.append(pl.BlockSpec((rows, cols), lambda i: (step_rows(i), 0)))
        cast_shapes.append(jax.ShapeDtypeStruct(w.shape[1:], BF16))
    return pl.pallas_call(
        _mlp_body,
        grid=(n + 1,),
        in_specs=[p_spec, s_spec, vec,
                  _const_spec((D_MODEL, D_FF), single=True), _const_spec((D_FF, D_MODEL), single=True), vec]
                 + cast_in,
        out_specs=[p_spec, s_spec] + cast_out,
        out_shape=[jax.ShapeDtypeStruct(xp.shape, F32), jax.ShapeDtypeStruct(xs.shape, F32)] + cast_shapes,
        compiler_params=_params("arbitrary"),
        name="mlp",
    )(xp, xs, g2, w1, w2, g3, *[w for w, _ in to_bf16])


def _rope_slab(z, c, s_lo, s_hi):
    return z * c + pltpu.roll(z, LANES - ROT_HALF, 1) * s_lo + pltpu.roll(z, ROT_HALF, 1) * s_hi


def _attn_in_body(x_ref, g_ref, w_ref, ca_ref, sla_ref, sha_ref, cb_ref, slb_ref, shb_ref,
                  qn_ref, qd_ref, kv_ref, *, scale):
    u = _rmsnorm(x_ref[...], g_ref[...]).astype(BF16)
    z = jnp.dot(u, w_ref[...], preferred_element_type=F32)
    ca, sla, sha = ca_ref[...], sla_ref[...], sha_ref[...]
    for s in range(Q_COLS // LANES):
        sl = slice(s * LANES, (s + 1) * LANES)
        q = _rope_slab(z[:, sl], ca, sla, sha) * scale
        if s < QN_SLABS:
            qn_ref[:, sl] = q.astype(BF16)
        else:
            qd_ref[s - QN_SLABS] = q
    kv_ref[0] = _rope_slab(z[:, Q_COLS:Q_COLS + LANES], ca, sla, sha)
    kv_ref[1] = z[:, Q_COLS + LANES:Q_COLS + 2 * LANES]
    cb, slb, shb = cb_ref[...], slb_ref[...], shb_ref[...]
    for s in range(2, KV_SLABS):
        sl = slice(Q_COLS + s * LANES, Q_COLS + (s + 1) * LANES)
        kv_ref[s] = _rope_slab(z[:, sl], cb, slb, shb)


QN_SLABS = (SWA_HEADS + DIL_HEADS) * HEAD_DIM // LANES
QD_SLABS = Q_COLS // LANES - QN_SLABS
KV_SLABS = KV_COLS // LANES


def _attn_in(x, g, w, tabs, tm, scale):
    t = x.shape[0]
    row = lambda i: (i, 0)
    slab = lambda i: (0, i, 0)
    nper = tabs[0].shape[0] // tm
    tab = pl.BlockSpec((tm, LANES), lambda i: (i % nper, 0))
    return pl.pallas_call(
        functools.partial(_attn_in_body, scale=scale),
        grid=(t // tm,),
        in_specs=[pl.BlockSpec((tm, D_MODEL), row), _const_spec((1, D_MODEL)),
                  _const_spec((D_MODEL, ATTN_COLS), single=True)] + [tab] * 6,
        out_specs=[pl.BlockSpec((tm, QN_SLABS * LANES), row), pl.BlockSpec((QD_SLABS, tm, LANES), slab),
                   pl.BlockSpec((KV_SLABS, tm, LANES), slab)],
        out_shape=[jax.ShapeDtypeStruct((t, QN_SLABS * LANES), BF16),
                   jax.ShapeDtypeStruct((QD_SLABS, t, LANES), F32),
                   jax.ShapeDtypeStruct((KV_SLABS, t, LANES), F32)],
        compiler_params=_params("arbitrary"),
        name="attn_in",
    )(x, g, w, *tabs)


LOG2E = math.log2(math.e)
ATTN_STEP = 4 * BLK
ATTN_LOOKAHEAD = 2
_NT = (((1,), (1,)), ((), ()))
_TN = (((0,), (0,)), ((), ()))


def _both_halves(x, in_hi):
    lane = lax.broadcasted_iota(jnp.int32, x.shape, 1)
    return jnp.where(lane >= HEAD_DIM if in_hi else lane < HEAD_DIM, x, pltpu.roll(x, HEAD_DIM, 1))


def _block_scores(q_slabs, k_tiles, k_hi, bias):
    in_a = lax.broadcasted_iota(jnp.int32, (BLK, LANES), 1) < HEAD_DIM
    kk = jnp.concatenate([_both_halves(t, k_hi) for t in k_tiles], axis=0).astype(BF16)
    zero = jnp.zeros((BLK, LANES), BF16)
    scores = []
    for qs in q_slabs:
        qq = jnp.concatenate([jnp.where(in_a, qs, zero), jnp.where(in_a, zero, qs)], axis=0)
        scores.append(lax.dot_general(kk, qq, _NT, preferred_element_type=F32) + bias)
    return scores


def _block_outputs(scores, v_tiles, v_hi, sinks):
    lane = lax.broadcasted_iota(jnp.int32, (BLK, LANES), 1)
    one_lane = 0 if v_hi else HEAD_DIM
    vv = jnp.concatenate([jnp.where(lane == one_lane, 1.0, t) for t in v_tiles], axis=0).astype(BF16)
    probs, maxes = [], []
    for slab, sh in enumerate(scores):
        m = jnp.max(sh, axis=0, keepdims=True)
        if sinks is not None:
            m = jnp.maximum(m, sinks[slab])
        probs.append(jnp.exp2(sh - m).astype(BF16))
        maxes.append(m)
    v0 = HEAD_DIM if v_hi else 0
    outs = []
    for slab, p in enumerate(probs):
        ot = lax.dot_general(vv, p, _TN, preferred_element_type=F32)
        m = maxes[slab]
        den = ot[one_lane:one_lane + 1, :]
        if sinks is not None:
            den = den + jnp.exp2(sinks[slab] - m)
        o_t = jnp.concatenate([ot[v0:v0 + HEAD_DIM, 0:BLK], ot[v0:v0 + HEAD_DIM, BLK:2 * BLK]], axis=0)
        tile = lambda row: jnp.concatenate([jnp.broadcast_to(row[:, 0:BLK], (HEAD_DIM, BLK)),
                                            jnp.broadcast_to(row[:, BLK:2 * BLK], (HEAD_DIM, BLK))], axis=0)
        outs.append(((o_t / tile(den)).T, tile(m + jnp.log2(den)).T))
    return outs


N_RES = 4


def _attn_seq_body(qn_ref, qd_ref, kv_ref, sink_ref, mix_ref, qc1, kc1, qc2, kc2, res):
    key2 = lax.broadcasted_iota(jnp.int32, (2 * BLK, 2 * BLK), 0)
    qry2 = lax.broadcasted_iota(jnp.int32, (2 * BLK, 2 * BLK), 1) & (BLK - 1)
    slack = jnp.where(key2 < BLK, key2 - qry2, qry2 - key2 + BLK)
    in_prev = jnp.where(key2 < BLK, 4 * BLK, 0)
    neg_inf = jnp.float32(-jnp.inf)
    bias_both = jnp.where(slack >= 0, 0.0, neg_inf)
    bias_cur = bias_both[BLK:2 * BLK]

    for gi, (qc, kc) in ((1, (qc1, kc1)), (2, (qc2, kc2))):
        dil = DIL_PATTERNS[gi][1]
        n = SEQ // dil
        for r in range(dil):
            dst = slice(r * n, (r + 1) * n)
            for s in range(2):
                qc[dst, s * LANES:(s + 1) * LANES] = qd_ref[2 * (gi - 1) + s, pl.ds(r, n, stride=dil), :].astype(BF16)
            kc[dst, :] = kv_ref[2 + gi, pl.ds(r, n, stride=dil), :]

    def run(q_src, q_col0, n_slabs, k_src, v_src, n_kv, packed, chain, sinks, write):
        slabs_per_kv = n_slabs // n_kv

        def step(it, carry):
            base = pl.multiple_of(it * ATTN_STEP, ATTN_STEP)
            items = [(j, kvh) for j in range(ATTN_STEP // BLK) for kvh in range(n_kv)]

            def key_rows(j):
                r0 = pl.multiple_of(base + j * BLK, BLK)
                if chain == "block" or (chain == "step" and j == 0):
                    return r0, [pl.ds(r0, BLK)], bias_cur
                if chain == "seq" and j == 0:
                    prev = pl.ds(pl.multiple_of(jnp.maximum(r0 - BLK, 0), BLK), BLK)
                    gone = in_prev * jnp.where(it > 0, 0, 1)
                    return r0, [prev, pl.ds(r0, BLK)], jnp.where(slack - gone >= 0, 0.0, neg_inf)
                return r0, [pl.ds(pl.multiple_of(r0 - BLK, BLK), BLK), pl.ds(r0, BLK)], bias_both

            def scores_of(item):
                j, kvh = item
                r0, krows, bias = key_rows(j)
                slabs = range(kvh * slabs_per_kv, (kvh + 1) * slabs_per_kv)
                q_slabs = [q_src[pl.ds(r0, BLK), q_col0 + s * LANES:q_col0 + (s + 1) * LANES] for s in slabs]
                k_hi = (not packed) and kvh == 1
                return _block_scores(q_slabs, [k_src[r, :] for r in krows], k_hi, bias)

            def finish(item, scores):
                j, kvh = item
                r0, krows, _ = key_rows(j)
                slabs = range(kvh * slabs_per_kv, (kvh + 1) * slabs_per_kv)
                sk = None if sinks is None else [sinks[s] for s in slabs]
                v_hi = packed or kvh == 1
                outs = _block_outputs(scores, [v_src[r, :] for r in krows], v_hi, sk)
                for s, (o, lse) in zip(slabs, outs):
                    write(it, j, r0, s, o, lse)

            ahead = min(ATTN_LOOKAHEAD, len(items))
            pending = [scores_of(item) for item in items[:ahead]]
            for n, item in enumerate(items):
                if n + ahead < len(items):
                    pending.append(scores_of(items[n + ahead]))
                finish(item, pending.pop(0))
            return carry

        lax.fori_loop(0, SEQ // ATTN_STEP, step, 0)

    sinks = [jnp.concatenate([jnp.broadcast_to(sink_ref[0:1, 2 * s + half:2 * s + half + 1] * LOG2E, (1, BLK))
                              for half in range(2)], axis=1) for s in range(SWA_HEADS // 2)]

    def write_swa(it, j, r0, s, o, lse):
        mix_ref[pl.ds(r0, BLK), s * LANES:(s + 1) * LANES] = o.astype(BF16)

    run(qn_ref, 0, SWA_HEADS // 2, kv_ref.at[0], kv_ref.at[1], SWA_KV, False, "seq", sinks, write_swa)

    def write_res(group, start_of, stride):
        def write(it, j, r0, s, o, lse):
            dst = pl.ds(start_of(it, j, r0), BLK) if stride == 1 else pl.ds(start_of(it, j, r0), BLK, stride=stride)
            res[group * N_RES + s, dst, :] = o
            res[group * N_RES + 2 + s, dst, :] = lse
        return write

    n_dil_slabs = DIL_HEADS // 2
    run(qn_ref, SWA_HEADS * HEAD_DIM, n_dil_slabs, kv_ref.at[2], kv_ref.at[2], 1, True, "seq", None,
        write_res(0, lambda it, j, r0: r0, 1))
    d1, d2 = DIL_PATTERNS[1][1], DIL_PATTERNS[2][1]
    run(qc1, 0, n_dil_slabs, kc1, kc1, 1, True, "step", None,
        write_res(1, lambda it, j, r0: d1 * j * BLK + it, d1))
    run(qc2, 0, n_dil_slabs, kc2, kc2, 1, True, "block", None,
        write_res(2, lambda it, j, r0: it * (ATTN_STEP // BLK) + j, d2))

    def merge(c, carry):
        rows = pl.ds(pl.multiple_of(c * ATTN_STEP, ATTN_STEP), ATTN_STEP)
        for s in range(n_dil_slabs):
            lse = [res[g * N_RES + 2 + s, rows, :] for g in range(3)]
            m = jnp.maximum(jnp.maximum(lse[0], lse[1]), lse[2])
            e = [jnp.exp2(l - m) for l in lse]
            o = e[0] * res[s, rows, :] + e[1] * res[N_RES + s, rows, :] + e[2] * res[2 * N_RES + s, rows, :]
            col = SWA_HEADS * HEAD_DIM + s * LANES
            mix_ref[rows, col:col + LANES] = (o / (e[0] + e[1] + e[2])).astype(BF16)
        return carry

    lax.fori_loop(0, SEQ // ATTN_STEP, merge, 0)


def _attn_seq(qn, qd, kv, sinks):
    seq2 = lambda b: (b, 0)
    seq3 = lambda b: (0, b, 0)
    wd = DIL_HEADS * HEAD_DIM
    return pl.pallas_call(
        _attn_seq_body,
        grid=(BATCH,),
        in_specs=[pl.BlockSpec((SEQ, QN_SLABS * LANES), seq2), pl.BlockSpec((QD_SLABS, SEQ, LANES), seq3),
                  pl.BlockSpec((KV_SLABS, SEQ, LANES), seq3), _const_spec(sinks.shape)],
        out_specs=pl.BlockSpec((SEQ, MIX_ATTN), seq2),
        out_shape=jax.ShapeDtypeStruct((BATCH * SEQ, MIX_ATTN), BF16),
        scratch_shapes=[pltpu.VMEM((SEQ, wd), BF16), pltpu.VMEM((SEQ, LANES), F32),
                        pltpu.VMEM((SEQ, wd), BF16), pltpu.VMEM((SEQ, LANES), F32),
                        pltpu.VMEM((3 * N_RES, SEQ, LANES), F32)],
        compiler_params=_params("arbitrary"),
        name="attn_seq",
    )(qn, qd, kv, sinks)


DEC_TILE = 4
N_MIX_HEADS = SWA_HEADS + DIL_HEADS


def _attn_dec_body(q_ref, kv_ref, kv_all_ref, sink_ref, csw_ref, cd0_ref, cd1_ref, cd2_ref,
                   nsw_ref, nd0_ref, nd1_ref, o_ref, kvt_ref):
    step = pl.program_id(0)
    bb, hd, grp = DEC_TILE, HEAD_DIM, DIL_HEADS
    n_rows = grp * bb

    @pl.when(step == 0)
    def _():
        kvt_ref[...] = kv_all_ref[...].T

    q = q_ref[0]
    kv = kv_ref[0]
    row_seq = lax.broadcasted_iota(jnp.int32, (n_rows, bb * hd), 0) & (bb - 1)
    col_seq = lax.broadcasted_iota(jnp.int32, (n_rows, bb * hd), 1) // hd
    diag = row_seq == col_seq
    lane = lax.broadcasted_iota(jnp.int32, (1, LANES), 1)
    nt = (((1,), (1,)), ((), ()))

    def q_rows(c0):
        return jnp.concatenate([q[:, c0 + s * hd:c0 + (s + 1) * hd] for s in range(grp)], axis=0)

    def per_row(x):
        return jnp.concatenate([x] * grp, axis=0)

    def block_diag(qr):
        return jnp.where(diag, jnp.concatenate([qr] * bb, axis=1), 0.0).astype(BF16)

    def take_diag(ob):
        ob = jnp.where(diag, ob, 0.0)
        out = ob[:, 0:hd]
        for b in range(1, bb):
            out = out + ob[:, b * hd:(b + 1) * hd]
        return out

    def shift_in(x, r0):
        return _shift_in_lanes(x, kvt_ref[r0:r0 + hd, :], step * bb, bb)

    def stack(ref, idx):
        x = ref[:, idx]
        return x.reshape(bb * hd, x.shape[2])

    def unstack(x):
        return x.reshape(bb, hd, x.shape[1])

    cd_refs = (cd0_ref, cd1_ref, cd2_ref)
    nd_refs = (nd0_ref, nd1_ref, None)
    n_dil = len(DIL_PATTERNS)
    jobs = [dict(src=csw_ref, dst=nsw_ref, ki=kvh, vi=SWA_KV + kvh, q0=kvh * grp * hd,
                 kc=kvh * hd, vc=LANES + kvh * hd, dil=1) for kvh in range(SWA_KV)]
    jobs += [dict(src=cd_refs[gi], dst=nd_refs[gi], ki=0, vi=1, q0=SWA_HEADS * hd + gi * grp * hd,
                  kc=(2 + gi) * LANES, vc=(2 + gi) * LANES + hd, dil=DIL_PATTERNS[gi][1]) for gi in range(n_dil)]

    for jb in jobs:
        k = stack(jb["src"], jb["ki"])
        qr = q_rows(jb["q0"])
        s = jnp.dot(block_diag(qr), k.astype(BF16), preferred_element_type=F32)
        if jb["dil"] > 1:
            pos = lax.broadcasted_iota(jnp.int32, (1, k.shape[1]), 1)
            s = jnp.where((pos & (jb["dil"] - 1)) == 0, s, -jnp.inf)
        jb["s"] = s
        jb["s_new"] = jnp.sum(qr * per_row(kv[:, jb["kc"]:jb["kc"] + hd]), axis=1, keepdims=True)
        jb["v_new"] = per_row(kv[:, jb["vc"]:jb["vc"] + hd])
        jb["m"] = jnp.maximum(jnp.max(s, axis=1, keepdims=True), jb["s_new"])

    m_dil = jobs[SWA_KV]["m"]
    for jb in jobs[SWA_KV + 1:]:
        m_dil = jnp.maximum(m_dil, jb["m"])
    for kvh, jb in enumerate(jobs):
        if kvh < SWA_KV:
            sk = jnp.concatenate([jnp.broadcast_to(sink_ref[0:1, kvh * grp + s:kvh * grp + s + 1], (bb, 1))
                                  for s in range(grp)], axis=0)
            m = jnp.maximum(jb["m"], sk)
            jb["extra"] = jnp.exp(sk - m)
        else:
            m = m_dil
            jb["extra"] = 0.0
        p = jnp.exp(jb["s"] - m)
        jb["p_new"] = jnp.exp(jb["s_new"] - m)
        jb["den"] = jnp.sum(p, axis=1, keepdims=True) + jb["p_new"] + jb["extra"]
        jb["p"] = p.astype(BF16)

    for jb in jobs:
        v = stack(jb["src"], jb["vi"])
        pv = lax.dot_general(jb["p"], v.astype(BF16), nt, preferred_element_type=F32)
        jb["acc"] = take_diag(pv) + jb["p_new"] * jb["v_new"]

    for kvh in range(SWA_KV):
        o = jobs[kvh]["acc"] / jobs[kvh]["den"]
        for s_ in range(grp):
            o_ref[0, kvh * grp + s_] = o[s_ * bb:(s_ + 1) * bb]
    dil_jobs = jobs[SWA_KV:]
    o = sum(jb["acc"] for jb in dil_jobs[1:]) + dil_jobs[0]["acc"]
    o = o / (sum(jb["den"] for jb in dil_jobs[1:]) + dil_jobs[0]["den"])
    for s_ in range(grp):
        o_ref[0, SWA_HEADS + s_] = o[s_ * bb:(s_ + 1) * bb]

    for jb in jobs:
        if jb["dst"] is not None:
            jb["dst"][:, jb["ki"]] = unstack(shift_in(stack(jb["src"], jb["ki"]), jb["kc"]))
            jb["dst"][:, jb["vi"]] = unstack(shift_in(stack(jb["src"], jb["vi"]), jb["vc"]))


def _attn_dec(q, kv, sinks, csw, cd0, cd1, cd2):
    caches = (csw, cd0, cd1, cd2)
    n_tiles = DEC_BATCH // DEC_TILE
    q3 = q.reshape(n_tiles, DEC_TILE, Q_COLS)
    kv3 = kv.reshape(n_tiles, DEC_TILE, KV_COLS)
    tile3 = lambda w: pl.BlockSpec((1, DEC_TILE, w), lambda i: (i, 0, 0))
    cspec = lambda c: pl.BlockSpec((DEC_TILE,) + c.shape[1:], lambda i: (i, 0, 0, 0))
    o_shape = (n_tiles, N_MIX_HEADS, DEC_TILE, HEAD_DIM)
    shifted = caches[:3]
    outs = pl.pallas_call(
        _attn_dec_body,
        grid=(n_tiles,),
        in_specs=[tile3(Q_COLS), tile3(KV_COLS), _const_spec(kv.shape), _const_spec(sinks.shape)]
                 + [cspec(c) for c in caches],
        out_specs=[cspec(c) for c in shifted]
                  + [pl.BlockSpec((1,) + o_shape[1:], lambda i: (i, 0, 0, 0)),
                     _const_spec((KV_COLS, DEC_BATCH))],
        out_shape=[jax.ShapeDtypeStruct(c.shape, F32) for c in shifted]
                  + [jax.ShapeDtypeStruct(o_shape, F32), jax.ShapeDtypeStruct((KV_COLS, DEC_BATCH), F32)],
        compiler_params=_params("arbitrary"),
        name="attn_dec",
    )(q3, kv3, kv, sinks, *caches)
    mix = jnp.transpose(outs[3], (0, 2, 1, 3)).reshape(DEC_BATCH, MIX_ATTN)
    return outs[0], outs[1], outs[2], mix, outs[4]


def _attn_weight_order(w):
    hd = HEAD_DIM
    nq, nkv = SWA_HEADS * hd, SWA_KV * hd
    base = nq + 2 * nkv
    per = (DIL_HEADS + 2) * hd
    qd = [w[:, base + g * per:base + g * per + DIL_HEADS * hd] for g in range(3)]
    kvd = [w[:, base + g * per + DIL_HEADS * hd:base + (g + 1) * per] for g in range(3)]
    return jnp.concatenate([w[:, :nq]] + qd + [w[:, nq:base]] + kvd, axis=1)


def _rope_angles(pos):
    inv = ROPE_THETA ** (-jnp.arange(ROT_HALF, dtype=F32) / ROT_HALF)
    ang = pos.astype(F32)[:, None] * inv[None, :]
    return jnp.cos(ang), jnp.sin(ang)


def _rope_tables(pos):
    cos, sin = _rope_angles(pos)
    n = pos.shape[0]
    ones = lambda w: jnp.ones((n, w), F32)
    zeros = lambda w: jnp.zeros((n, w), F32)
    rest = HEAD_DIM - 2 * ROT_HALF
    c_h = jnp.concatenate([cos, cos, ones(rest)], axis=1)
    lo_h = jnp.concatenate([-sin, zeros(ROT_HALF), zeros(rest)], axis=1)
    hi_h = jnp.concatenate([zeros(ROT_HALF), sin, zeros(rest)], axis=1)
    both = tuple(jnp.concatenate([t, t], axis=1) for t in (c_h, lo_h, hi_h))
    first = (jnp.concatenate([c_h, ones(HEAD_DIM)], axis=1),
             jnp.concatenate([lo_h, zeros(HEAD_DIM)], axis=1),
             jnp.concatenate([hi_h, zeros(HEAD_DIM)], axis=1))
    return both + first


def _cache_view(c):
    b, l, two, kv, hd = c.shape
    return jnp.transpose(c, (0, 2, 3, 4, 1)).reshape(b, two * kv, hd, l)


def _cache_unview(c, kv):
    b, _, hd, l = c.shape
    return jnp.transpose(c.reshape(b, 2, kv, hd, l), (0, 4, 1, 2, 3))


def kernel(x_prompt, x_sample, state_conv_a, state_conv_b, cache_swa_kv, cache_dil0_kv, cache_dil1_kv,
           cache_dil2_kv, norm_g, w_in_conv, conv_a_w, conv_a_b, conv_a_ln_g, conv_a_ln_b, conv_b_w,
           w_out_conv, w_in_attn, attn_sinks, w_out_attn, mlp_w1, mlp_w2):
    tm = 512
    hp = x_prompt.reshape(BATCH * SEQ, D_MODEL)
    hs = x_sample.reshape(DEC_BATCH, D_MODEL)
    g = lambda layer, i: norm_g[layer, i].reshape(1, D_MODEL)

    w_in0 = w_in_conv[0].astype(BF16)
    w_out0 = w_out_conv[0].astype(BF16)
    conv_small = (conv_a_w[0], conv_a_b, conv_a_ln_g, conv_a_ln_b, conv_b_w[0])
    hp, sta_p, stb_p = _conv_layer(hp, g(0, 0), w_in0, *conv_small, w_out0, g(0, 1))
    ga_s, zb_s, gb_s = _conv_in(hs, g(0, 0), w_in0, DEC_BATCH)
    sta = jnp.transpose(state_conv_a[0], (1, 0, 2))
    sb0, sb1 = state_conv_b[0, :, 0], state_conv_b[0, :, 1]
    hs, new_sta = _conv_mix_dec(ga_s, zb_s, gb_s, hs, sta, sb0, sb1, *conv_small, w_out0, g(0, 1))

    later = ((w_in_attn, 0), (w_out_attn, 0), (mlp_w1, 1), (mlp_w2, 1))
    hp, hs, w_in1, w_out1, w1_1, w2_1 = _mlp(hp, hs, g(0, 2), mlp_w1[0].astype(BF16), mlp_w2[0].astype(BF16),
                                             g(0, 3), MLP_TILE, to_bf16=later)

    w_in1 = _attn_weight_order(w_in1)
    softmax_scale = HEAD_DIM ** -0.5
    qn_p, qd_p, kv_p = _attn_in(hp, g(1, 0), w_in1, _rope_tables(jnp.arange(SEQ)), tm,
                                softmax_scale * LOG2E)
    mix_p = _attn_seq(qn_p, qd_p, kv_p, attn_sinks)

    tabs_s = _rope_tables(jnp.full((DEC_BATCH,), PAST_LEN, jnp.int32))
    qn_s, qd_s, kv_s = _attn_in(hs, g(1, 0), w_in1, tabs_s, DEC_BATCH, softmax_scale)
    q_s = jnp.concatenate([qn_s.astype(F32)] + [qd_s[s] for s in range(QD_SLABS)], axis=1)
    kv_s = jnp.concatenate([kv_s[s] for s in range(KV_SLABS)], axis=1)
    caches = (cache_swa_kv[0], cache_dil0_kv[0], cache_dil1_kv[0], cache_dil2_kv[0])
    views = [_cache_view(c) for c in caches]
    nsw, nd0, nd1, mix_s, kvt_s = _attn_dec(q_s, kv_s, attn_sinks, *views)

    hp, hs, nd2 = _mix_mlp(mix_p, mix_s, hp, hs, w_out1, g(1, 1), g(1, 2), w1_1, w2_1, g(1, 3),
                           views[3], kvt_s, (KV_SLABS - 1) * LANES, MIX_MLP_TILE)

    n_a, n_b = CONV_WIDTH - 1, SC_WIDTH - 1
    kv4 = kv_p.reshape(KV_SLABS, BATCH, SEQ, LANES)
    swa_p = jnp.stack([kv4[0, :, SEQ - BLK:], kv4[1, :, SEQ - BLK:]], axis=2)
    swa_p = swa_p.reshape(BATCH, BLK, 2, SWA_KV, HEAD_DIM)
    dil_p = [kv4[2 + gi, :, SEQ - min(w, SEQ):].reshape(BATCH, min(w, SEQ), 2, 1, HEAD_DIM)
             for gi, (w, _) in enumerate(DIL_PATTERNS)]
    return (hp.reshape(BATCH, SEQ, D_MODEL), hs.reshape(DEC_BATCH, 1, D_MODEL),
            sta_p[None, :, HALO_A - n_a:],
            jnp.transpose(new_sta, (1, 0, 2))[None],
            stb_p[None, :, HALO_B - n_b:],
            jnp.stack([sb1, zb_s], axis=1)[None],
            swa_p[None], _cache_unview(nsw, SWA_KV)[None],
            dil_p[0][None], _cache_unview(nd0, 1)[None],
            dil_p[1][None], _cache_unview(nd1, 1)[None],
            dil_p[2][None], _cache_unview(nd2, 1)[None])
```

```python
import functools
import math

import jax
import jax.numpy as jnp
import numpy as np
from jax import lax
from jax.experimental import pallas as pl
from jax.experimental.pallas import tpu as pltpu

F32 = jnp.float32
BF16 = jnp.bfloat16

D_MODEL = 1024
BATCH = 8
SEQ = 2048
DEC_BATCH = 128
PAST_LEN = 8192
HEAD_DIM = 64
ROT_HALF = 8
ROPE_THETA = 500000.0
D_FF = 4 * D_MODEL
EPS = 1e-6
CONV_CH = 512
CONV_WIDTH = 31
SC_WIDTH = 3
SWA_HEADS = 8
SWA_KV = 2
DIL_HEADS = 4
DIL_PATTERNS = ((128, 1), (512, 4), (2048, 16))
Q_COLS = (SWA_HEADS + 3 * DIL_HEADS) * HEAD_DIM
KV_COLS = 2 * SWA_KV * HEAD_DIM + 3 * 2 * HEAD_DIM
ATTN_COLS = Q_COLS + KV_COLS
MIX_ATTN = (SWA_HEADS + DIL_HEADS) * HEAD_DIM
LANES = 128
BLK = 128

V7X_VMEM_BYTES = 64 * 1024 * 1024
VMEM_LIMIT = V7X_VMEM_BYTES - 8 * 1024 * 1024


def _params(*sem):
    return pltpu.CompilerParams(dimension_semantics=sem, vmem_limit_bytes=VMEM_LIMIT)


def _const_spec(shape, single=False):
    zeros = (0,) * len(shape)
    if single:
        return pl.BlockSpec(shape, lambda *_: zeros, pipeline_mode=pl.Buffered(1))
    return pl.BlockSpec(shape, lambda *_: zeros)


def _rmsnorm(x, g):
    return x * lax.rsqrt(jnp.mean(x * x, axis=-1, keepdims=True) + EPS) * g


def _conv_in_body(x_ref, g_ref, w_ref, ga_ref, zb_ref, gb_ref):
    u = _rmsnorm(x_ref[...], g_ref[...]).astype(BF16)
    z = jnp.dot(u, w_ref[...], preferred_element_type=F32)
    c = CONV_CH
    ga_ref[...] = z[:, 0:c] * jax.nn.sigmoid(z[:, c:2 * c])
    zb_ref[...] = z[:, 4 * c:5 * c] * z[:, 2 * c:3 * c]
    gb_ref[...] = z[:, 3 * c:4 * c]


def _conv_in(x, g, w, tm):
    t = x.shape[0]
    row = lambda i: (i, 0)
    out = jax.ShapeDtypeStruct((t, CONV_CH), F32)
    return pl.pallas_call(
        _conv_in_body,
        grid=(t // tm,),
        in_specs=[pl.BlockSpec((tm, D_MODEL), row), _const_spec((1, D_MODEL)),
                  _const_spec((D_MODEL, 5 * CONV_CH))],
        out_specs=[pl.BlockSpec((tm, CONV_CH), row)] * 3,
        out_shape=[out] * 3,
        compiler_params=_params("arbitrary"),
        name="conv_in",
    )(x, g, w)


HALO_A = 32
HALO_B = 8
CONV_CHUNK = 128


def _layernorm_silu(c, g, b):
    mu = jnp.mean(c, axis=-1, keepdims=True)
    d = c - mu
    var = jnp.mean(d * d, axis=-1, keepdims=True)
    y = d * lax.rsqrt(var + EPS) * g + b
    return y * jax.nn.sigmoid(y)


SUBLANES = 8
CONV_TILE = 1024
CONV_SUB = 256


def _conv_a_slab(ext_a, aw_ref, r0, s):
    cols = slice(s * LANES, (s + 1) * LANES)
    out = None
    for b in range(SUBLANES):
        rows = CONV_CHUNK if b == 0 else CONV_CHUNK + SUBLANES
        yb = None
        for a in range((CONV_WIDTH + 1) // SUBLANES + 1):
            j = SUBLANES * a + b - (HALO_A - (CONV_WIDTH - 1))
            if 0 <= j < CONV_WIDTH:
                term = aw_ref[j:j + 1, cols] * ext_a[r0 + SUBLANES * a:r0 + SUBLANES * a + rows, cols]
                yb = term if yb is None else yb + term
        yb = yb[b:b + CONV_CHUNK]
        out = yb if out is None else out + yb
    return out


def _conv_layer_body(x_ref, g0_ref, win_ref, aw_ref, ab_ref, lng_ref, lnb_ref, bw_ref, wout_ref,
                     g1_ref, o_ref, sta_ref, stb_ref, a_scr, b_scr, gb_scr, mix_scr, u_scr, conv_scr):
    t = pl.program_id(1)
    sub, c = CONV_SUB, CONV_CH
    n_sub = CONV_TILE // sub

    @pl.when(t == 0)
    def _():
        a_scr[0, 0:HALO_A, :] = jnp.zeros((HALO_A, c), F32)
        b_scr[0, 0:HALO_B, :] = jnp.zeros((HALO_B, c), F32)

    def row_block(k):
        return pl.ds(k * sub, sub) if isinstance(k, int) else pl.ds(pl.multiple_of(k * sub, sub), sub)

    def project_steps(k):
        rows = row_block(k)
        nxt = (k + 1) % n_sub if isinstance(k, int) else jnp.where(k + 1 == n_sub, 0, k + 1)
        proj = lambda lo, hi: jnp.dot(u_scr[...], win_ref[:, lo * c:hi * c], preferred_element_type=F32)

        def norm():
            u_scr[...] = _rmsnorm(x_ref[rows, :], g0_ref[...]).astype(BF16)

        def mixer_a():
            za = proj(0, 2)
            ga = za[:, 0:c] * jax.nn.sigmoid(za[:, c:2 * c])
            a_scr[k, HALO_A:HALO_A + sub, :] = ga
            a_scr[nxt, 0:HALO_A, :] = ga[sub - HALO_A:sub]

        def mixer_b_in():
            zb = proj(4, 5) * proj(2, 3)
            b_scr[k, HALO_B:HALO_B + sub, :] = zb
            b_scr[nxt, 0:HALO_B, :] = zb[sub - HALO_B:sub]

        def mixer_b_gate():
            gb_scr[k] = proj(3, 4)

        return [norm, mixer_a, mixer_b_in, mixer_b_gate]

    def mix_steps(k):
        rows = row_block(k)
        a_buf, b_buf, gb_buf, mix_buf = a_scr.at[k], b_scr.at[k], gb_scr.at[k], mix_scr.at[k]
        off_b = HALO_B - (SC_WIDTH - 1)
        steps = []
        for r0 in range(0, sub, CONV_CHUNK):
            chunk = slice(r0, r0 + CONV_CHUNK)
            for s in range(c // LANES):
                def conv_slab(r0=r0, s=s, chunk=chunk):
                    conv_scr[chunk, s * LANES:(s + 1) * LANES] = _conv_a_slab(a_buf, aw_ref, r0, s)
                steps.append(conv_slab)

            def gate(r0=r0, chunk=chunk):
                ya = _layernorm_silu(conv_scr[chunk, :] + ab_ref[...], lng_ref[...], lnb_ref[...])
                cb = bw_ref[0:1, :] * b_buf[off_b + r0:off_b + r0 + CONV_CHUNK, :]
                for j in range(1, SC_WIDTH):
                    cb = cb + bw_ref[j:j + 1, :] * b_buf[off_b + r0 + j:off_b + r0 + j + CONV_CHUNK, :]
                yb = gb_buf[chunk, :] * cb
                mix_buf[chunk, :] = jnp.concatenate([ya, yb], axis=-1).astype(BF16)
            steps.append(gate)

        def out_proj():
            y = jnp.dot(mix_buf[...], wout_ref[...], preferred_element_type=F32)
            o_ref[rows, :] = x_ref[rows, :] + _rmsnorm(y, g1_ref[...])
        steps.append(out_proj)
        return steps

    def interleave(matmul_steps, vector_steps):
        per = -(-len(vector_steps) // max(len(matmul_steps), 1))
        while matmul_steps or vector_steps:
            if matmul_steps:
                matmul_steps.pop(0)()
            for _ in range(per):
                if vector_steps:
                    vector_steps.pop(0)()

    interleave(project_steps(0), [])
    for k in range(n_sub - 1):
        interleave(project_steps(k + 1), mix_steps(k))
    interleave([], mix_steps(n_sub - 1))

    sta_ref[0] = a_scr[0, 0:HALO_A, :]
    stb_ref[0] = b_scr[0, 0:HALO_B, :]


def _conv_layer(x, g0, w_in, aw, ab, lng, lnb, bw, w_out, g1):
    nt = SEQ // CONV_TILE
    n_sub = CONV_TILE // CONV_SUB
    row = lambda b, t: (b * nt + t, 0)
    seq = lambda b, t: (b, 0, 0)
    return pl.pallas_call(
        _conv_layer_body,
        grid=(BATCH, nt),
        in_specs=[pl.BlockSpec((CONV_TILE, D_MODEL), row), _const_spec((1, D_MODEL)),
                  _const_spec((D_MODEL, 5 * CONV_CH), single=True),
                  _const_spec((CONV_WIDTH, CONV_CH)), _const_spec((1, CONV_CH)),
                  _const_spec((1, CONV_CH)), _const_spec((1, CONV_CH)),
                  _const_spec((SC_WIDTH, CONV_CH)), _const_spec((D_MODEL, D_MODEL), single=True),
                  _const_spec((1, D_MODEL))],
        out_specs=[pl.BlockSpec((CONV_TILE, D_MODEL), row),
                   pl.BlockSpec((1, HALO_A, CONV_CH), seq), pl.BlockSpec((1, HALO_B, CONV_CH), seq)],
        out_shape=[jax.ShapeDtypeStruct(x.shape, F32),
                   jax.ShapeDtypeStruct((BATCH, HALO_A, CONV_CH), F32),
                   jax.ShapeDtypeStruct((BATCH, HALO_B, CONV_CH), F32)],
        scratch_shapes=[pltpu.VMEM((n_sub, halo + CONV_SUB, CONV_CH), F32) for halo in (HALO_A, HALO_B, 0)]
                       + [pltpu.VMEM((n_sub, CONV_SUB, 2 * CONV_CH), BF16),
                          pltpu.VMEM((CONV_SUB, D_MODEL), BF16), pltpu.VMEM((CONV_SUB, CONV_CH), F32)],
        compiler_params=_params("arbitrary", "arbitrary"),
        name="conv_layer",
    )(x, g0, w_in, aw, ab, lng, lnb, bw, w_out, g1)


def _conv_mix_dec_body(ga_ref, zb_ref, gb_ref, h_ref, sta_ref, sb0_ref, sb1_ref, aw_ref, ab_ref,
                       lng_ref, lnb_ref, bw_ref, w_ref, g_ref, o_ref, nsta_ref):
    ga = ga_ref[...]
    n_state = CONV_WIDTH - 1
    acc = aw_ref[n_state:n_state + 1, :] * ga
    for j in range(n_state):
        acc = acc + aw_ref[j:j + 1, :] * sta_ref[j]
    for j in range(n_state - 1):
        nsta_ref[j] = sta_ref[j + 1]
    nsta_ref[n_state - 1] = ga
    ya = _layernorm_silu(acc + ab_ref[...], lng_ref[...], lnb_ref[...])
    cb = bw_ref[0:1, :] * sb0_ref[...] + bw_ref[1:2, :] * sb1_ref[...] + bw_ref[2:3, :] * zb_ref[...]
    yb = gb_ref[...] * cb
    mix = jnp.concatenate([ya, yb], axis=-1).astype(BF16)
    y = jnp.dot(mix, w_ref[...], preferred_element_type=F32)
    o_ref[...] = h_ref[...] + _rmsnorm(y, g_ref[...])


def _conv_mix_dec(ga, zb, gb, h, sta, sb0, sb1, aw, ab, lng, lnb, bw, w, g):
    args = (ga, zb, gb, h, sta, sb0, sb1, aw, ab, lng, lnb, bw, w, g)
    return pl.pallas_call(
        _conv_mix_dec_body,
        grid=(1,),
        in_specs=[_const_spec(a.shape) for a in args],
        out_specs=[_const_spec(h.shape), _const_spec(sta.shape)],
        out_shape=[jax.ShapeDtypeStruct(h.shape, F32), jax.ShapeDtypeStruct(sta.shape, F32)],
        compiler_params=_params("arbitrary"),
        name="conv_mix_dec",
    )(*args)


FF_CHUNK = 1024
MLP_TILE = 1024
MIX_MLP_TILE = 512


def _shift_in_lanes(x, new_cols, first_seq, n_seq):
    lane = lax.broadcasted_iota(jnp.int32, (1, LANES), 1)
    new = jnp.concatenate([pltpu.roll(new_cols, LANES - 1 - first_seq - b, 1) for b in range(n_seq)], axis=0)
    n_t = x.shape[1] // LANES
    rolled = [pltpu.roll(x[:, t * LANES:(t + 1) * LANES], LANES - 1, 1) for t in range(n_t)]
    tiles = [jnp.where(lane == LANES - 1, rolled[t + 1] if t + 1 < n_t else new, rolled[t])
             for t in range(n_t)]
    return tiles[0] if n_t == 1 else jnp.concatenate(tiles, axis=1)


def _mlp_block(x, g2_ref, w1_ref, w2_ref, g3_ref):
    u = _rmsnorm(x, g2_ref[...]).astype(BF16)
    acc = jnp.zeros(x.shape, F32)
    for c in range(D_FF // FF_CHUNK):
        sl = slice(c * FF_CHUNK, (c + 1) * FF_CHUNK)
        hid = jnp.dot(u, w1_ref[:, sl], preferred_element_type=F32)
        hid = jnp.square(jnp.maximum(hid, 0.0)).astype(BF16)
        acc = acc + jnp.dot(hid, w2_ref[sl, :], preferred_element_type=F32)
    return x + _rmsnorm(acc, g3_ref[...])


def _mlp_body(xp_ref, xs_ref, g2_ref, w1_ref, w2_ref, g3_ref, *rest):
    n_cast = (len(rest) - 2) // 2
    cast_in, (op_ref, os_ref), cast_out = rest[:n_cast], rest[n_cast:n_cast + 2], rest[n_cast + 2:]
    i, n = pl.program_id(0), pl.num_programs(0) - 1

    @pl.when(i < n)
    def _():
        for src, dst in zip(cast_in, cast_out):
            dst[...] = src[...].astype(BF16)
        op_ref[...] = _mlp_block(xp_ref[...], g2_ref, w1_ref, w2_ref, g3_ref)

    @pl.when(i == n)
    def _():
        os_ref[...] = _mlp_block(xs_ref[...], g2_ref, w1_ref, w2_ref, g3_ref)


def _mix_mlp_body(mp_ref, ms_ref, hp_ref, hs_ref, wo_ref, g1_ref, g2_ref, w1_ref, w2_ref, g3_ref,
                  cache_ref, kvt_ref, op_ref, os_ref, ncache_ref, *, cache_rows):
    i, n = pl.program_id(0), pl.num_programs(0) - 1
    n_seq = cache_ref.shape[0]

    def block(mix_ref, h_ref):
        y = jnp.dot(mix_ref[...].astype(BF16), wo_ref[...], preferred_element_type=F32)
        x = h_ref[...] + _rmsnorm(y, g1_ref[...])
        return _mlp_block(x, g2_ref, w1_ref, w2_ref, g3_ref)

    @pl.when(i < n)
    def _():
        for part in range(2):
            x = cache_ref[:, part]
            r0 = cache_rows + part * HEAD_DIM
            new = _shift_in_lanes(x.reshape(n_seq * HEAD_DIM, x.shape[2]), kvt_ref[r0:r0 + HEAD_DIM, :],
                                  i * n_seq, n_seq)
            ncache_ref[:, part] = new.reshape(x.shape)
        op_ref[...] = block(mp_ref, hp_ref)

    @pl.when(i == n)
    def _():
        os_ref[...] = block(ms_ref, hs_ref)


def _two_group_specs(xp, xs, tm):
    n = xp.shape[0] // tm
    prompt = pl.BlockSpec((tm, xp.shape[1]), lambda i: (jnp.minimum(i, n - 1), 0))
    return n, prompt, _const_spec(xs.shape)


def _mix_mlp(mix_p, mix_s, hp, hs, wo, g1, g2, w1, w2, g3, cache, kvt, cache_rows, tm):
    n, mp_spec, ms_spec = _two_group_specs(mix_p, mix_s, tm)
    _, hp_spec, hs_spec = _two_group_specs(hp, hs, tm)
    n_seq = cache.shape[0] // n
    assert n_seq * n == cache.shape[0]
    c_spec = pl.BlockSpec((n_seq,) + cache.shape[1:], lambda i: (jnp.minimum(i, n - 1), 0, 0, 0))
    vec = _const_spec((1, D_MODEL))
    return pl.pallas_call(
        functools.partial(_mix_mlp_body, cache_rows=cache_rows),
        grid=(n + 1,),
        in_specs=[mp_spec, ms_spec, hp_spec, hs_spec, _const_spec(wo.shape, single=True), vec, vec,
                  _const_spec((D_MODEL, D_FF), single=True), _const_spec((D_FF, D_MODEL), single=True), vec,
                  c_spec, _const_spec(kvt.shape)],
        out_specs=[hp_spec, hs_spec, c_spec],
        out_shape=[jax.ShapeDtypeStruct(hp.shape, F32), jax.ShapeDtypeStruct(hs.shape, F32),
                   jax.ShapeDtypeStruct(cache.shape, F32)],
        compiler_params=_params("arbitrary"),
        name="mix_mlp",
    )(mix_p, mix_s, hp, hs, wo, g1, g2, w1, w2, g3, cache, kvt)


def _mlp(xp, xs, g2, w1, w2, g3, tm, to_bf16=()):
    n, p_spec, s_spec = _two_group_specs(xp, xs, tm)
    vec = _const_spec((1, D_MODEL))
    cast_in, cast_out, cast_shapes = [], [], []
    for w, idx in to_bf16:
        rows, cols = w.shape[1] // n, w.shape[2]
        assert rows * n == w.shape[1] and rows % 16 == 0
        step_rows = lambda i: jnp.minimum(i, n - 1)
        cast_in.append(pl.BlockSpec((None, rows, cols), lambda i, idx=idx: (idx, step_rows(i), 0)))
        cast_out.append(pl.BlockSpec((rows, cols), lambda i: (step_rows(i), 0)))
        cast_shapes.append(jax.ShapeDtypeStruct(w.shape[1:], BF16))
    return pl.pallas_call(
        _mlp_body,
        grid=(n + 1,),
        in_specs=[p_spec, s_spec, vec,
                  _const_spec((D_MODEL, D_FF), single=True), _const_spec((D_FF, D_MODEL), single=True), vec]
                 + cast_in,
        out_specs=[p_spec, s_spec] + cast_out,
        out_shape=[jax.ShapeDtypeStruct(xp.shape, F32), jax.ShapeDtypeStruct(xs.shape, F32)] + cast_shapes,
        compiler_params=_params("arbitrary"),
        name="mlp",
    )(xp, xs, g2, w1, w2, g3, *[w for w, _ in to_bf16])


def _rope_slab(z, c, s_lo, s_hi):
    return z * c + pltpu.roll(z, LANES - ROT_HALF, 1) * s_lo + pltpu.roll(z, ROT_HALF, 1) * s_hi


def _attn_in_body(x_ref, g_ref, w_ref, ca_ref, sla_ref, sha_ref, cb_ref, slb_ref, shb_ref,
                  qn_ref, qd_ref, kv_ref, *, scale):
    u = _rmsnorm(x_ref[...], g_ref[...]).astype(BF16)
    z = jnp.dot(u, w_ref[...], preferred_element_type=F32)
    ca, sla, sha = ca_ref[...], sla_ref[...], sha_ref[...]
    for s in range(Q_COLS // LANES):
        sl = slice(s * LANES, (s + 1) * LANES)
        q = _rope_slab(z[:, sl], ca, sla, sha) * scale
        if s < QN_SLABS:
            qn_ref[:, sl] = q.astype(BF16)
        else:
            qd_ref[s - QN_SLABS] = q
    kv_ref[0] = _rope_slab(z[:, Q_COLS:Q_COLS + LANES], ca, sla, sha)
    kv_ref[1] = z[:, Q_COLS + LANES:Q_COLS + 2 * LANES]
    cb, slb, shb = cb_ref[...], slb_ref[...], shb_ref[...]
    for s in range(2, KV_SLABS):
        sl = slice(Q_COLS + s * LANES, Q_COLS + (s + 1) * LANES)
        kv_ref[s] = _rope_slab(z[:, sl], cb, slb, shb)


QN_SLABS = (SWA_HEADS + DIL_HEADS) * HEAD_DIM // LANES
QD_SLABS = Q_COLS // LANES - QN_SLABS
KV_SLABS = KV_COLS // LANES


def _attn_in(x, g, w, tabs, tm, scale):
    t = x.shape[0]
    row = lambda i: (i, 0)
    slab = lambda i: (0, i, 0)
    nper = tabs[0].shape[0] // tm
    tab = pl.BlockSpec((tm, LANES), lambda i: (i % nper, 0))
    return pl.pallas_call(
        functools.partial(_attn_in_body, scale=scale),
        grid=(t // tm,),
        in_specs=[pl.BlockSpec((tm, D_MODEL), row), _const_spec((1, D_MODEL)),
                  _const_spec((D_MODEL, ATTN_COLS), single=True)] + [tab] * 6,
        out_specs=[pl.BlockSpec((tm, QN_SLABS * LANES), row), pl.BlockSpec((QD_SLABS, tm, LANES), slab),
                   pl.BlockSpec((KV_SLABS, tm, LANES), slab)],
        out_shape=[jax.ShapeDtypeStruct((t, QN_SLABS * LANES), BF16),
                   jax.ShapeDtypeStruct((QD_SLABS, t, LANES), F32),
                   jax.ShapeDtypeStruct((KV_SLABS, t, LANES), F32)],
        compiler_params=_params("arbitrary"),
        name="attn_in",
    )(x, g, w, *tabs)


LOG2E = math.log2(math.e)
ATTN_STEP = 8 * BLK
ATTN_LOOKAHEAD = 2
_NT = (((1,), (1,)), ((), ()))
_TN = (((0,), (0,)), ((), ()))


def _both_halves(x, in_hi):
    lane = lax.broadcasted_iota(jnp.int32, x.shape, 1)
    return jnp.where(lane >= HEAD_DIM if in_hi else lane < HEAD_DIM, x, pltpu.roll(x, HEAD_DIM, 1))


def _block_scores(q_slabs, k_tiles, k_hi, bias):
    in_a = lax.broadcasted_iota(jnp.int32, (BLK, LANES), 1) < HEAD_DIM
    kk = jnp.concatenate([_both_halves(t, k_hi) for t in k_tiles], axis=0).astype(BF16)
    zero = jnp.zeros((BLK, LANES), BF16)
    scores = []
    for qs in q_slabs:
        qq = jnp.concatenate([jnp.where(in_a, qs, zero), jnp.where(in_a, zero, qs)], axis=0)
        scores.append(lax.dot_general(kk, qq, _NT, preferred_element_type=F32) + bias)
    return scores


def _block_outputs(scores, v_tiles, v_hi, sinks):
    lane = lax.broadcasted_iota(jnp.int32, (BLK, LANES), 1)
    one_lane = 0 if v_hi else HEAD_DIM
    vv = jnp.concatenate([jnp.where(lane == one_lane, 1.0, t) for t in v_tiles], axis=0).astype(BF16)
    probs, maxes = [], []
    for slab, sh in enumerate(scores):
        m = jnp.max(sh, axis=0, keepdims=True)
        if sinks is not None:
            m = jnp.maximum(m, sinks[slab])
        probs.append(jnp.exp2(sh - m).astype(BF16))
        maxes.append(m)
    v0 = HEAD_DIM if v_hi else 0
    outs = []
    for slab, p in enumerate(probs):
        ot = lax.dot_general(vv, p, _TN, preferred_element_type=F32)
        m = maxes[slab]
        den = ot[one_lane:one_lane + 1, :]
        if sinks is not None:
            den = den + jnp.exp2(sinks[slab] - m)
        o_t = jnp.concatenate([ot[v0:v0 + HEAD_DIM, 0:BLK], ot[v0:v0 + HEAD_DIM, BLK:2 * BLK]], axis=0)
        tile = lambda row: jnp.concatenate([jnp.broadcast_to(row[:, 0:BLK], (HEAD_DIM, BLK)),
                                            jnp.broadcast_to(row[:, BLK:2 * BLK], (HEAD_DIM, BLK))], axis=0)
        outs.append(((o_t / tile(den)).T, tile(m + jnp.log2(den)).T))
    return outs


N_RES = 4


def _attn_seq_body(qn_ref, qd_ref, kv_ref, sink_ref, mix_ref, qc1, kc1, qc2, kc2, res):
    key2 = lax.broadcasted_iota(jnp.int32, (2 * BLK, 2 * BLK), 0)
    qry2 = lax.broadcasted_iota(jnp.int32, (2 * BLK, 2 * BLK), 1) & (BLK - 1)
    slack = jnp.where(key2 < BLK, key2 - qry2, qry2 - key2 + BLK)
    in_prev = jnp.where(key2 < BLK, 4 * BLK, 0)
    neg_inf = jnp.float32(-jnp.inf)
    bias_both = jnp.where(slack >= 0, 0.0, neg_inf)
    bias_cur = bias_both[BLK:2 * BLK]

    for gi, (qc, kc) in ((1, (qc1, kc1)), (2, (qc2, kc2))):
        dil = DIL_PATTERNS[gi][1]
        n = SEQ // dil
        for r in range(dil):
            dst = slice(r * n, (r + 1) * n)
            for s in range(2):
                qc[dst, s * LANES:(s + 1) * LANES] = qd_ref[2 * (gi - 1) + s, pl.ds(r, n, stride=dil), :].astype(BF16)
            kc[dst, :] = kv_ref[2 + gi, pl.ds(r, n, stride=dil), :]

    def run(q_src, q_col0, n_slabs, k_src, v_src, n_kv, packed, chain, sinks, write, class_blocks=None):
        step_rows = ATTN_STEP
        slabs_per_kv = n_slabs // n_kv

        def step(it, carry):
            base = pl.multiple_of(it * step_rows, step_rows)
            items = [(j, kvh) for j in range(step_rows // BLK) for kvh in range(n_kv)]

            def key_rows(j):
                r0 = pl.multiple_of(base + j * BLK, BLK)
                if chain == "block" or (chain == "class" and j % class_blocks == 0):
                    return r0, [pl.ds(r0, BLK)], bias_cur
                if chain == "seq" and j == 0:
                    prev = pl.ds(pl.multiple_of(jnp.maximum(r0 - BLK, 0), BLK), BLK)
                    gone = in_prev * jnp.where(it > 0, 0, 1)
                    return r0, [prev, pl.ds(r0, BLK)], jnp.where(slack - gone >= 0, 0.0, neg_inf)
                return r0, [pl.ds(pl.multiple_of(r0 - BLK, BLK), BLK), pl.ds(r0, BLK)], bias_both

            def scores_of(item):
                j, kvh = item
                r0, krows, bias = key_rows(j)
                slabs = range(kvh * slabs_per_kv, (kvh + 1) * slabs_per_kv)
                q_slabs = [q_src[pl.ds(r0, BLK), q_col0 + s * LANES:q_col0 + (s + 1) * LANES] for s in slabs]
                k_hi = (not packed) and kvh == 1
                return _block_scores(q_slabs, [k_src[r, :] for r in krows], k_hi, bias)

            def finish(item, scores):
                j, kvh = item
                r0, krows, _ = key_rows(j)
                slabs = range(kvh * slabs_per_kv, (kvh + 1) * slabs_per_kv)
                sk = None if sinks is None else [sinks[s] for s in slabs]
                v_hi = packed or kvh == 1
                outs = _block_outputs(scores, [v_src[r, :] for r in krows], v_hi, sk)
                for s, (o, lse) in zip(slabs, outs):
                    write(it, j, r0, s, o, lse)

            ahead = min(ATTN_LOOKAHEAD, len(items))
            pending = [scores_of(item) for item in items[:ahead]]
            for n, item in enumerate(items):
                if n + ahead < len(items):
                    pending.append(scores_of(items[n + ahead]))
                finish(item, pending.pop(0))
            return carry

        lax.fori_loop(0, SEQ // step_rows, step, 0)

    sinks = [jnp.concatenate([jnp.broadcast_to(sink_ref[0:1, 2 * s + half:2 * s + half + 1] * LOG2E, (1, BLK))
                              for half in range(2)], axis=1) for s in range(SWA_HEADS // 2)]

    def write_swa(it, j, r0, s, o, lse):
        mix_ref[pl.ds(r0, BLK), s * LANES:(s + 1) * LANES] = o.astype(BF16)

    run(qn_ref, 0, SWA_HEADS // 2, kv_ref.at[0], kv_ref.at[1], SWA_KV, False, "seq", sinks, write_swa)

    def write_res(group, start_of, stride):
        def write(it, j, r0, s, o, lse):
            dst = pl.ds(start_of(it, j, r0), BLK) if stride == 1 else pl.ds(start_of(it, j, r0), BLK, stride=stride)
            res[group * N_RES + s, dst, :] = o
            res[group * N_RES + 2 + s, dst, :] = lse
        return write

    n_dil_slabs = DIL_HEADS // 2
    run(qn_ref, SWA_HEADS * HEAD_DIM, n_dil_slabs, kv_ref.at[2], kv_ref.at[2], 1, True, "seq", None,
        write_res(0, lambda it, j, r0: r0, 1))
    d1, d2 = DIL_PATTERNS[1][1], DIL_PATTERNS[2][1]
    cb1 = SEQ // d1 // BLK
    per_step = ATTN_STEP // BLK // cb1
    run(qc1, 0, n_dil_slabs, kc1, kc1, 1, True, "class", None,
        write_res(1, lambda it, j, r0: d1 * (j % cb1) * BLK + it * per_step + j // cb1, d1), class_blocks=cb1)
    run(qc2, 0, n_dil_slabs, kc2, kc2, 1, True, "block", None,
        write_res(2, lambda it, j, r0: it * (ATTN_STEP // BLK) + j, d2))

    def merge(c, carry):
        rows = pl.ds(pl.multiple_of(c * ATTN_STEP, ATTN_STEP), ATTN_STEP)
        for s in range(n_dil_slabs):
            lse = [res[g * N_RES + 2 + s, rows, :] for g in range(3)]
            m = jnp.maximum(jnp.maximum(lse[0], lse[1]), lse[2])
            e = [jnp.exp2(l - m) for l in lse]
            o = e[0] * res[s, rows, :] + e[1] * res[N_RES + s, rows, :] + e[2] * res[2 * N_RES + s, rows, :]
            col = SWA_HEADS * HEAD_DIM + s * LANES
            mix_ref[rows, col:col + LANES] = (o / (e[0] + e[1] + e[2])).astype(BF16)
        return carry

    lax.fori_loop(0, SEQ // ATTN_STEP, merge, 0)


def _attn_seq(qn, qd, kv, sinks):
    seq2 = lambda b: (b, 0)
    seq3 = lambda b: (0, b, 0)
    wd = DIL_HEADS * HEAD_DIM
    return pl.pallas_call(
        _attn_seq_body,
        grid=(BATCH,),
        in_specs=[pl.BlockSpec((SEQ, QN_SLABS * LANES), seq2), pl.BlockSpec((QD_SLABS, SEQ, LANES), seq3),
                  pl.BlockSpec((KV_SLABS, SEQ, LANES), seq3), _const_spec(sinks.shape)],
        out_specs=pl.BlockSpec((SEQ, MIX_ATTN), seq2),
        out_shape=jax.ShapeDtypeStruct((BATCH * SEQ, MIX_ATTN), BF16),
        scratch_shapes=[pltpu.VMEM((SEQ, wd), BF16), pltpu.VMEM((SEQ, LANES), F32),
                        pltpu.VMEM((SEQ, wd), BF16), pltpu.VMEM((SEQ, LANES), F32),
                        pltpu.VMEM((3 * N_RES, SEQ, LANES), F32)],
        compiler_params=_params("arbitrary"),
        name="attn_seq",
    )(qn, qd, kv, sinks)


DEC_TILE = 4
N_MIX_HEADS = SWA_HEADS + DIL_HEADS


def _attn_dec_body(q_ref, kv_ref, kv_all_ref, sink_ref, csw_ref, cd0_ref, cd1_ref, cd2_ref,
                   nsw_ref, nd0_ref, nd1_ref, o_ref, kvt_ref):
    step = pl.program_id(0)
    bb, hd, grp = DEC_TILE, HEAD_DIM, DIL_HEADS
    n_rows = grp * bb

    @pl.when(step == 0)
    def _():
        kvt_ref[...] = kv_all_ref[...].T

    q = q_ref[0]
    kv = kv_ref[0]
    row_seq = lax.broadcasted_iota(jnp.int32, (n_rows, bb * hd), 0) & (bb - 1)
    col_seq = lax.broadcasted_iota(jnp.int32, (n_rows, bb * hd), 1) // hd
    diag = row_seq == col_seq
    lane = lax.broadcasted_iota(jnp.int32, (1, LANES), 1)
    nt = (((1,), (1,)), ((), ()))

    def q_rows(c0):
        return jnp.concatenate([q[:, c0 + s * hd:c0 + (s + 1) * hd] for s in range(grp)], axis=0)

    def per_row(x):
        return jnp.concatenate([x] * grp, axis=0)

    def block_diag(qr):
        return jnp.where(diag, jnp.concatenate([qr] * bb, axis=1), 0.0).astype(BF16)

    def take_diag(ob):
        ob = jnp.where(diag, ob, 0.0)
        out = ob[:, 0:hd]
        for b in range(1, bb):
            out = out + ob[:, b * hd:(b + 1) * hd]
        return out

    def shift_in(x, r0):
        return _shift_in_lanes(x, kvt_ref[r0:r0 + hd, :], step * bb, bb)

    def stack(ref, idx):
        x = ref[:, idx]
        return x.reshape(bb * hd, x.shape[2])

    def unstack(x):
        return x.reshape(bb, hd, x.shape[1])

    cd_refs = (cd0_ref, cd1_ref, cd2_ref)
    nd_refs = (nd0_ref, nd1_ref, None)
    n_dil = len(DIL_PATTERNS)
    jobs = [dict(src=csw_ref, dst=nsw_ref, ki=kvh, vi=SWA_KV + kvh, q0=kvh * grp * hd,
                 kc=kvh * hd, vc=LANES + kvh * hd, dil=1) for kvh in range(SWA_KV)]
    jobs += [dict(src=cd_refs[gi], dst=nd_refs[gi], ki=0, vi=1, q0=SWA_HEADS * hd + gi * grp * hd,
                  kc=(2 + gi) * LANES, vc=(2 + gi) * LANES + hd, dil=DIL_PATTERNS[gi][1]) for gi in range(n_dil)]

    for jb in jobs:
        k = stack(jb["src"], jb["ki"])
        qr = q_rows(jb["q0"])
        s = jnp.dot(block_diag(qr), k.astype(BF16), preferred_element_type=F32)
        if jb["dil"] > 1:
            pos = lax.broadcasted_iota(jnp.int32, (1, k.shape[1]), 1)
            s = jnp.where((pos & (jb["dil"] - 1)) == 0, s, -jnp.inf)
        jb["s"] = s
        jb["s_new"] = jnp.sum(qr * per_row(kv[:, jb["kc"]:jb["kc"] + hd]), axis=1, keepdims=True)
        jb["v_new"] = per_row(kv[:, jb["vc"]:jb["vc"] + hd])
        jb["m"] = jnp.maximum(jnp.max(s, axis=1, keepdims=True), jb["s_new"])

    m_dil = jobs[SWA_KV]["m"]
    for jb in jobs[SWA_KV + 1:]:
        m_dil = jnp.maximum(m_dil, jb["m"])
    for kvh, jb in enumerate(jobs):
        if kvh < SWA_KV:
            sk = jnp.concatenate([jnp.broadcast_to(sink_ref[0:1, kvh * grp + s:kvh * grp + s + 1], (bb, 1))
                                  for s in range(grp)], axis=0)
            m = jnp.maximum(jb["m"], sk)
            jb["extra"] = jnp.exp(sk - m)
        else:
            m = m_dil
            jb["extra"] = 0.0
        p = jnp.exp(jb["s"] - m)
        jb["p_new"] = jnp.exp(jb["s_new"] - m)
        jb["den"] = jnp.sum(p, axis=1, keepdims=True) + jb["p_new"] + jb["extra"]
        jb["p"] = p.astype(BF16)

    for jb in jobs:
        v = stack(jb["src"], jb["vi"])
        pv = lax.dot_general(jb["p"], v.astype(BF16), nt, preferred_element_type=F32)
        jb["acc"] = take_diag(pv) + jb["p_new"] * jb["v_new"]

    for kvh in range(SWA_KV):
        o = jobs[kvh]["acc"] / jobs[kvh]["den"]
        for s_ in range(grp):
            o_ref[0, kvh * grp + s_] = o[s_ * bb:(s_ + 1) * bb]
    dil_jobs = jobs[SWA_KV:]
    o = sum(jb["acc"] for jb in dil_jobs[1:]) + dil_jobs[0]["acc"]
    o = o / (sum(jb["den"] for jb in dil_jobs[1:]) + dil_jobs[0]["den"])
    for s_ in range(grp):
        o_ref[0, SWA_HEADS + s_] = o[s_ * bb:(s_ + 1) * bb]

    for jb in jobs:
        if jb["dst"] is not None:
            jb["dst"][:, jb["ki"]] = unstack(shift_in(stack(jb["src"], jb["ki"]), jb["kc"]))
            jb["dst"][:, jb["vi"]] = unstack(shift_in(stack(jb["src"], jb["vi"]), jb["vc"]))


def _attn_dec(q, kv, sinks, csw, cd0, cd1, cd2):
    caches = (csw, cd0, cd1, cd2)
    n_tiles = DEC_BATCH // DEC_TILE
    q3 = q.reshape(n_tiles, DEC_TILE, Q_COLS)
    kv3 = kv.reshape(n_tiles, DEC_TILE, KV_COLS)
    tile3 = lambda w: pl.BlockSpec((1, DEC_TILE, w), lambda i: (i, 0, 0))
    cspec = lambda c: pl.BlockSpec((DEC_TILE,) + c.shape[1:], lambda i: (i, 0, 0, 0))
    o_shape = (n_tiles, N_MIX_HEADS, DEC_TILE, HEAD_DIM)
    shifted = caches[:3]
    outs = pl.pallas_call(
        _attn_dec_body,
        grid=(n_tiles,),
        in_specs=[tile3(Q_COLS), tile3(KV_COLS), _const_spec(kv.shape), _const_spec(sinks.shape)]
                 + [cspec(c) for c in caches],
        out_specs=[cspec(c) for c in shifted]
                  + [pl.BlockSpec((1,) + o_shape[1:], lambda i: (i, 0, 0, 0)),
                     _const_spec((KV_COLS, DEC_BATCH))],
        out_shape=[jax.ShapeDtypeStruct(c.shape, F32) for c in shifted]
                  + [jax.ShapeDtypeStruct(o_shape, F32), jax.ShapeDtypeStruct((KV_COLS, DEC_BATCH), F32)],
        compiler_params=_params("arbitrary"),
        name="attn_dec",
    )(q3, kv3, kv, sinks, *caches)
    mix = jnp.transpose(outs[3], (0, 2, 1, 3)).reshape(DEC_BATCH, MIX_ATTN)
    return outs[0], outs[1], outs[2], mix, outs[4]


def _attn_weight_order(w):
    hd = HEAD_DIM
    nq, nkv = SWA_HEADS * hd, SWA_KV * hd
    base = nq + 2 * nkv
    per = (DIL_HEADS + 2) * hd
    qd = [w[:, base + g * per:base + g * per + DIL_HEADS * hd] for g in range(3)]
    kvd = [w[:, base + g * per + DIL_HEADS * hd:base + (g + 1) * per] for g in range(3)]
    return jnp.concatenate([w[:, :nq]] + qd + [w[:, nq:base]] + kvd, axis=1)


def _rope_tables(pos):
    lane = np.arange(LANES)
    dim = lane % HEAD_DIM
    lo = (dim < ROT_HALF).astype(np.float32)
    hi = ((dim >= ROT_HALF) & (dim < 2 * ROT_HALF)).astype(np.float32)
    first = (lane < HEAD_DIM).astype(np.float32)
    inv = ROPE_THETA ** (-jnp.arange(ROT_HALF, dtype=F32) / ROT_HALF)
    ang = pos.astype(F32)[:, None] * jnp.tile(inv, LANES // ROT_HALF)[None, :]
    cos, sin = jnp.cos(ang), jnp.sin(ang)
    tables = []
    for heads in (np.ones(LANES, np.float32), first):
        rot = (lo + hi) * heads
        tables += [cos * rot + (1.0 - rot), -sin * (lo * heads), sin * (hi * heads)]
    return tuple(tables)


def _cache_view(c):
    b, l, two, kv, hd = c.shape
    return jnp.transpose(c, (0, 2, 3, 4, 1)).reshape(b, two * kv, hd, l)


def _cache_unview(c, kv):
    b, _, hd, l = c.shape
    return jnp.transpose(c.reshape(b, 2, kv, hd, l), (0, 4, 1, 2, 3))


def kernel(x_prompt, x_sample, state_conv_a, state_conv_b, cache_swa_kv, cache_dil0_kv, cache_dil1_kv,
           cache_dil2_kv, norm_g, w_in_conv, conv_a_w, conv_a_b, conv_a_ln_g, conv_a_ln_b, conv_b_w,
           w_out_conv, w_in_attn, attn_sinks, w_out_attn, mlp_w1, mlp_w2):
    tm = 512
    hp = x_prompt.reshape(BATCH * SEQ, D_MODEL)
    hs = x_sample.reshape(DEC_BATCH, D_MODEL)
    g = lambda layer, i: norm_g[layer, i].reshape(1, D_MODEL)

    w_in0 = w_in_conv[0].astype(BF16)
    w_out0 = w_out_conv[0].astype(BF16)
    conv_small = (conv_a_w[0], conv_a_b, conv_a_ln_g, conv_a_ln_b, conv_b_w[0])
    hp, sta_p, stb_p = _conv_layer(hp, g(0, 0), w_in0, *conv_small, w_out0, g(0, 1))
    ga_s, zb_s, gb_s = _conv_in(hs, g(0, 0), w_in0, DEC_BATCH)
    sta = jnp.transpose(state_conv_a[0], (1, 0, 2))
    sb0, sb1 = state_conv_b[0, :, 0], state_conv_b[0, :, 1]
    hs, new_sta = _conv_mix_dec(ga_s, zb_s, gb_s, hs, sta, sb0, sb1, *conv_small, w_out0, g(0, 1))

    later = ((w_in_attn, 0), (w_out_attn, 0), (mlp_w1, 1), (mlp_w2, 1))
    hp, hs, w_in1, w_out1, w1_1, w2_1 = _mlp(hp, hs, g(0, 2), mlp_w1[0].astype(BF16), mlp_w2[0].astype(BF16),
                                             g(0, 3), MLP_TILE, to_bf16=later)

    w_in1 = _attn_weight_order(w_in1)
    softmax_scale = HEAD_DIM ** -0.5
    qn_p, qd_p, kv_p = _attn_in(hp, g(1, 0), w_in1, _rope_tables(jnp.arange(SEQ)), tm,
                                softmax_scale * LOG2E)
    mix_p = _attn_seq(qn_p, qd_p, kv_p, attn_sinks)

    tabs_s = _rope_tables(jnp.full((DEC_BATCH,), PAST_LEN, jnp.int32))
    qn_s, qd_s, kv_s = _attn_in(hs, g(1, 0), w_in1, tabs_s, DEC_BATCH, softmax_scale)
    q_s = jnp.concatenate([qn_s.astype(F32)] + [qd_s[s] for s in range(QD_SLABS)], axis=1)
    kv_s = jnp.concatenate([kv_s[s] for s in range(KV_SLABS)], axis=1)
    caches = (cache_swa_kv[0], cache_dil0_kv[0], cache_dil1_kv[0], cache_dil2_kv[0])
    views = [_cache_view(c) for c in caches]
    nsw, nd0, nd1, mix_s, kvt_s = _attn_dec(q_s, kv_s, attn_sinks, *views)

    hp, hs, nd2 = _mix_mlp(mix_p, mix_s, hp, hs, w_out1, g(1, 1), g(1, 2), w1_1, w2_1, g(1, 3),
                           views[3], kvt_s, (KV_SLABS - 1) * LANES, MIX_MLP_TILE)

    n_a, n_b = CONV_WIDTH - 1, SC_WIDTH - 1
    kv4 = kv_p.reshape(KV_SLABS, BATCH, SEQ, LANES)
    swa_p = jnp.stack([kv4[0, :, SEQ - BLK:], kv4[1, :, SEQ - BLK:]], axis=2)
    swa_p = swa_p.reshape(BATCH, BLK, 2, SWA_KV, HEAD_DIM)
    dil_p = [kv4[2 + gi, :, SEQ - min(w, SEQ):].reshape(BATCH, min(w, SEQ), 2, 1, HEAD_DIM)
             for gi, (w, _) in enumerate(DIL_PATTERNS)]
    return (hp.reshape(BATCH, SEQ, D_MODEL), hs.reshape(DEC_BATCH, 1, D_MODEL),
            sta_p[None, :, HALO_A - n_a:],
            jnp.transpose(new_sta, (1, 0, 2))[None],
            stb_p[None, :, HALO_B - n_b:],
            jnp.stack([sb1, zb_s], axis=1)[None],
            swa_p[None], _cache_unview(nsw, SWA_KV)[None],
            dil_p[0][None], _cache_unview(nd0, 1)[None],
            dil_p[1][None], _cache_unview(nd1, 1)[None],
            dil_p[2][None], _cache_unview(nd2, 1)[None])
```

```python
import functools
import math

import jax
import jax.numpy as jnp
import numpy as np
from jax import lax
from jax.experimental import pallas as pl
from jax.experimental.pallas import tpu as pltpu

F32 = jnp.float32
BF16 = jnp.bfloat16

D_MODEL = 1024
BATCH = 8
SEQ = 2048
DEC_BATCH = 128
PAST_LEN = 8192
HEAD_DIM = 64
ROT_HALF = 8
ROPE_THETA = 500000.0
D_FF = 4 * D_MODEL
EPS = 1e-6
CONV_CH = 512
CONV_WIDTH = 31
SC_WIDTH = 3
SWA_HEADS = 8
SWA_KV = 2
DIL_HEADS = 4
DIL_PATTERNS = ((128, 1), (512, 4), (2048, 16))
Q_COLS = (SWA_HEADS + 3 * DIL_HEADS) * HEAD_DIM
KV_COLS = 2 * SWA_KV * HEAD_DIM + 3 * 2 * HEAD_DIM
ATTN_COLS = Q_COLS + KV_COLS
MIX_ATTN = (SWA_HEADS + DIL_HEADS) * HEAD_DIM
LANES = 128
BLK = 128

V7X_VMEM_BYTES = 64 * 1024 * 1024
VMEM_LIMIT = V7X_VMEM_BYTES - 8 * 1024 * 1024


def _params(*sem):
    return pltpu.CompilerParams(dimension_semantics=sem, vmem_limit_bytes=VMEM_LIMIT)


def _const_spec(shape, single=False):
    zeros = (0,) * len(shape)
    if single:
        return pl.BlockSpec(shape, lambda *_: zeros, pipeline_mode=pl.Buffered(1))
    return pl.BlockSpec(shape, lambda *_: zeros)


def _rmsnorm(x, g):
    return x * lax.rsqrt(jnp.mean(x * x, axis=-1, keepdims=True) + EPS) * g


def _conv_in_body(x_ref, g_ref, w_ref, ga_ref, zb_ref, gb_ref):
    u = _rmsnorm(x_ref[...], g_ref[...]).astype(BF16)
    z = jnp.dot(u, w_ref[...], preferred_element_type=F32)
    c = CONV_CH
    ga_ref[...] = z[:, 0:c] * jax.nn.sigmoid(z[:, c:2 * c])
    zb_ref[...] = z[:, 4 * c:5 * c] * z[:, 2 * c:3 * c]
    gb_ref[...] = z[:, 3 * c:4 * c]


def _conv_in(x, g, w, tm):
    t = x.shape[0]
    row = lambda i: (i, 0)
    out = jax.ShapeDtypeStruct((t, CONV_CH), F32)
    return pl.pallas_call(
        _conv_in_body,
        grid=(t // tm,),
        in_specs=[pl.BlockSpec((tm, D_MODEL), row), _const_spec((1, D_MODEL)),
                  _const_spec((D_MODEL, 5 * CONV_CH))],
        out_specs=[pl.BlockSpec((tm, CONV_CH), row)] * 3,
        out_shape=[out] * 3,
        compiler_params=_params("arbitrary"),
        name="conv_in",
    )(x, g, w)


HALO_A = 32
HALO_B = 8
CONV_CHUNK = 128


def _layernorm_silu(c, g, b):
    mu = jnp.mean(c, axis=-1, keepdims=True)
    d = c - mu
    var = jnp.mean(d * d, axis=-1, keepdims=True)
    y = d * lax.rsqrt(var + EPS) * g + b
    return y * jax.nn.sigmoid(y)


SUBLANES = 8
CONV_TILE = 1024
CONV_SUB = 256


def _conv_a_slab(ext_a, aw_ref, r0, s):
    cols = slice(s * LANES, (s + 1) * LANES)
    out = None
    for b in range(SUBLANES):
        rows = CONV_CHUNK if b == 0 else CONV_CHUNK + SUBLANES
        yb = None
        for a in range((CONV_WIDTH + 1) // SUBLANES + 1):
            j = SUBLANES * a + b - (HALO_A - (CONV_WIDTH - 1))
            if 0 <= j < CONV_WIDTH:
                term = aw_ref[j:j + 1, cols] * ext_a[r0 + SUBLANES * a:r0 + SUBLANES * a + rows, cols]
                yb = term if yb is None else yb + term
        yb = yb[b:b + CONV_CHUNK]
        out = yb if out is None else out + yb
    return out


def _conv_layer_body(*refs, n_cast):
    x_ref, g0_ref, win_ref, aw_ref, ab_ref, lng_ref, lnb_ref, bw_ref, wout_ref, g1_ref = refs[:10]
    cast_in = refs[10:10 + n_cast]
    o_ref, sta_ref, stb_ref = refs[10 + n_cast:13 + n_cast]
    cast_out = refs[13 + n_cast:13 + 2 * n_cast]
    a_scr, b_scr, gb_scr, mix_scr, u_scr, conv_scr = refs[13 + 2 * n_cast:]
    for src, dst in zip(cast_in, cast_out):
        dst[...] = src[...].astype(BF16)
    t = pl.program_id(1)
    sub, c = CONV_SUB, CONV_CH
    n_sub = CONV_TILE // sub

    @pl.when(t == 0)
    def _():
        a_scr[0, 0:HALO_A, :] = jnp.zeros((HALO_A, c), F32)
        b_scr[0, 0:HALO_B, :] = jnp.zeros((HALO_B, c), F32)

    def row_block(k):
        return pl.ds(k * sub, sub) if isinstance(k, int) else pl.ds(pl.multiple_of(k * sub, sub), sub)

    def project_steps(k):
        rows = row_block(k)
        nxt = (k + 1) % n_sub if isinstance(k, int) else jnp.where(k + 1 == n_sub, 0, k + 1)
        proj = lambda lo, hi: jnp.dot(u_scr[...], win_ref[:, lo * c:hi * c], preferred_element_type=F32)

        def norm():
            u_scr[...] = _rmsnorm(x_ref[rows, :], g0_ref[...]).astype(BF16)

        def mixer_a():
            za = proj(0, 2)
            ga = za[:, 0:c] * jax.nn.sigmoid(za[:, c:2 * c])
            a_scr[k, HALO_A:HALO_A + sub, :] = ga
            a_scr[nxt, 0:HALO_A, :] = ga[sub - HALO_A:sub]

        def mixer_b_in():
            zb = proj(4, 5) * proj(2, 3)
            b_scr[k, HALO_B:HALO_B + sub, :] = zb
            b_scr[nxt, 0:HALO_B, :] = zb[sub - HALO_B:sub]

        def mixer_b_gate():
            gb_scr[k] = proj(3, 4)

        return [norm, mixer_a, mixer_b_in, mixer_b_gate]

    def mix_steps(k):
        rows = row_block(k)
        a_buf, b_buf, gb_buf, mix_buf = a_scr.at[k], b_scr.at[k], gb_scr.at[k], mix_scr.at[k]
        off_b = HALO_B - (SC_WIDTH - 1)
        steps = []
        for r0 in range(0, sub, CONV_CHUNK):
            chunk = slice(r0, r0 + CONV_CHUNK)
            for s in range(c // LANES):
                def conv_slab(r0=r0, s=s, chunk=chunk):
                    conv_scr[chunk, s * LANES:(s + 1) * LANES] = _conv_a_slab(a_buf, aw_ref, r0, s)
                steps.append(conv_slab)

            def gate(r0=r0, chunk=chunk):
                ya = _layernorm_silu(conv_scr[chunk, :] + ab_ref[...], lng_ref[...], lnb_ref[...])
                cb = bw_ref[0:1, :] * b_buf[off_b + r0:off_b + r0 + CONV_CHUNK, :]
                for j in range(1, SC_WIDTH):
                    cb = cb + bw_ref[j:j + 1, :] * b_buf[off_b + r0 + j:off_b + r0 + j + CONV_CHUNK, :]
                yb = gb_buf[chunk, :] * cb
                mix_buf[chunk, :] = jnp.concatenate([ya, yb], axis=-1).astype(BF16)
            steps.append(gate)

        def out_proj():
            y = jnp.dot(mix_buf[...], wout_ref[...], preferred_element_type=F32)
            o_ref[rows, :] = x_ref[rows, :] + _rmsnorm(y, g1_ref[...])
        steps.append(out_proj)
        return steps

    def interleave(matmul_steps, vector_steps):
        per = -(-len(vector_steps) // max(len(matmul_steps), 1))
        while matmul_steps or vector_steps:
            if matmul_steps:
                matmul_steps.pop(0)()
            for _ in range(per):
                if vector_steps:
                    vector_steps.pop(0)()

    interleave(project_steps(0), [])
    for k in range(n_sub - 1):
        interleave(project_steps(k + 1), mix_steps(k))
    interleave([], mix_steps(n_sub - 1))

    sta_ref[0] = a_scr[0, 0:HALO_A, :]
    stb_ref[0] = b_scr[0, 0:HALO_B, :]


def _conv_layer(x, g0, w_in, aw, ab, lng, lnb, bw, w_out, g1, to_bf16=()):
    nt = SEQ // CONV_TILE
    n_sub = CONV_TILE // CONV_SUB
    row = lambda b, t: (b * nt + t, 0)
    seq = lambda b, t: (b, 0, 0)
    cast_in, cast_out, cast_shapes = _cast_specs(to_bf16, BATCH * nt, lambda b, t: b * nt + t)
    return pl.pallas_call(
        functools.partial(_conv_layer_body, n_cast=len(to_bf16)),
        grid=(BATCH, nt),
        in_specs=[pl.BlockSpec((CONV_TILE, D_MODEL), row), _const_spec((1, D_MODEL)),
                  _const_spec((D_MODEL, 5 * CONV_CH), single=True),
                  _const_spec((CONV_WIDTH, CONV_CH)), _const_spec((1, CONV_CH)),
                  _const_spec((1, CONV_CH)), _const_spec((1, CONV_CH)),
                  _const_spec((SC_WIDTH, CONV_CH)), _const_spec((D_MODEL, D_MODEL), single=True),
                  _const_spec((1, D_MODEL))] + cast_in,
        out_specs=[pl.BlockSpec((CONV_TILE, D_MODEL), row),
                   pl.BlockSpec((1, HALO_A, CONV_CH), seq), pl.BlockSpec((1, HALO_B, CONV_CH), seq)] + cast_out,
        out_shape=[jax.ShapeDtypeStruct(x.shape, F32),
                   jax.ShapeDtypeStruct((BATCH, HALO_A, CONV_CH), F32),
                   jax.ShapeDtypeStruct((BATCH, HALO_B, CONV_CH), F32)] + cast_shapes,
        scratch_shapes=[pltpu.VMEM((n_sub, halo + CONV_SUB, CONV_CH), F32) for halo in (HALO_A, HALO_B, 0)]
                       + [pltpu.VMEM((n_sub, CONV_SUB, 2 * CONV_CH), BF16),
                          pltpu.VMEM((CONV_SUB, D_MODEL), BF16), pltpu.VMEM((CONV_SUB, CONV_CH), F32)],
        compiler_params=_params("arbitrary", "arbitrary"),
        name="conv_layer",
    )(x, g0, w_in, aw, ab, lng, lnb, bw, w_out, g1, *[w for w, _ in to_bf16])


def _conv_mix_dec_body(ga_ref, zb_ref, gb_ref, h_ref, sta_ref, sb0_ref, sb1_ref, aw_ref, ab_ref,
                       lng_ref, lnb_ref, bw_ref, w_ref, g_ref, o_ref, nsta_ref):
    ga = ga_ref[...]
    n_state = CONV_WIDTH - 1
    acc = aw_ref[n_state:n_state + 1, :] * ga
    for j in range(n_state):
        acc = acc + aw_ref[j:j + 1, :] * sta_ref[j]
    for j in range(n_state - 1):
        nsta_ref[j] = sta_ref[j + 1]
    nsta_ref[n_state - 1] = ga
    ya = _layernorm_silu(acc + ab_ref[...], lng_ref[...], lnb_ref[...])
    cb = bw_ref[0:1, :] * sb0_ref[...] + bw_ref[1:2, :] * sb1_ref[...] + bw_ref[2:3, :] * zb_ref[...]
    yb = gb_ref[...] * cb
    mix = jnp.concatenate([ya, yb], axis=-1).astype(BF16)
    y = jnp.dot(mix, w_ref[...], preferred_element_type=F32)
    o_ref[...] = h_ref[...] + _rmsnorm(y, g_ref[...])


def _conv_mix_dec(ga, zb, gb, h, sta, sb0, sb1, aw, ab, lng, lnb, bw, w, g):
    args = (ga, zb, gb, h, sta, sb0, sb1, aw, ab, lng, lnb, bw, w, g)
    return pl.pallas_call(
        _conv_mix_dec_body,
        grid=(1,),
        in_specs=[_const_spec(a.shape) for a in args],
        out_specs=[_const_spec(h.shape), _const_spec(sta.shape)],
        out_shape=[jax.ShapeDtypeStruct(h.shape, F32), jax.ShapeDtypeStruct(sta.shape, F32)],
        compiler_params=_params("arbitrary"),
        name="conv_mix_dec",
    )(*args)


FF_CHUNK = 1024
MLP_TILE = 1024
MIX_MLP_TILE = 512


def _cast_specs(to_bf16, n_steps, step_of):
    cast_in, cast_out, cast_shapes = [], [], []
    for w, idx in to_bf16:
        rows, cols = w.shape[1] // n_steps, w.shape[2]
        assert rows * n_steps == w.shape[1] and rows % 16 == 0
        cast_in.append(pl.BlockSpec((None, rows, cols), lambda *g, idx=idx: (idx, step_of(*g), 0)))
        cast_out.append(pl.BlockSpec((rows, cols), lambda *g: (step_of(*g), 0)))
        cast_shapes.append(jax.ShapeDtypeStruct(w.shape[1:], BF16))
    return cast_in, cast_out, cast_shapes


def _shift_in_lanes(x, new_cols, first_seq, n_seq):
    lane = lax.broadcasted_iota(jnp.int32, (1, LANES), 1)
    new = jnp.concatenate([pltpu.roll(new_cols, LANES - 1 - first_seq - b, 1) for b in range(n_seq)], axis=0)
    n_t = x.shape[1] // LANES
    rolled = [pltpu.roll(x[:, t * LANES:(t + 1) * LANES], LANES - 1, 1) for t in range(n_t)]
    tiles = [jnp.where(lane == LANES - 1, rolled[t + 1] if t + 1 < n_t else new, rolled[t])
             for t in range(n_t)]
    return tiles[0] if n_t == 1 else jnp.concatenate(tiles, axis=1)


def _mlp_block(x, g2_ref, w1_ref, w2_ref, g3_ref):
    u = _rmsnorm(x, g2_ref[...]).astype(BF16)
    acc = jnp.zeros(x.shape, F32)
    for c in range(D_FF // FF_CHUNK):
        sl = slice(c * FF_CHUNK, (c + 1) * FF_CHUNK)
        hid = jnp.dot(u, w1_ref[:, sl], preferred_element_type=F32)
        hid = jnp.square(jnp.maximum(hid, 0.0)).astype(BF16)
        acc = acc + jnp.dot(hid, w2_ref[sl, :], preferred_element_type=F32)
    return x + _rmsnorm(acc, g3_ref[...])


def _mlp_body(xp_ref, xs_ref, g2_ref, w1_ref, w2_ref, g3_ref, *rest):
    n_cast = (len(rest) - 2) // 2
    cast_in, (op_ref, os_ref), cast_out = rest[:n_cast], rest[n_cast:n_cast + 2], rest[n_cast + 2:]
    i, n = pl.program_id(0), pl.num_programs(0) - 1

    @pl.when(i < n)
    def _():
        for src, dst in zip(cast_in, cast_out):
            dst[...] = src[...].astype(BF16)
        op_ref[...] = _mlp_block(xp_ref[...], g2_ref, w1_ref, w2_ref, g3_ref)

    @pl.when(i == n)
    def _():
        os_ref[...] = _mlp_block(xs_ref[...], g2_ref, w1_ref, w2_ref, g3_ref)


def _mix_mlp_body(mp_ref, ms_ref, hp_ref, hs_ref, wo_ref, g1_ref, g2_ref, w1_ref, w2_ref, g3_ref,
                  cache_ref, kvt_ref, op_ref, os_ref, ncache_ref, *, cache_rows):
    i, n = pl.program_id(0), pl.num_programs(0) - 1
    n_seq = cache_ref.shape[0]

    def block(mix_ref, h_ref):
        y = jnp.dot(mix_ref[...].astype(BF16), wo_ref[...], preferred_element_type=F32)
        x = h_ref[...] + _rmsnorm(y, g1_ref[...])
        return _mlp_block(x, g2_ref, w1_ref, w2_ref, g3_ref)

    @pl.when(i < n)
    def _():
        for part in range(2):
            x = cache_ref[:, part]
            r0 = cache_rows + part * HEAD_DIM
            new = _shift_in_lanes(x.reshape(n_seq * HEAD_DIM, x.shape[2]), kvt_ref[r0:r0 + HEAD_DIM, :],
                                  i * n_seq, n_seq)
            ncache_ref[:, part] = new.reshape(x.shape)
        op_ref[...] = block(mp_ref, hp_ref)

    @pl.when(i == n)
    def _():
        os_ref[...] = block(ms_ref, hs_ref)


def _two_group_specs(xp, xs, tm):
    n = xp.shape[0] // tm
    prompt = pl.BlockSpec((tm, xp.shape[1]), lambda i: (jnp.minimum(i, n - 1), 0))
    return n, prompt, _const_spec(xs.shape)


def _mix_mlp(mix_p, mix_s, hp, hs, wo, g1, g2, w1, w2, g3, cache, kvt, cache_rows, tm):
    n, mp_spec, ms_spec = _two_group_specs(mix_p, mix_s, tm)
    _, hp_spec, hs_spec = _two_group_specs(hp, hs, tm)
    n_seq = cache.shape[0] // n
    assert n_seq * n == cache.shape[0]
    c_spec = pl.BlockSpec((n_seq,) + cache.shape[1:], lambda i: (jnp.minimum(i, n - 1), 0, 0, 0))
    vec = _const_spec((1, D_MODEL))
    return pl.pallas_call(
        functools.partial(_mix_mlp_body, cache_rows=cache_rows),
        grid=(n + 1,),
        in_specs=[mp_spec, ms_spec, hp_spec, hs_spec, _const_spec(wo.shape, single=True), vec, vec,
                  _const_spec((D_MODEL, D_FF), single=True), _const_spec((D_FF, D_MODEL), single=True), vec,
                  c_spec, _const_spec(kvt.shape)],
        out_specs=[hp_spec, hs_spec, c_spec],
        out_shape=[jax.ShapeDtypeStruct(hp.shape, F32), jax.ShapeDtypeStruct(hs.shape, F32),
                   jax.ShapeDtypeStruct(cache.shape, F32)],
        compiler_params=_params("arbitrary"),
        name="mix_mlp",
    )(mix_p, mix_s, hp, hs, wo, g1, g2, w1, w2, g3, cache, kvt)


def _mlp(xp, xs, g2, w1, w2, g3, tm, to_bf16=()):
    n, p_spec, s_spec = _two_group_specs(xp, xs, tm)
    vec = _const_spec((1, D_MODEL))
    cast_in, cast_out, cast_shapes = _cast_specs(to_bf16, n, lambda i: jnp.minimum(i, n - 1))
    return pl.pallas_call(
        _mlp_body,
        grid=(n + 1,),
        in_specs=[p_spec, s_spec, vec,
                  _const_spec((D_MODEL, D_FF), single=True), _const_spec((D_FF, D_MODEL), single=True), vec]
                 + cast_in,
        out_specs=[p_spec, s_spec] + cast_out,
        out_shape=[jax.ShapeDtypeStruct(xp.shape, F32), jax.ShapeDtypeStruct(xs.shape, F32)] + cast_shapes,
        compiler_params=_params("arbitrary"),
        name="mlp",
    )(xp, xs, g2, w1, w2, g3, *[w for w, _ in to_bf16])


def _rope_slab(z, c, s_lo, s_hi):
    return z * c + pltpu.roll(z, LANES - ROT_HALF, 1) * s_lo + pltpu.roll(z, ROT_HALF, 1) * s_hi


def _attn_in_body(x_ref, g_ref, w_ref, ca_ref, sla_ref, sha_ref, cb_ref, slb_ref, shb_ref,
                  qn_ref, qd_ref, kv_ref, *, scale):
    u = _rmsnorm(x_ref[...], g_ref[...]).astype(BF16)
    z = jnp.dot(u, w_ref[...], preferred_element_type=F32)
    ca, sla, sha = ca_ref[...], sla_ref[...], sha_ref[...]
    for s in range(Q_COLS // LANES):
        sl = slice(s * LANES, (s + 1) * LANES)
        q = _rope_slab(z[:, sl], ca, sla, sha) * scale
        if s < QN_SLABS:
            qn_ref[:, sl] = q.astype(BF16)
        else:
            qd_ref[s - QN_SLABS] = q
    kv_ref[0] = _rope_slab(z[:, Q_COLS:Q_COLS + LANES], ca, sla, sha)
    kv_ref[1] = z[:, Q_COLS + LANES:Q_COLS + 2 * LANES]
    cb, slb, shb = cb_ref[...], slb_ref[...], shb_ref[...]
    for s in range(2, KV_SLABS):
        sl = slice(Q_COLS + s * LANES, Q_COLS + (s + 1) * LANES)
        kv_ref[s] = _rope_slab(z[:, sl], cb, slb, shb)


QN_SLABS = (SWA_HEADS + DIL_HEADS) * HEAD_DIM // LANES
QD_SLABS = Q_COLS // LANES - QN_SLABS
KV_SLABS = KV_COLS // LANES


def _attn_in(x, g, w, tabs, tm, scale):
    t = x.shape[0]
    row = lambda i: (i, 0)
    slab = lambda i: (0, i, 0)
    nper = tabs[0].shape[0] // tm
    tab = pl.BlockSpec((tm, LANES), lambda i: (i % nper, 0))
    return pl.pallas_call(
        functools.partial(_attn_in_body, scale=scale),
        grid=(t // tm,),
        in_specs=[pl.BlockSpec((tm, D_MODEL), row), _const_spec((1, D_MODEL)),
                  _const_spec((D_MODEL, ATTN_COLS), single=True)] + [tab] * 6,
        out_specs=[pl.BlockSpec((tm, QN_SLABS * LANES), row), pl.BlockSpec((QD_SLABS, tm, LANES), slab),
                   pl.BlockSpec((KV_SLABS, tm, LANES), slab)],
        out_shape=[jax.ShapeDtypeStruct((t, QN_SLABS * LANES), BF16),
                   jax.ShapeDtypeStruct((QD_SLABS, t, LANES), F32),
                   jax.ShapeDtypeStruct((KV_SLABS, t, LANES), F32)],
        compiler_params=_params("arbitrary"),
        name="attn_in",
    )(x, g, w, *tabs)


LOG2E = math.log2(math.e)
ATTN_STEP = 8 * BLK
ATTN_LOOKAHEAD = 2
_NT = (((1,), (1,)), ((), ()))
_TN = (((0,), (0,)), ((), ()))


def _both_halves(x, in_hi):
    lane = lax.broadcasted_iota(jnp.int32, x.shape, 1)
    return jnp.where(lane >= HEAD_DIM if in_hi else lane < HEAD_DIM, x, pltpu.roll(x, HEAD_DIM, 1))


def _block_scores(q_slabs, k_tiles, k_hi, bias):
    in_a = lax.broadcasted_iota(jnp.int32, (BLK, LANES), 1) < HEAD_DIM
    kk = jnp.concatenate([_both_halves(t, k_hi) for t in k_tiles], axis=0).astype(BF16)
    zero = jnp.zeros((BLK, LANES), BF16)
    scores = []
    for qs in q_slabs:
        qq = jnp.concatenate([jnp.where(in_a, qs, zero), jnp.where(in_a, zero, qs)], axis=0)
        scores.append(lax.dot_general(kk, qq, _NT, preferred_element_type=F32) + bias)
    return scores


def _block_outputs(scores, v_tiles, v_hi, sinks):
    lane = lax.broadcasted_iota(jnp.int32, (BLK, LANES), 1)
    one_lane = 0 if v_hi else HEAD_DIM
    vv = jnp.concatenate([jnp.where(lane == one_lane, 1.0, t) for t in v_tiles], axis=0).astype(BF16)
    probs, maxes = [], []
    for slab, sh in enumerate(scores):
        m = jnp.max(sh, axis=0, keepdims=True)
        if sinks is not None:
            m = jnp.maximum(m, sinks[slab])
        probs.append(jnp.exp2(sh - m).astype(BF16))
        maxes.append(m)
    v0 = HEAD_DIM if v_hi else 0
    outs = []
    for slab, p in enumerate(probs):
        ot = lax.dot_general(vv, p, _TN, preferred_element_type=F32)
        m = maxes[slab]
        den = ot[one_lane:one_lane + 1, :]
        if sinks is not None:
            den = den + jnp.exp2(sinks[slab] - m)
        o_t = jnp.concatenate([ot[v0:v0 + HEAD_DIM, 0:BLK], ot[v0:v0 + HEAD_DIM, BLK:2 * BLK]], axis=0)
        tile = lambda row: jnp.concatenate([jnp.broadcast_to(row[:, 0:BLK], (HEAD_DIM, BLK)),
                                            jnp.broadcast_to(row[:, BLK:2 * BLK], (HEAD_DIM, BLK))], axis=0)
        outs.append(((o_t / tile(den)).T, tile(m + jnp.log2(den)).T))
    return outs


N_RES = 4


def _attn_seq_body(qn_ref, qd_ref, kv_ref, sink_ref, mix_ref, qc1, kc1, qc2, kc2, res):
    key2 = lax.broadcasted_iota(jnp.int32, (2 * BLK, 2 * BLK), 0)
    qry2 = lax.broadcasted_iota(jnp.int32, (2 * BLK, 2 * BLK), 1) & (BLK - 1)
    slack = jnp.where(key2 < BLK, key2 - qry2, qry2 - key2 + BLK)
    in_prev = jnp.where(key2 < BLK, 4 * BLK, 0)
    neg_inf = jnp.float32(-jnp.inf)
    bias_both = jnp.where(slack >= 0, 0.0, neg_inf)
    bias_cur = bias_both[BLK:2 * BLK]

    for gi, (qc, kc) in ((1, (qc1, kc1)), (2, (qc2, kc2))):
        dil = DIL_PATTERNS[gi][1]
        n = SEQ // dil
        for r in range(dil):
            dst = slice(r * n, (r + 1) * n)
            for s in range(2):
                qc[dst, s * LANES:(s + 1) * LANES] = qd_ref[2 * (gi - 1) + s, pl.ds(r, n, stride=dil), :].astype(BF16)
            kc[dst, :] = kv_ref[2 + gi, pl.ds(r, n, stride=dil), :]

    def run(q_src, q_col0, n_slabs, k_src, v_src, n_kv, packed, chain, sinks, write, class_blocks=None):
        step_rows = ATTN_STEP
        slabs_per_kv = n_slabs // n_kv

        def step(it, carry):
            base = pl.multiple_of(it * step_rows, step_rows)
            items = [(j, kvh) for j in range(step_rows // BLK) for kvh in range(n_kv)]

            def key_rows(j):
                r0 = pl.multiple_of(base + j * BLK, BLK)
                if chain == "block" or (chain == "class" and j % class_blocks == 0):
                    return r0, [pl.ds(r0, BLK)], bias_cur
                if chain == "seq" and j == 0:
                    prev = pl.ds(pl.multiple_of(jnp.maximum(r0 - BLK, 0), BLK), BLK)
                    gone = in_prev * jnp.where(it > 0, 0, 1)
                    return r0, [prev, pl.ds(r0, BLK)], jnp.where(slack - gone >= 0, 0.0, neg_inf)
                return r0, [pl.ds(pl.multiple_of(r0 - BLK, BLK), BLK), pl.ds(r0, BLK)], bias_both

            def scores_of(item):
                j, kvh = item
                r0, krows, bias = key_rows(j)
                slabs = range(kvh * slabs_per_kv, (kvh + 1) * slabs_per_kv)
                q_slabs = [q_src[pl.ds(r0, BLK), q_col0 + s * LANES:q_col0 + (s + 1) * LANES] for s in slabs]
                k_hi = (not packed) and kvh == 1
                return _block_scores(q_slabs, [k_src[r, :] for r in krows], k_hi, bias)

            def finish(item, scores):
                j, kvh = item
                r0, krows, _ = key_rows(j)
                slabs = range(kvh * slabs_per_kv, (kvh + 1) * slabs_per_kv)
                sk = None if sinks is None else [sinks[s] for s in slabs]
                v_hi = packed or kvh == 1
                outs = _block_outputs(scores, [v_src[r, :] for r in krows], v_hi, sk)
                for s, (o, lse) in zip(slabs, outs):
                    write(it, j, r0, s, o, lse)

            ahead = min(ATTN_LOOKAHEAD, len(items))
            pending = [scores_of(item) for item in items[:ahead]]
            for n, item in enumerate(items):
                if n + ahead < len(items):
                    pending.append(scores_of(items[n + ahead]))
                finish(item, pending.pop(0))
            return carry

        lax.fori_loop(0, SEQ // step_rows, step, 0)

    sinks = [jnp.concatenate([jnp.broadcast_to(sink_ref[0:1, 2 * s + half:2 * s + half + 1] * LOG2E, (1, BLK))
                              for half in range(2)], axis=1) for s in range(SWA_HEADS // 2)]

    def write_swa(it, j, r0, s, o, lse):
        mix_ref[pl.ds(r0, BLK), s * LANES:(s + 1) * LANES] = o.astype(BF16)

    run(qn_ref, 0, SWA_HEADS // 2, kv_ref.at[0], kv_ref.at[1], SWA_KV, False, "seq", sinks, write_swa)

    def write_res(group, start_of, stride):
        def write(it, j, r0, s, o, lse):
            dst = pl.ds(start_of(it, j, r0), BLK) if stride == 1 else pl.ds(start_of(it, j, r0), BLK, stride=stride)
            res[group * N_RES + s, dst, :] = o
            res[group * N_RES + 2 + s, dst, :] = lse
        return write

    n_dil_slabs = DIL_HEADS // 2
    run(qn_ref, SWA_HEADS * HEAD_DIM, n_dil_slabs, kv_ref.at[2], kv_ref.at[2], 1, True, "seq", None,
        write_res(0, lambda it, j, r0: r0, 1))
    d1, d2 = DIL_PATTERNS[1][1], DIL_PATTERNS[2][1]
    cb1 = SEQ // d1 // BLK
    per_step = ATTN_STEP // BLK // cb1
    run(qc1, 0, n_dil_slabs, kc1, kc1, 1, True, "class", None,
        write_res(1, lambda it, j, r0: d1 * (j % cb1) * BLK + it * per_step + j // cb1, d1), class_blocks=cb1)
    run(qc2, 0, n_dil_slabs, kc2, kc2, 1, True, "block", None,
        write_res(2, lambda it, j, r0: it * (ATTN_STEP // BLK) + j, d2))

    def merge(c, carry):
        rows = pl.ds(pl.multiple_of(c * ATTN_STEP, ATTN_STEP), ATTN_STEP)
        for s in range(n_dil_slabs):
            lse = [res[g * N_RES + 2 + s, rows, :] for g in range(3)]
            m = jnp.maximum(jnp.maximum(lse[0], lse[1]), lse[2])
            e = [jnp.exp2(l - m) for l in lse]
            o = e[0] * res[s, rows, :] + e[1] * res[N_RES + s, rows, :] + e[2] * res[2 * N_RES + s, rows, :]
            col = SWA_HEADS * HEAD_DIM + s * LANES
            mix_ref[rows, col:col + LANES] = (o / (e[0] + e[1] + e[2])).astype(BF16)
        return carry

    lax.fori_loop(0, SEQ // ATTN_STEP, merge, 0)


def _attn_seq(qn, qd, kv, sinks):
    seq2 = lambda b: (b, 0)
    seq3 = lambda b: (0, b, 0)
    wd = DIL_HEADS * HEAD_DIM
    return pl.pallas_call(
        _attn_seq_body,
        grid=(BATCH,),
        in_specs=[pl.BlockSpec((SEQ, QN_SLABS * LANES), seq2), pl.BlockSpec((QD_SLABS, SEQ, LANES), seq3),
                  pl.BlockSpec((KV_SLABS, SEQ, LANES), seq3), _const_spec(sinks.shape)],
        out_specs=pl.BlockSpec((SEQ, MIX_ATTN), seq2),
        out_shape=jax.ShapeDtypeStruct((BATCH * SEQ, MIX_ATTN), BF16),
        scratch_shapes=[pltpu.VMEM((SEQ, wd), BF16), pltpu.VMEM((SEQ, LANES), F32),
                        pltpu.VMEM((SEQ, wd), BF16), pltpu.VMEM((SEQ, LANES), F32),
                        pltpu.VMEM((3 * N_RES, SEQ, LANES), F32)],
        compiler_params=_params("arbitrary"),
        name="attn_seq",
    )(qn, qd, kv, sinks)


DEC_TILE = 4
N_MIX_HEADS = SWA_HEADS + DIL_HEADS


def _attn_dec_body(q_ref, kv_ref, kv_all_ref, sink_ref, csw_ref, cd0_ref, cd1_ref, cd2_ref,
                   nsw_ref, nd0_ref, nd1_ref, o_ref, kvt_ref):
    step = pl.program_id(0)
    bb, hd, grp = DEC_TILE, HEAD_DIM, DIL_HEADS
    n_rows = grp * bb

    @pl.when(step == 0)
    def _():
        kvt_ref[...] = kv_all_ref[...].T

    q = q_ref[0]
    kv = kv_ref[0]
    row_seq = lax.broadcasted_iota(jnp.int32, (n_rows, bb * hd), 0) & (bb - 1)
    col_seq = lax.broadcasted_iota(jnp.int32, (n_rows, bb * hd), 1) // hd
    diag = row_seq == col_seq
    lane = lax.broadcasted_iota(jnp.int32, (1, LANES), 1)
    nt = (((1,), (1,)), ((), ()))

    def q_rows(c0):
        return jnp.concatenate([q[:, c0 + s * hd:c0 + (s + 1) * hd] for s in range(grp)], axis=0)

    def per_row(x):
        return jnp.concatenate([x] * grp, axis=0)

    def block_diag(qr):
        return jnp.where(diag, jnp.concatenate([qr] * bb, axis=1), 0.0).astype(BF16)

    def take_diag(ob):
        ob = jnp.where(diag, ob, 0.0)
        out = ob[:, 0:hd]
        for b in range(1, bb):
            out = out + ob[:, b * hd:(b + 1) * hd]
        return out

    def shift_in(x, r0):
        return _shift_in_lanes(x, kvt_ref[r0:r0 + hd, :], step * bb, bb)

    def stack(ref, idx):
        x = ref[:, idx]
        return x.reshape(bb * hd, x.shape[2])

    def unstack(x):
        return x.reshape(bb, hd, x.shape[1])

    cd_refs = (cd0_ref, cd1_ref, cd2_ref)
    nd_refs = (nd0_ref, nd1_ref, None)
    n_dil = len(DIL_PATTERNS)
    jobs = [dict(src=csw_ref, dst=nsw_ref, ki=kvh, vi=SWA_KV + kvh, q0=kvh * grp * hd,
                 kc=kvh * hd, vc=LANES + kvh * hd, dil=1) for kvh in range(SWA_KV)]
    jobs += [dict(src=cd_refs[gi], dst=nd_refs[gi], ki=0, vi=1, q0=SWA_HEADS * hd + gi * grp * hd,
                  kc=(2 + gi) * LANES, vc=(2 + gi) * LANES + hd, dil=DIL_PATTERNS[gi][1]) for gi in range(n_dil)]

    for jb in jobs:
        k = stack(jb["src"], jb["ki"])
        qr = q_rows(jb["q0"])
        s = jnp.dot(block_diag(qr), k.astype(BF16), preferred_element_type=F32)
        if jb["dil"] > 1:
            pos = lax.broadcasted_iota(jnp.int32, (1, k.shape[1]), 1)
            s = jnp.where((pos & (jb["dil"] - 1)) == 0, s, -jnp.inf)
        jb["s"] = s
        jb["s_new"] = jnp.sum(qr * per_row(kv[:, jb["kc"]:jb["kc"] + hd]), axis=1, keepdims=True)
        jb["v_new"] = per_row(kv[:, jb["vc"]:jb["vc"] + hd])
        jb["m"] = jnp.maximum(jnp.max(s, axis=1, keepdims=True), jb["s_new"])

    m_dil = jobs[SWA_KV]["m"]
    for jb in jobs[SWA_KV + 1:]:
        m_dil = jnp.maximum(m_dil, jb["m"])
    for kvh, jb in enumerate(jobs):
        if kvh < SWA_KV:
            sk = jnp.concatenate([jnp.broadcast_to(sink_ref[0:1, kvh * grp + s:kvh * grp + s + 1], (bb, 1))
                                  for s in range(grp)], axis=0)
            m = jnp.maximum(jb["m"], sk)
            jb["extra"] = jnp.exp(sk - m)
        else:
            m = m_dil
            jb["extra"] = 0.0
        p = jnp.exp(jb["s"] - m)
        jb["p_new"] = jnp.exp(jb["s_new"] - m)
        jb["den"] = jnp.sum(p, axis=1, keepdims=True) + jb["p_new"] + jb["extra"]
        jb["p"] = p.astype(BF16)

    for jb in jobs:
        v = stack(jb["src"], jb["vi"])
        pv = lax.dot_general(jb["p"], v.astype(BF16), nt, preferred_element_type=F32)
        jb["acc"] = take_diag(pv) + jb["p_new"] * jb["v_new"]

    for kvh in range(SWA_KV):
        o = jobs[kvh]["acc"] / jobs[kvh]["den"]
        for s_ in range(grp):
            o_ref[0, kvh * grp + s_] = o[s_ * bb:(s_ + 1) * bb]
    dil_jobs = jobs[SWA_KV:]
    o = sum(jb["acc"] for jb in dil_jobs[1:]) + dil_jobs[0]["acc"]
    o = o / (sum(jb["den"] for jb in dil_jobs[1:]) + dil_jobs[0]["den"])
    for s_ in range(grp):
        o_ref[0, SWA_HEADS + s_] = o[s_ * bb:(s_ + 1) * bb]

    for jb in jobs:
        if jb["dst"] is not None:
            jb["dst"][:, jb["ki"]] = unstack(shift_in(stack(jb["src"], jb["ki"]), jb["kc"]))
            jb["dst"][:, jb["vi"]] = unstack(shift_in(stack(jb["src"], jb["vi"]), jb["vc"]))


def _attn_dec(q, kv, sinks, csw, cd0, cd1, cd2):
    caches = (csw, cd0, cd1, cd2)
    n_tiles = DEC_BATCH // DEC_TILE
    q3 = q.reshape(n_tiles, DEC_TILE, Q_COLS)
    kv3 = kv.reshape(n_tiles, DEC_TILE, KV_COLS)
    tile3 = lambda w: pl.BlockSpec((1, DEC_TILE, w), lambda i: (i, 0, 0))
    cspec = lambda c: pl.BlockSpec((DEC_TILE,) + c.shape[1:], lambda i: (i, 0, 0, 0))
    o_shape = (n_tiles, N_MIX_HEADS, DEC_TILE, HEAD_DIM)
    shifted = caches[:3]
    outs = pl.pallas_call(
        _attn_dec_body,
        grid=(n_tiles,),
        in_specs=[tile3(Q_COLS), tile3(KV_COLS), _const_spec(kv.shape), _const_spec(sinks.shape)]
                 + [cspec(c) for c in caches],
        out_specs=[cspec(c) for c in shifted]
                  + [pl.BlockSpec((1,) + o_shape[1:], lambda i: (i, 0, 0, 0)),
                     _const_spec((KV_COLS, DEC_BATCH))],
        out_shape=[jax.ShapeDtypeStruct(c.shape, F32) for c in shifted]
                  + [jax.ShapeDtypeStruct(o_shape, F32), jax.ShapeDtypeStruct((KV_COLS, DEC_BATCH), F32)],
        compiler_params=_params("arbitrary"),
        name="attn_dec",
    )(q3, kv3, kv, sinks, *caches)
    mix = jnp.transpose(outs[3], (0, 2, 1, 3)).reshape(DEC_BATCH, MIX_ATTN)
    return outs[0], outs[1], outs[2], mix, outs[4]


def _attn_weight_order(w):
    hd = HEAD_DIM
    nq, nkv = SWA_HEADS * hd, SWA_KV * hd
    base = nq + 2 * nkv
    per = (DIL_HEADS + 2) * hd
    qd = [w[:, base + g * per:base + g * per + DIL_HEADS * hd] for g in range(3)]
    kvd = [w[:, base + g * per + DIL_HEADS * hd:base + (g + 1) * per] for g in range(3)]
    return jnp.concatenate([w[:, :nq]] + qd + [w[:, nq:base]] + kvd, axis=1)


def _rope_tables(pos):
    lane = np.arange(LANES)
    dim = lane % HEAD_DIM
    lo = (dim < ROT_HALF).astype(np.float32)
    hi = ((dim >= ROT_HALF) & (dim < 2 * ROT_HALF)).astype(np.float32)
    first = (lane < HEAD_DIM).astype(np.float32)
    inv = ROPE_THETA ** (-jnp.arange(ROT_HALF, dtype=F32) / ROT_HALF)
    ang = pos.astype(F32)[:, None] * jnp.tile(inv, LANES // ROT_HALF)[None, :]
    cos, sin = jnp.cos(ang), jnp.sin(ang)
    tables = []
    for heads in (np.ones(LANES, np.float32), first):
        rot = (lo + hi) * heads
        tables += [cos * rot + (1.0 - rot), -sin * (lo * heads), sin * (hi * heads)]
    return tuple(tables)


def _cache_view(c):
    b, l, two, kv, hd = c.shape
    return jnp.transpose(c, (0, 2, 3, 4, 1)).reshape(b, two * kv, hd, l)


def _cache_unview(c, kv):
    b, _, hd, l = c.shape
    return jnp.transpose(c.reshape(b, 2, kv, hd, l), (0, 4, 1, 2, 3))


def kernel(x_prompt, x_sample, state_conv_a, state_conv_b, cache_swa_kv, cache_dil0_kv, cache_dil1_kv,
           cache_dil2_kv, norm_g, w_in_conv, conv_a_w, conv_a_b, conv_a_ln_g, conv_a_ln_b, conv_b_w,
           w_out_conv, w_in_attn, attn_sinks, w_out_attn, mlp_w1, mlp_w2):
    tm = 512
    hp = x_prompt.reshape(BATCH * SEQ, D_MODEL)
    hs = x_sample.reshape(DEC_BATCH, D_MODEL)
    g = lambda layer, i: norm_g[layer, i].reshape(1, D_MODEL)

    w_in0 = w_in_conv[0].astype(BF16)
    w_out0 = w_out_conv[0].astype(BF16)
    conv_small = (conv_a_w[0], conv_a_b, conv_a_ln_g, conv_a_ln_b, conv_b_w[0])
    hp, sta_p, stb_p, w1_0, w2_0 = _conv_layer(hp, g(0, 0), w_in0, *conv_small, w_out0, g(0, 1),
                                               to_bf16=((mlp_w1, 0), (mlp_w2, 0)))
    ga_s, zb_s, gb_s = _conv_in(hs, g(0, 0), w_in0, DEC_BATCH)
    sta = jnp.transpose(state_conv_a[0], (1, 0, 2))
    sb0, sb1 = state_conv_b[0, :, 0], state_conv_b[0, :, 1]
    hs, new_sta = _conv_mix_dec(ga_s, zb_s, gb_s, hs, sta, sb0, sb1, *conv_small, w_out0, g(0, 1))

    later = ((w_in_attn, 0), (w_out_attn, 0), (mlp_w1, 1), (mlp_w2, 1))
    hp, hs, w_in1, w_out1, w1_1, w2_1 = _mlp(hp, hs, g(0, 2), w1_0, w2_0, g(0, 3), MLP_TILE, to_bf16=later)

    w_in1 = _attn_weight_order(w_in1)
    softmax_scale = HEAD_DIM ** -0.5
    qn_p, qd_p, kv_p = _attn_in(hp, g(1, 0), w_in1, _rope_tables(jnp.arange(SEQ)), tm,
                                softmax_scale * LOG2E)
    mix_p = _attn_seq(qn_p, qd_p, kv_p, attn_sinks)

    tabs_s = _rope_tables(jnp.full((DEC_BATCH,), PAST_LEN, jnp.int32))
    qn_s, qd_s, kv_s = _attn_in(hs, g(1, 0), w_in1, tabs_s, DEC_BATCH, softmax_scale)
    q_s = jnp.concatenate([qn_s.astype(F32)] + [qd_s[s] for s in range(QD_SLABS)], axis=1)
    kv_s = jnp.concatenate([kv_s[s] for s in range(KV_SLABS)], axis=1)
    caches = (cache_swa_kv[0], cache_dil0_kv[0], cache_dil1_kv[0], cache_dil2_kv[0])
    views = [_cache_view(c) for c in caches]
    nsw, nd0, nd1, mix_s, kvt_s = _attn_dec(q_s, kv_s, attn_sinks, *views)

    hp, hs, nd2 = _mix_mlp(mix_p, mix_s, hp, hs, w_out1, g(1, 1), g(1, 2), w1_1, w2_1, g(1, 3),
                           views[3], kvt_s, (KV_SLABS - 1) * LANES, MIX_MLP_TILE)

    n_a, n_b = CONV_WIDTH - 1, SC_WIDTH - 1
    kv4 = kv_p.reshape(KV_SLABS, BATCH, SEQ, LANES)
    swa_p = jnp.stack([kv4[0, :, SEQ - BLK:], kv4[1, :, SEQ - BLK:]], axis=2)
    swa_p = swa_p.reshape(BATCH, BLK, 2, SWA_KV, HEAD_DIM)
    dil_p = [kv4[2 + gi, :, SEQ - min(w, SEQ):].reshape(BATCH, min(w, SEQ), 2, 1, HEAD_DIM)
             for gi, (w, _) in enumerate(DIL_PATTERNS)]
    return (hp.reshape(BATCH, SEQ, D_MODEL), hs.reshape(DEC_BATCH, 1, D_MODEL),
            sta_p[None, :, HALO_A - n_a:],
            jnp.transpose(new_sta, (1, 0, 2))[None],
            stb_p[None, :, HALO_B - n_b:],
            jnp.stack([sb1, zb_s], axis=1)[None],
            swa_p[None], _cache_unview(nsw, SWA_KV)[None],
            dil_p[0][None], _cache_unview(nd0, 1)[None],
            dil_p[1][None], _cache_unview(nd1, 1)[None],
            dil_p[2][None], _cache_unview(nd2, 1)[None])
```

```python
import functools
import math

import jax
import jax.numpy as jnp
import numpy as np
from jax import lax
from jax.experimental import pallas as pl
from jax.experimental.pallas import tpu as pltpu

F32 = jnp.float32
BF16 = jnp.bfloat16

D_MODEL = 1024
BATCH = 8
SEQ = 2048
DEC_BATCH = 128
PAST_LEN = 8192
HEAD_DIM = 64
ROT_HALF = 8
ROPE_THETA = 500000.0
D_FF = 4 * D_MODEL
EPS = 1e-6
CONV_CH = 512
CONV_WIDTH = 31
SC_WIDTH = 3
SWA_HEADS = 8
SWA_KV = 2
DIL_HEADS = 4
DIL_PATTERNS = ((128, 1), (512, 4), (2048, 16))
Q_COLS = (SWA_HEADS + 3 * DIL_HEADS) * HEAD_DIM
KV_COLS = 2 * SWA_KV * HEAD_DIM + 3 * 2 * HEAD_DIM
ATTN_COLS = Q_COLS + KV_COLS
MIX_ATTN = (SWA_HEADS + DIL_HEADS) * HEAD_DIM
LANES = 128
BLK = 128

V7X_VMEM_BYTES = 64 * 1024 * 1024
VMEM_LIMIT = V7X_VMEM_BYTES - 8 * 1024 * 1024


def _params(*sem):
    return pltpu.CompilerParams(dimension_semantics=sem, vmem_limit_bytes=VMEM_LIMIT)


def _const_spec(shape, single=False):
    zeros = (0,) * len(shape)
    if single:
        return pl.BlockSpec(shape, lambda *_: zeros, pipeline_mode=pl.Buffered(1))
    return pl.BlockSpec(shape, lambda *_: zeros)


def _rmsnorm(x, g):
    return x * lax.rsqrt(jnp.mean(x * x, axis=-1, keepdims=True) + EPS) * g


def _conv_in_body(x_ref, g_ref, w_ref, ga_ref, zb_ref, gb_ref):
    u = _rmsnorm(x_ref[...], g_ref[...]).astype(BF16)
    z = jnp.dot(u, w_ref[...], preferred_element_type=F32)
    c = CONV_CH
    ga_ref[...] = z[:, 0:c] * jax.nn.sigmoid(z[:, c:2 * c])
    zb_ref[...] = z[:, 4 * c:5 * c] * z[:, 2 * c:3 * c]
    gb_ref[...] = z[:, 3 * c:4 * c]


def _conv_in(x, g, w, tm):
    t = x.shape[0]
    row = lambda i: (i, 0)
    out = jax.ShapeDtypeStruct((t, CONV_CH), F32)
    return pl.pallas_call(
        _conv_in_body,
        grid=(t // tm,),
        in_specs=[pl.BlockSpec((tm, D_MODEL), row), _const_spec((1, D_MODEL)),
                  _const_spec((D_MODEL, 5 * CONV_CH))],
        out_specs=[pl.BlockSpec((tm, CONV_CH), row)] * 3,
        out_shape=[out] * 3,
        compiler_params=_params("arbitrary"),
        name="conv_in",
    )(x, g, w)


HALO_A = 32
HALO_B = 8
CONV_CHUNK = 128


def _layernorm_silu(c, g, b):
    mu = jnp.mean(c, axis=-1, keepdims=True)
    d = c - mu
    var = jnp.mean(d * d, axis=-1, keepdims=True)
    y = d * lax.rsqrt(var + EPS) * g + b
    return y * jax.nn.sigmoid(y)


SUBLANES = 8
CONV_TILE = 1024
CONV_SUB = 256


def _conv_a_slab(ext_a, aw_ref, r0, s):
    cols = slice(s * LANES, (s + 1) * LANES)
    out = None
    for b in range(SUBLANES):
        rows = CONV_CHUNK if b == 0 else CONV_CHUNK + SUBLANES
        yb = None
        for a in range((CONV_WIDTH + 1) // SUBLANES + 1):
            j = SUBLANES * a + b - (HALO_A - (CONV_WIDTH - 1))
            if 0 <= j < CONV_WIDTH:
                term = aw_ref[j:j + 1, cols] * ext_a[r0 + SUBLANES * a:r0 + SUBLANES * a + rows, cols]
                yb = term if yb is None else yb + term
        yb = yb[b:b + CONV_CHUNK]
        out = yb if out is None else out + yb
    return out


def _conv_layer_body(*refs, n_cast):
    x_ref, g0_ref, win_ref, aw_ref, ab_ref, lng_ref, lnb_ref, bw_ref, wout_ref, g1_ref = refs[:10]
    cast_in = refs[10:10 + n_cast]
    o_ref, sta_ref, stb_ref = refs[10 + n_cast:13 + n_cast]
    cast_out = refs[13 + n_cast:13 + 2 * n_cast]
    win_bf, wout_bf = refs[13 + 2 * n_cast:15 + 2 * n_cast]
    a_scr, b_scr, gb_scr, mix_scr, u_scr, conv_scr = refs[15 + 2 * n_cast:]
    for src, dst in zip(cast_in, cast_out):
        dst[...] = src[...].astype(BF16)
    t = pl.program_id(1)

    @pl.when(jnp.logical_and(pl.program_id(0) == 0, t == 0))
    def _():
        win_bf[...] = win_ref[...].astype(BF16)
        wout_bf[...] = wout_ref[...].astype(BF16)
    sub, c = CONV_SUB, CONV_CH
    n_sub = CONV_TILE // sub

    @pl.when(t == 0)
    def _():
        a_scr[0, 0:HALO_A, :] = jnp.zeros((HALO_A, c), F32)
        b_scr[0, 0:HALO_B, :] = jnp.zeros((HALO_B, c), F32)

    def row_block(k):
        return pl.ds(k * sub, sub) if isinstance(k, int) else pl.ds(pl.multiple_of(k * sub, sub), sub)

    def project_steps(k):
        rows = row_block(k)
        nxt = (k + 1) % n_sub if isinstance(k, int) else jnp.where(k + 1 == n_sub, 0, k + 1)
        proj = lambda lo, hi: jnp.dot(u_scr[...], win_bf[:, lo * c:hi * c], preferred_element_type=F32)

        def norm():
            u_scr[...] = _rmsnorm(x_ref[rows, :], g0_ref[...]).astype(BF16)

        def mixer_a():
            za = proj(0, 2)
            ga = za[:, 0:c] * jax.nn.sigmoid(za[:, c:2 * c])
            a_scr[k, HALO_A:HALO_A + sub, :] = ga
            a_scr[nxt, 0:HALO_A, :] = ga[sub - HALO_A:sub]

        def mixer_b_in():
            zb = proj(4, 5) * proj(2, 3)
            b_scr[k, HALO_B:HALO_B + sub, :] = zb
            b_scr[nxt, 0:HALO_B, :] = zb[sub - HALO_B:sub]

        def mixer_b_gate():
            gb_scr[k] = proj(3, 4)

        return [norm, mixer_a, mixer_b_in, mixer_b_gate]

    def mix_steps(k):
        rows = row_block(k)
        a_buf, b_buf, gb_buf, mix_buf = a_scr.at[k], b_scr.at[k], gb_scr.at[k], mix_scr.at[k]
        off_b = HALO_B - (SC_WIDTH - 1)
        steps = []
        for r0 in range(0, sub, CONV_CHUNK):
            chunk = slice(r0, r0 + CONV_CHUNK)
            for s in range(c // LANES):
                def conv_slab(r0=r0, s=s, chunk=chunk):
                    conv_scr[chunk, s * LANES:(s + 1) * LANES] = _conv_a_slab(a_buf, aw_ref, r0, s)
                steps.append(conv_slab)

            def gate(r0=r0, chunk=chunk):
                ya = _layernorm_silu(conv_scr[chunk, :] + ab_ref[...], lng_ref[...], lnb_ref[...])
                cb = bw_ref[0:1, :] * b_buf[off_b + r0:off_b + r0 + CONV_CHUNK, :]
                for j in range(1, SC_WIDTH):
                    cb = cb + bw_ref[j:j + 1, :] * b_buf[off_b + r0 + j:off_b + r0 + j + CONV_CHUNK, :]
                yb = gb_buf[chunk, :] * cb
                mix_buf[chunk, :] = jnp.concatenate([ya, yb], axis=-1).astype(BF16)
            steps.append(gate)

        def out_proj():
            y = jnp.dot(mix_buf[...], wout_bf[...], preferred_element_type=F32)
            o_ref[rows, :] = x_ref[rows, :] + _rmsnorm(y, g1_ref[...])
        steps.append(out_proj)
        return steps

    def interleave(matmul_steps, vector_steps):
        per = -(-len(vector_steps) // max(len(matmul_steps), 1))
        while matmul_steps or vector_steps:
            if matmul_steps:
                matmul_steps.pop(0)()
            for _ in range(per):
                if vector_steps:
                    vector_steps.pop(0)()

    interleave(project_steps(0), [])
    for k in range(n_sub - 1):
        interleave(project_steps(k + 1), mix_steps(k))
    interleave([], mix_steps(n_sub - 1))

    sta_ref[0] = a_scr[0, 0:HALO_A, :]
    stb_ref[0] = b_scr[0, 0:HALO_B, :]


def _conv_layer(x, g0, w_in, aw, ab, lng, lnb, bw, w_out, g1, to_bf16=()):
    nt = SEQ // CONV_TILE
    n_sub = CONV_TILE // CONV_SUB
    row = lambda b, t: (b * nt + t, 0)
    seq = lambda b, t: (b, 0, 0)
    cast_in, cast_out, cast_shapes = _cast_specs(to_bf16, BATCH * nt, lambda b, t: b * nt + t)
    return pl.pallas_call(
        functools.partial(_conv_layer_body, n_cast=len(to_bf16)),
        grid=(BATCH, nt),
        in_specs=[pl.BlockSpec((CONV_TILE, D_MODEL), row), _const_spec((1, D_MODEL)),
                  _const_spec((D_MODEL, 5 * CONV_CH), single=True),
                  _const_spec((CONV_WIDTH, CONV_CH)), _const_spec((1, CONV_CH)),
                  _const_spec((1, CONV_CH)), _const_spec((1, CONV_CH)),
                  _const_spec((SC_WIDTH, CONV_CH)), _const_spec((D_MODEL, D_MODEL), single=True),
                  _const_spec((1, D_MODEL))] + cast_in,
        out_specs=[pl.BlockSpec((CONV_TILE, D_MODEL), row),
                   pl.BlockSpec((1, HALO_A, CONV_CH), seq), pl.BlockSpec((1, HALO_B, CONV_CH), seq)] + cast_out
                  + [_const_spec(w_in.shape, single=True), _const_spec(w_out.shape, single=True)],
        out_shape=[jax.ShapeDtypeStruct(x.shape, F32),
                   jax.ShapeDtypeStruct((BATCH, HALO_A, CONV_CH), F32),
                   jax.ShapeDtypeStruct((BATCH, HALO_B, CONV_CH), F32)] + cast_shapes
                  + [jax.ShapeDtypeStruct(w_in.shape, BF16), jax.ShapeDtypeStruct(w_out.shape, BF16)],
        scratch_shapes=[pltpu.VMEM((n_sub, halo + CONV_SUB, CONV_CH), F32) for halo in (HALO_A, HALO_B, 0)]
                       + [pltpu.VMEM((n_sub, CONV_SUB, 2 * CONV_CH), BF16),
                          pltpu.VMEM((CONV_SUB, D_MODEL), BF16), pltpu.VMEM((CONV_SUB, CONV_CH), F32)],
        compiler_params=_params("arbitrary", "arbitrary"),
        name="conv_layer",
    )(x, g0, w_in, aw, ab, lng, lnb, bw, w_out, g1, *[w for w, _ in to_bf16])


def _conv_mix_dec_body(ga_ref, zb_ref, gb_ref, h_ref, sta_ref, sb0_ref, sb1_ref, aw_ref, ab_ref,
                       lng_ref, lnb_ref, bw_ref, w_ref, g_ref, o_ref, nsta_ref):
    ga = ga_ref[...]
    n_state = CONV_WIDTH - 1
    acc = aw_ref[n_state:n_state + 1, :] * ga
    for j in range(n_state):
        acc = acc + aw_ref[j:j + 1, :] * sta_ref[j]
    for j in range(n_state - 1):
        nsta_ref[j] = sta_ref[j + 1]
    nsta_ref[n_state - 1] = ga
    ya = _layernorm_silu(acc + ab_ref[...], lng_ref[...], lnb_ref[...])
    cb = bw_ref[0:1, :] * sb0_ref[...] + bw_ref[1:2, :] * sb1_ref[...] + bw_ref[2:3, :] * zb_ref[...]
    yb = gb_ref[...] * cb
    mix = jnp.concatenate([ya, yb], axis=-1).astype(BF16)
    y = jnp.dot(mix, w_ref[...], preferred_element_type=F32)
    o_ref[...] = h_ref[...] + _rmsnorm(y, g_ref[...])


def _conv_mix_dec(ga, zb, gb, h, sta, sb0, sb1, aw, ab, lng, lnb, bw, w, g):
    args = (ga, zb, gb, h, sta, sb0, sb1, aw, ab, lng, lnb, bw, w, g)
    return pl.pallas_call(
        _conv_mix_dec_body,
        grid=(1,),
        in_specs=[_const_spec(a.shape) for a in args],
        out_specs=[_const_spec(h.shape), _const_spec(sta.shape)],
        out_shape=[jax.ShapeDtypeStruct(h.shape, F32), jax.ShapeDtypeStruct(sta.shape, F32)],
        compiler_params=_params("arbitrary"),
        name="conv_mix_dec",
    )(*args)


FF_CHUNK = 1024
MLP_TILE = 1024
MIX_MLP_TILE = 512


def _cast_specs(to_bf16, n_steps, step_of):
    cast_in, cast_out, cast_shapes = [], [], []
    for w, idx in to_bf16:
        rows, cols = w.shape[1] // n_steps, w.shape[2]
        assert rows * n_steps == w.shape[1] and rows % 16 == 0
        cast_in.append(pl.BlockSpec((None, rows, cols), lambda *g, idx=idx: (idx, step_of(*g), 0)))
        cast_out.append(pl.BlockSpec((rows, cols), lambda *g: (step_of(*g), 0)))
        cast_shapes.append(jax.ShapeDtypeStruct(w.shape[1:], BF16))
    return cast_in, cast_out, cast_shapes


def _shift_in_lanes(x, new_cols, first_seq, n_seq):
    lane = lax.broadcasted_iota(jnp.int32, (1, LANES), 1)
    new = jnp.concatenate([pltpu.roll(new_cols, LANES - 1 - first_seq - b, 1) for b in range(n_seq)], axis=0)
    n_t = x.shape[1] // LANES
    rolled = [pltpu.roll(x[:, t * LANES:(t + 1) * LANES], LANES - 1, 1) for t in range(n_t)]
    tiles = [jnp.where(lane == LANES - 1, rolled[t + 1] if t + 1 < n_t else new, rolled[t])
             for t in range(n_t)]
    return tiles[0] if n_t == 1 else jnp.concatenate(tiles, axis=1)


def _mlp_block(x, g2_ref, w1_ref, w2_ref, g3_ref):
    u = _rmsnorm(x, g2_ref[...]).astype(BF16)
    acc = jnp.zeros(x.shape, F32)
    for c in range(D_FF // FF_CHUNK):
        sl = slice(c * FF_CHUNK, (c + 1) * FF_CHUNK)
        hid = jnp.dot(u, w1_ref[:, sl], preferred_element_type=F32)
        hid = jnp.square(jnp.maximum(hid, 0.0)).astype(BF16)
        acc = acc + jnp.dot(hid, w2_ref[sl, :], preferred_element_type=F32)
    return x + _rmsnorm(acc, g3_ref[...])


def _mlp_body(xp_ref, xs_ref, g2_ref, w1_ref, w2_ref, g3_ref, *rest):
    n_cast = (len(rest) - 2) // 2
    cast_in, (op_ref, os_ref), cast_out = rest[:n_cast], rest[n_cast:n_cast + 2], rest[n_cast + 2:]
    i, n = pl.program_id(0), pl.num_programs(0) - 1

    @pl.when(i < n)
    def _():
        for src, dst in zip(cast_in, cast_out):
            dst[...] = src[...].astype(BF16)
        op_ref[...] = _mlp_block(xp_ref[...], g2_ref, w1_ref, w2_ref, g3_ref)

    @pl.when(i == n)
    def _():
        os_ref[...] = _mlp_block(xs_ref[...], g2_ref, w1_ref, w2_ref, g3_ref)


def _mix_mlp_body(mp_ref, ms_ref, hp_ref, hs_ref, wo_ref, g1_ref, g2_ref, w1_ref, w2_ref, g3_ref,
                  cache_ref, kvt_ref, op_ref, os_ref, ncache_ref, *, cache_rows):
    i, n = pl.program_id(0), pl.num_programs(0) - 1
    n_seq = cache_ref.shape[0]

    def block(mix_ref, h_ref):
        y = jnp.dot(mix_ref[...].astype(BF16), wo_ref[...], preferred_element_type=F32)
        x = h_ref[...] + _rmsnorm(y, g1_ref[...])
        return _mlp_block(x, g2_ref, w1_ref, w2_ref, g3_ref)

    @pl.when(i < n)
    def _():
        for part in range(2):
            x = cache_ref[:, part]
            r0 = cache_rows + part * HEAD_DIM
            new = _shift_in_lanes(x.reshape(n_seq * HEAD_DIM, x.shape[2]), kvt_ref[r0:r0 + HEAD_DIM, :],
                                  i * n_seq, n_seq)
            ncache_ref[:, part] = new.reshape(x.shape)
        op_ref[...] = block(mp_ref, hp_ref)

    @pl.when(i == n)
    def _():
        os_ref[...] = block(ms_ref, hs_ref)


def _two_group_specs(xp, xs, tm):
    n = xp.shape[0] // tm
    prompt = pl.BlockSpec((tm, xp.shape[1]), lambda i: (jnp.minimum(i, n - 1), 0))
    return n, prompt, _const_spec(xs.shape)


def _mix_mlp(mix_p, mix_s, hp, hs, wo, g1, g2, w1, w2, g3, cache, kvt, cache_rows, tm):
    n, mp_spec, ms_spec = _two_group_specs(mix_p, mix_s, tm)
    _, hp_spec, hs_spec = _two_group_specs(hp, hs, tm)
    n_seq = cache.shape[0] // n
    assert n_seq * n == cache.shape[0]
    c_spec = pl.BlockSpec((n_seq,) + cache.shape[1:], lambda i: (jnp.minimum(i, n - 1), 0, 0, 0))
    vec = _const_spec((1, D_MODEL))
    return pl.pallas_call(
        functools.partial(_mix_mlp_body, cache_rows=cache_rows),
        grid=(n + 1,),
        in_specs=[mp_spec, ms_spec, hp_spec, hs_spec, _const_spec(wo.shape, single=True), vec, vec,
                  _const_spec((D_MODEL, D_FF), single=True), _const_spec((D_FF, D_MODEL), single=True), vec,
                  c_spec, _const_spec(kvt.shape)],
        out_specs=[hp_spec, hs_spec, c_spec],
        out_shape=[jax.ShapeDtypeStruct(hp.shape, F32), jax.ShapeDtypeStruct(hs.shape, F32),
                   jax.ShapeDtypeStruct(cache.shape, F32)],
        compiler_params=_params("arbitrary"),
        name="mix_mlp",
    )(mix_p, mix_s, hp, hs, wo, g1, g2, w1, w2, g3, cache, kvt)


def _mlp(xp, xs, g2, w1, w2, g3, tm, to_bf16=()):
    n, p_spec, s_spec = _two_group_specs(xp, xs, tm)
    vec = _const_spec((1, D_MODEL))
    cast_in, cast_out, cast_shapes = _cast_specs(to_bf16, n, lambda i: jnp.minimum(i, n - 1))
    return pl.pallas_call(
        _mlp_body,
        grid=(n + 1,),
        in_specs=[p_spec, s_spec, vec,
                  _const_spec((D_MODEL, D_FF), single=True), _const_spec((D_FF, D_MODEL), single=True), vec]
                 + cast_in,
        out_specs=[p_spec, s_spec] + cast_out,
        out_shape=[jax.ShapeDtypeStruct(xp.shape, F32), jax.ShapeDtypeStruct(xs.shape, F32)] + cast_shapes,
        compiler_params=_params("arbitrary"),
        name="mlp",
    )(xp, xs, g2, w1, w2, g3, *[w for w, _ in to_bf16])


def _rope_slab(z, c, s_lo, s_hi):
    return z * c + pltpu.roll(z, LANES - ROT_HALF, 1) * s_lo + pltpu.roll(z, ROT_HALF, 1) * s_hi


def _attn_in_body(x_ref, g_ref, w_ref, ca_ref, sla_ref, sha_ref, cb_ref, slb_ref, shb_ref,
                  qn_ref, qd_ref, kv_ref, *, scale):
    u = _rmsnorm(x_ref[...], g_ref[...]).astype(BF16)
    z = jnp.dot(u, w_ref[...], preferred_element_type=F32)
    ca, sla, sha = ca_ref[...], sla_ref[...], sha_ref[...]
    for s in range(Q_COLS // LANES):
        sl = slice(s * LANES, (s + 1) * LANES)
        q = _rope_slab(z[:, sl], ca, sla, sha) * scale
        if s < QN_SLABS:
            qn_ref[:, sl] = q.astype(BF16)
        else:
            qd_ref[s - QN_SLABS] = q
    kv_ref[0] = _rope_slab(z[:, Q_COLS:Q_COLS + LANES], ca, sla, sha)
    kv_ref[1] = z[:, Q_COLS + LANES:Q_COLS + 2 * LANES]
    cb, slb, shb = cb_ref[...], slb_ref[...], shb_ref[...]
    for s in range(2, KV_SLABS):
        sl = slice(Q_COLS + s * LANES, Q_COLS + (s + 1) * LANES)
        kv_ref[s] = _rope_slab(z[:, sl], cb, slb, shb)


QN_SLABS = (SWA_HEADS + DIL_HEADS) * HEAD_DIM // LANES
QD_SLABS = Q_COLS // LANES - QN_SLABS
KV_SLABS = KV_COLS // LANES


def _attn_in(x, g, w, tabs, tm, scale):
    t = x.shape[0]
    row = lambda i: (i, 0)
    slab = lambda i: (0, i, 0)
    nper = tabs[0].shape[0] // tm
    tab = pl.BlockSpec((tm, LANES), lambda i: (i % nper, 0))
    return pl.pallas_call(
        functools.partial(_attn_in_body, scale=scale),
        grid=(t // tm,),
        in_specs=[pl.BlockSpec((tm, D_MODEL), row), _const_spec((1, D_MODEL)),
                  _const_spec((D_MODEL, ATTN_COLS), single=True)] + [tab] * 6,
        out_specs=[pl.BlockSpec((tm, QN_SLABS * LANES), row), pl.BlockSpec((QD_SLABS, tm, LANES), slab),
                   pl.BlockSpec((KV_SLABS, tm, LANES), slab)],
        out_shape=[jax.ShapeDtypeStruct((t, QN_SLABS * LANES), BF16),
                   jax.ShapeDtypeStruct((QD_SLABS, t, LANES), F32),
                   jax.ShapeDtypeStruct((KV_SLABS, t, LANES), F32)],
        compiler_params=_params("arbitrary"),
        name="attn_in",
    )(x, g, w, *tabs)


LOG2E = math.log2(math.e)
ATTN_STEP = 8 * BLK
ATTN_LOOKAHEAD = 2
_NT = (((1,), (1,)), ((), ()))
_TN = (((0,), (0,)), ((), ()))


def _both_halves(x, in_hi):
    lane = lax.broadcasted_iota(jnp.int32, x.shape, 1)
    return jnp.where(lane >= HEAD_DIM if in_hi else lane < HEAD_DIM, x, pltpu.roll(x, HEAD_DIM, 1))


def _block_scores(q_slabs, k_tiles, k_hi, bias):
    in_a = lax.broadcasted_iota(jnp.int32, (BLK, LANES), 1) < HEAD_DIM
    kk = jnp.concatenate([_both_halves(t, k_hi) for t in k_tiles], axis=0).astype(BF16)
    zero = jnp.zeros((BLK, LANES), BF16)
    scores = []
    for qs in q_slabs:
        qq = jnp.concatenate([jnp.where(in_a, qs, zero), jnp.where(in_a, zero, qs)], axis=0)
        scores.append(lax.dot_general(kk, qq, _NT, preferred_element_type=F32) + bias)
    return scores


def _block_outputs(scores, v_tiles, v_hi, sinks):
    lane = lax.broadcasted_iota(jnp.int32, (BLK, LANES), 1)
    one_lane = 0 if v_hi else HEAD_DIM
    vv = jnp.concatenate([jnp.where(lane == one_lane, 1.0, t) for t in v_tiles], axis=0).astype(BF16)
    probs, maxes = [], []
    for slab, sh in enumerate(scores):
        m = jnp.max(sh, axis=0, keepdims=True)
        if sinks is not None:
            m = jnp.maximum(m, sinks[slab])
        probs.append(jnp.exp2(sh - m).astype(BF16))
        maxes.append(m)
    v0 = HEAD_DIM if v_hi else 0
    outs = []
    for slab, p in enumerate(probs):
        ot = lax.dot_general(vv, p, _TN, preferred_element_type=F32)
        m = maxes[slab]
        den = ot[one_lane:one_lane + 1, :]
        if sinks is not None:
            den = den + jnp.exp2(sinks[slab] - m)
        o_t = jnp.concatenate([ot[v0:v0 + HEAD_DIM, 0:BLK], ot[v0:v0 + HEAD_DIM, BLK:2 * BLK]], axis=0)
        tile = lambda row: jnp.concatenate([jnp.broadcast_to(row[:, 0:BLK], (HEAD_DIM, BLK)),
                                            jnp.broadcast_to(row[:, BLK:2 * BLK], (HEAD_DIM, BLK))], axis=0)
        outs.append(((o_t / tile(den)).T, tile(m + jnp.log2(den)).T))
    return outs


N_RES = 4


def _attn_seq_body(qn_ref, qd_ref, kv_ref, sink_ref, mix_ref, qc1, kc1, qc2, kc2, res):
    key2 = lax.broadcasted_iota(jnp.int32, (2 * BLK, 2 * BLK), 0)
    qry2 = lax.broadcasted_iota(jnp.int32, (2 * BLK, 2 * BLK), 1) & (BLK - 1)
    slack = jnp.where(key2 < BLK, key2 - qry2, qry2 - key2 + BLK)
    in_prev = jnp.where(key2 < BLK, 4 * BLK, 0)
    neg_inf = jnp.float32(-jnp.inf)
    bias_both = jnp.where(slack >= 0, 0.0, neg_inf)
    bias_cur = bias_both[BLK:2 * BLK]

    for gi, (qc, kc) in ((1, (qc1, kc1)), (2, (qc2, kc2))):
        dil = DIL_PATTERNS[gi][1]
        n = SEQ // dil
        for r in range(dil):
            dst = slice(r * n, (r + 1) * n)
            for s in range(2):
                qc[dst, s * LANES:(s + 1) * LANES] = qd_ref[2 * (gi - 1) + s, pl.ds(r, n, stride=dil), :].astype(BF16)
            kc[dst, :] = kv_ref[2 + gi, pl.ds(r, n, stride=dil), :]

    def run(q_src, q_col0, n_slabs, k_src, v_src, n_kv, packed, chain, sinks, write, class_blocks=None):
        step_rows = ATTN_STEP
        slabs_per_kv = n_slabs // n_kv

        def step(it, carry):
            base = pl.multiple_of(it * step_rows, step_rows)
            items = [(j, kvh) for j in range(step_rows // BLK) for kvh in range(n_kv)]

            def key_rows(j):
                r0 = pl.multiple_of(base + j * BLK, BLK)
                if chain == "block" or (chain == "class" and j % class_blocks == 0):
                    return r0, [pl.ds(r0, BLK)], bias_cur
                if chain == "seq" and j == 0:
                    prev = pl.ds(pl.multiple_of(jnp.maximum(r0 - BLK, 0), BLK), BLK)
                    gone = in_prev * jnp.where(it > 0, 0, 1)
                    return r0, [prev, pl.ds(r0, BLK)], jnp.where(slack - gone >= 0, 0.0, neg_inf)
                return r0, [pl.ds(pl.multiple_of(r0 - BLK, BLK), BLK), pl.ds(r0, BLK)], bias_both

            def scores_of(item):
                j, kvh = item
                r0, krows, bias = key_rows(j)
                slabs = range(kvh * slabs_per_kv, (kvh + 1) * slabs_per_kv)
                q_slabs = [q_src[pl.ds(r0, BLK), q_col0 + s * LANES:q_col0 + (s + 1) * LANES] for s in slabs]
                k_hi = (not packed) and kvh == 1
                return _block_scores(q_slabs, [k_src[r, :] for r in krows], k_hi, bias)

            def finish(item, scores):
                j, kvh = item
                r0, krows, _ = key_rows(j)
                slabs = range(kvh * slabs_per_kv, (kvh + 1) * slabs_per_kv)
                sk = None if sinks is None else [sinks[s] for s in slabs]
                v_hi = packed or kvh == 1
                outs = _block_outputs(scores, [v_src[r, :] for r in krows], v_hi, sk)
                for s, (o, lse) in zip(slabs, outs):
                    write(it, j, r0, s, o, lse)

            ahead = min(ATTN_LOOKAHEAD, len(items))
            pending = [scores_of(item) for item in items[:ahead]]
            for n, item in enumerate(items):
                if n + ahead < len(items):
                    pending.append(scores_of(items[n + ahead]))
                finish(item, pending.pop(0))
            return carry

        lax.fori_loop(0, SEQ // step_rows, step, 0)

    sinks = [jnp.concatenate([jnp.broadcast_to(sink_ref[0:1, 2 * s + half:2 * s + half + 1] * LOG2E, (1, BLK))
                              for half in range(2)], axis=1) for s in range(SWA_HEADS // 2)]

    def write_swa(it, j, r0, s, o, lse):
        mix_ref[pl.ds(r0, BLK), s * LANES:(s + 1) * LANES] = o.astype(BF16)

    run(qn_ref, 0, SWA_HEADS // 2, kv_ref.at[0], kv_ref.at[1], SWA_KV, False, "seq", sinks, write_swa)

    def write_res(group, start_of, stride):
        def write(it, j, r0, s, o, lse):
            dst = pl.ds(start_of(it, j, r0), BLK) if stride == 1 else pl.ds(start_of(it, j, r0), BLK, stride=stride)
            res[group * N_RES + s, dst, :] = o
            res[group * N_RES + 2 + s, dst, :] = lse
        return write

    n_dil_slabs = DIL_HEADS // 2
    run(qn_ref, SWA_HEADS * HEAD_DIM, n_dil_slabs, kv_ref.at[2], kv_ref.at[2], 1, True, "seq", None,
        write_res(0, lambda it, j, r0: r0, 1))
    d1, d2 = DIL_PATTERNS[1][1], DIL_PATTERNS[2][1]
    cb1 = SEQ // d1 // BLK
    per_step = ATTN_STEP // BLK // cb1
    run(qc1, 0, n_dil_slabs, kc1, kc1, 1, True, "class", None,
        write_res(1, lambda it, j, r0: d1 * (j % cb1) * BLK + it * per_step + j // cb1, d1), class_blocks=cb1)
    run(qc2, 0, n_dil_slabs, kc2, kc2, 1, True, "block", None,
        write_res(2, lambda it, j, r0: it * (ATTN_STEP // BLK) + j, d2))

    def merge(c, carry):
        rows = pl.ds(pl.multiple_of(c * ATTN_STEP, ATTN_STEP), ATTN_STEP)
        for s in range(n_dil_slabs):
            lse = [res[g * N_RES + 2 + s, rows, :] for g in range(3)]
            m = jnp.maximum(jnp.maximum(lse[0], lse[1]), lse[2])
            e = [jnp.exp2(l - m) for l in lse]
            o = e[0] * res[s, rows, :] + e[1] * res[N_RES + s, rows, :] + e[2] * res[2 * N_RES + s, rows, :]
            col = SWA_HEADS * HEAD_DIM + s * LANES
            mix_ref[rows, col:col + LANES] = (o / (e[0] + e[1] + e[2])).astype(BF16)
        return carry

    lax.fori_loop(0, SEQ // ATTN_STEP, merge, 0)


def _attn_seq(qn, qd, kv, sinks):
    seq2 = lambda b: (b, 0)
    seq3 = lambda b: (0, b, 0)
    wd = DIL_HEADS * HEAD_DIM
    return pl.pallas_call(
        _attn_seq_body,
        grid=(BATCH,),
        in_specs=[pl.BlockSpec((SEQ, QN_SLABS * LANES), seq2), pl.BlockSpec((QD_SLABS, SEQ, LANES), seq3),
                  pl.BlockSpec((KV_SLABS, SEQ, LANES), seq3), _const_spec(sinks.shape)],
        out_specs=pl.BlockSpec((SEQ, MIX_ATTN), seq2),
        out_shape=jax.ShapeDtypeStruct((BATCH * SEQ, MIX_ATTN), BF16),
        scratch_shapes=[pltpu.VMEM((SEQ, wd), BF16), pltpu.VMEM((SEQ, LANES), F32),
                        pltpu.VMEM((SEQ, wd), BF16), pltpu.VMEM((SEQ, LANES), F32),
                        pltpu.VMEM((3 * N_RES, SEQ, LANES), F32)],
        compiler_params=_params("arbitrary"),
        name="attn_seq",
    )(qn, qd, kv, sinks)


DEC_TILE = 4
N_MIX_HEADS = SWA_HEADS + DIL_HEADS


def _attn_dec_body(q_ref, kv_ref, kv_all_ref, sink_ref, csw_ref, cd0_ref, cd1_ref, cd2_ref,
                   nsw_ref, nd0_ref, nd1_ref, o_ref, kvt_ref):
    step = pl.program_id(0)
    bb, hd, grp = DEC_TILE, HEAD_DIM, DIL_HEADS
    n_rows = grp * bb

    @pl.when(step == 0)
    def _():
        kvt_ref[...] = kv_all_ref[...].T

    q = q_ref[0]
    kv = kv_ref[0]
    row_seq = lax.broadcasted_iota(jnp.int32, (n_rows, bb * hd), 0) & (bb - 1)
    col_seq = lax.broadcasted_iota(jnp.int32, (n_rows, bb * hd), 1) // hd
    diag = row_seq == col_seq
    lane = lax.broadcasted_iota(jnp.int32, (1, LANES), 1)
    nt = (((1,), (1,)), ((), ()))

    def q_rows(c0):
        return jnp.concatenate([q[:, c0 + s * hd:c0 + (s + 1) * hd] for s in range(grp)], axis=0)

    def per_row(x):
        return jnp.concatenate([x] * grp, axis=0)

    def block_diag(qr):
        return jnp.where(diag, jnp.concatenate([qr] * bb, axis=1), 0.0).astype(BF16)

    def take_diag(ob):
        ob = jnp.where(diag, ob, 0.0)
        out = ob[:, 0:hd]
        for b in range(1, bb):
            out = out + ob[:, b * hd:(b + 1) * hd]
        return out

    def shift_in(x, r0):
        return _shift_in_lanes(x, kvt_ref[r0:r0 + hd, :], step * bb, bb)

    def stack(ref, idx):
        x = ref[:, idx]
        return x.reshape(bb * hd, x.shape[2])

    def unstack(x):
        return x.reshape(bb, hd, x.shape[1])

    cd_refs = (cd0_ref, cd1_ref, cd2_ref)
    nd_refs = (nd0_ref, nd1_ref, None)
    n_dil = len(DIL_PATTERNS)
    jobs = [dict(src=csw_ref, dst=nsw_ref, ki=kvh, vi=SWA_KV + kvh, q0=kvh * grp * hd,
                 kc=kvh * hd, vc=LANES + kvh * hd, dil=1) for kvh in range(SWA_KV)]
    jobs += [dict(src=cd_refs[gi], dst=nd_refs[gi], ki=0, vi=1, q0=SWA_HEADS * hd + gi * grp * hd,
                  kc=(2 + gi) * LANES, vc=(2 + gi) * LANES + hd, dil=DIL_PATTERNS[gi][1]) for gi in range(n_dil)]

    for jb in jobs:
        k = stack(jb["src"], jb["ki"])
        qr = q_rows(jb["q0"])
        s = jnp.dot(block_diag(qr), k.astype(BF16), preferred_element_type=F32)
        if jb["dil"] > 1:
            pos = lax.broadcasted_iota(jnp.int32, (1, k.shape[1]), 1)
            s = jnp.where((pos & (jb["dil"] - 1)) == 0, s, -jnp.inf)
        jb["s"] = s
        jb["s_new"] = jnp.sum(qr * per_row(kv[:, jb["kc"]:jb["kc"] + hd]), axis=1, keepdims=True)
        jb["v_new"] = per_row(kv[:, jb["vc"]:jb["vc"] + hd])
        jb["m"] = jnp.maximum(jnp.max(s, axis=1, keepdims=True), jb["s_new"])

    m_dil = jobs[SWA_KV]["m"]
    for jb in jobs[SWA_KV + 1:]:
        m_dil = jnp.maximum(m_dil, jb["m"])
    for kvh, jb in enumerate(jobs):
        if kvh < SWA_KV:
            sk = jnp.concatenate([jnp.broadcast_to(sink_ref[0:1, kvh * grp + s:kvh * grp + s + 1], (bb, 1))
                                  for s in range(grp)], axis=0)
            m = jnp.maximum(jb["m"], sk)
            jb["extra"] = jnp.exp(sk - m)
        else:
            m = m_dil
            jb["extra"] = 0.0
        p = jnp.exp(jb["s"] - m)
        jb["p_new"] = jnp.exp(jb["s_new"] - m)
        jb["den"] = jnp.sum(p, axis=1, keepdims=True) + jb["p_new"] + jb["extra"]
        jb["p"] = p.astype(BF16)

    for jb in jobs:
        v = stack(jb["src"], jb["vi"])
        pv = lax.dot_general(jb["p"], v.astype(BF16), nt, preferred_element_type=F32)
        jb["acc"] = take_diag(pv) + jb["p_new"] * jb["v_new"]

    for kvh in range(SWA_KV):
        o = jobs[kvh]["acc"] / jobs[kvh]["den"]
        for s_ in range(grp):
            o_ref[0, kvh * grp + s_] = o[s_ * bb:(s_ + 1) * bb]
    dil_jobs = jobs[SWA_KV:]
    o = sum(jb["acc"] for jb in dil_jobs[1:]) + dil_jobs[0]["acc"]
    o = o / (sum(jb["den"] for jb in dil_jobs[1:]) + dil_jobs[0]["den"])
    for s_ in range(grp):
        o_ref[0, SWA_HEADS + s_] = o[s_ * bb:(s_ + 1) * bb]

    for jb in jobs:
        if jb["dst"] is not None:
            jb["dst"][:, jb["ki"]] = unstack(shift_in(stack(jb["src"], jb["ki"]), jb["kc"]))
            jb["dst"][:, jb["vi"]] = unstack(shift_in(stack(jb["src"], jb["vi"]), jb["vc"]))


def _attn_dec(q, kv, sinks, csw, cd0, cd1, cd2):
    caches = (csw, cd0, cd1, cd2)
    n_tiles = DEC_BATCH // DEC_TILE
    q3 = q.reshape(n_tiles, DEC_TILE, Q_COLS)
    kv3 = kv.reshape(n_tiles, DEC_TILE, KV_COLS)
    tile3 = lambda w: pl.BlockSpec((1, DEC_TILE, w), lambda i: (i, 0, 0))
    cspec = lambda c: pl.BlockSpec((DEC_TILE,) + c.shape[1:], lambda i: (i, 0, 0, 0))
    o_shape = (n_tiles, N_MIX_HEADS, DEC_TILE, HEAD_DIM)
    shifted = caches[:3]
    outs = pl.pallas_call(
        _attn_dec_body,
        grid=(n_tiles,),
        in_specs=[tile3(Q_COLS), tile3(KV_COLS), _const_spec(kv.shape), _const_spec(sinks.shape)]
                 + [cspec(c) for c in caches],
        out_specs=[cspec(c) for c in shifted]
                  + [pl.BlockSpec((1,) + o_shape[1:], lambda i: (i, 0, 0, 0)),
                     _const_spec((KV_COLS, DEC_BATCH))],
        out_shape=[jax.ShapeDtypeStruct(c.shape, F32) for c in shifted]
                  + [jax.ShapeDtypeStruct(o_shape, F32), jax.ShapeDtypeStruct((KV_COLS, DEC_BATCH), F32)],
        compiler_params=_params("arbitrary"),
        name="attn_dec",
    )(q3, kv3, kv, sinks, *caches)
    mix = jnp.transpose(outs[3], (0, 2, 1, 3)).reshape(DEC_BATCH, MIX_ATTN)
    return outs[0], outs[1], outs[2], mix, outs[4]


def _attn_weight_order(w):
    hd = HEAD_DIM
    nq, nkv = SWA_HEADS * hd, SWA_KV * hd
    base = nq + 2 * nkv
    per = (DIL_HEADS + 2) * hd
    qd = [w[:, base + g * per:base + g * per + DIL_HEADS * hd] for g in range(3)]
    kvd = [w[:, base + g * per + DIL_HEADS * hd:base + (g + 1) * per] for g in range(3)]
    return jnp.concatenate([w[:, :nq]] + qd + [w[:, nq:base]] + kvd, axis=1)


def _rope_tables(pos):
    lane = np.arange(LANES)
    dim = lane % HEAD_DIM
    lo = (dim < ROT_HALF).astype(np.float32)
    hi = ((dim >= ROT_HALF) & (dim < 2 * ROT_HALF)).astype(np.float32)
    first = (lane < HEAD_DIM).astype(np.float32)
    inv = ROPE_THETA ** (-jnp.arange(ROT_HALF, dtype=F32) / ROT_HALF)
    ang = pos.astype(F32)[:, None] * jnp.tile(inv, LANES // ROT_HALF)[None, :]
    cos, sin = jnp.cos(ang), jnp.sin(ang)
    tables = []
    for heads in (np.ones(LANES, np.float32), first):
        rot = (lo + hi) * heads
        tables += [cos * rot + (1.0 - rot), -sin * (lo * heads), sin * (hi * heads)]
    return tuple(tables)


def _cache_view(c):
    b, l, two, kv, hd = c.shape
    return jnp.transpose(c, (0, 2, 3, 4, 1)).reshape(b, two * kv, hd, l)


def _cache_unview(c, kv):
    b, _, hd, l = c.shape
    return jnp.transpose(c.reshape(b, 2, kv, hd, l), (0, 4, 1, 2, 3))


def kernel(x_prompt, x_sample, state_conv_a, state_conv_b, cache_swa_kv, cache_dil0_kv, cache_dil1_kv,
           cache_dil2_kv, norm_g, w_in_conv, conv_a_w, conv_a_b, conv_a_ln_g, conv_a_ln_b, conv_b_w,
           w_out_conv, w_in_attn, attn_sinks, w_out_attn, mlp_w1, mlp_w2):
    tm = 512
    hp = x_prompt.reshape(BATCH * SEQ, D_MODEL)
    hs = x_sample.reshape(DEC_BATCH, D_MODEL)
    g = lambda layer, i: norm_g[layer, i].reshape(1, D_MODEL)

    conv_small = (conv_a_w[0], conv_a_b, conv_a_ln_g, conv_a_ln_b, conv_b_w[0])
    hp, sta_p, stb_p, w1_0, w2_0, w_in0, w_out0 = _conv_layer(
        hp, g(0, 0), w_in_conv[0], *conv_small, w_out_conv[0], g(0, 1), to_bf16=((mlp_w1, 0), (mlp_w2, 0)))
    ga_s, zb_s, gb_s = _conv_in(hs, g(0, 0), w_in0, DEC_BATCH)
    sta = jnp.transpose(state_conv_a[0], (1, 0, 2))
    sb0, sb1 = state_conv_b[0, :, 0], state_conv_b[0, :, 1]
    hs, new_sta = _conv_mix_dec(ga_s, zb_s, gb_s, hs, sta, sb0, sb1, *conv_small, w_out0, g(0, 1))

    later = ((w_in_attn, 0), (w_out_attn, 0), (mlp_w1, 1), (mlp_w2, 1))
    hp, hs, w_in1, w_out1, w1_1, w2_1 = _mlp(hp, hs, g(0, 2), w1_0, w2_0, g(0, 3), MLP_TILE, to_bf16=later)

    w_in1 = _attn_weight_order(w_in1)
    softmax_scale = HEAD_DIM ** -0.5
    qn_p, qd_p, kv_p = _attn_in(hp, g(1, 0), w_in1, _rope_tables(jnp.arange(SEQ)), tm,
                                softmax_scale * LOG2E)
    mix_p = _attn_seq(qn_p, qd_p, kv_p, attn_sinks)

    tabs_s = _rope_tables(jnp.full((DEC_BATCH,), PAST_LEN, jnp.int32))
    qn_s, qd_s, kv_s = _attn_in(hs, g(1, 0), w_in1, tabs_s, DEC_BATCH, softmax_scale)
    q_s = jnp.concatenate([qn_s.astype(F32)] + [qd_s[s] for s in range(QD_SLABS)], axis=1)
    kv_s = jnp.concatenate([kv_s[s] for s in range(KV_SLABS)], axis=1)
    caches = (cache_swa_kv[0], cache_dil0_kv[0], cache_dil1_kv[0], cache_dil2_kv[0])
    views = [_cache_view(c) for c in caches]
    nsw, nd0, nd1, mix_s, kvt_s = _attn_dec(q_s, kv_s, attn_sinks, *views)

    hp, hs, nd2 = _mix_mlp(mix_p, mix_s, hp, hs, w_out1, g(1, 1), g(1, 2), w1_1, w2_1, g(1, 3),
                           views[3], kvt_s, (KV_SLABS - 1) * LANES, MIX_MLP_TILE)

    n_a, n_b = CONV_WIDTH - 1, SC_WIDTH - 1
    kv4 = kv_p.reshape(KV_SLABS, BATCH, SEQ, LANES)
    swa_p = jnp.stack([kv4[0, :, SEQ - BLK:], kv4[1, :, SEQ - BLK:]], axis=2)
    swa_p = swa_p.reshape(BATCH, BLK, 2, SWA_KV, HEAD_DIM)
    dil_p = [kv4[2 + gi, :, SEQ - min(w, SEQ):].reshape(BATCH, min(w, SEQ), 2, 1, HEAD_DIM)
             for gi, (w, _) in enumerate(DIL_PATTERNS)]
    return (hp.reshape(BATCH, SEQ, D_MODEL), hs.reshape(DEC_BATCH, 1, D_MODEL),
            sta_p[None, :, HALO_A - n_a:],
            jnp.transpose(new_sta, (1, 0, 2))[None],
            stb_p[None, :, HALO_B - n_b:],
            jnp.stack([sb1, zb_s], axis=1)[None],
            swa_p[None], _cache_unview(nsw, SWA_KV)[None],
            dil_p[0][None], _cache_unview(nd0, 1)[None],
            dil_p[1][None], _cache_unview(nd1, 1)[None],
            dil_p[2][None], _cache_unview(nd2, 1)[None])
```

```python
import functools
import math

import jax
import jax.numpy as jnp
import numpy as np
from jax import lax
from jax.experimental import pallas as pl
from jax.experimental.pallas import tpu as pltpu

F32 = jnp.float32
BF16 = jnp.bfloat16

D_MODEL = 1024
BATCH = 8
SEQ = 2048
DEC_BATCH = 128
PAST_LEN = 8192
HEAD_DIM = 64
ROT_HALF = 8
ROPE_THETA = 500000.0
D_FF = 4 * D_MODEL
EPS = 1e-6
CONV_CH = 512
CONV_WIDTH = 31
SC_WIDTH = 3
SWA_HEADS = 8
SWA_KV = 2
DIL_HEADS = 4
DIL_PATTERNS = ((128, 1), (512, 4), (2048, 16))
Q_COLS = (SWA_HEADS + 3 * DIL_HEADS) * HEAD_DIM
KV_COLS = 2 * SWA_KV * HEAD_DIM + 3 * 2 * HEAD_DIM
ATTN_COLS = Q_COLS + KV_COLS
MIX_ATTN = (SWA_HEADS + DIL_HEADS) * HEAD_DIM
LANES = 128
BLK = 128

V7X_VMEM_BYTES = 64 * 1024 * 1024
VMEM_LIMIT = V7X_VMEM_BYTES - 8 * 1024 * 1024


def _params(*sem):
    return pltpu.CompilerParams(dimension_semantics=sem, vmem_limit_bytes=VMEM_LIMIT)


def _const_spec(shape, single=False):
    zeros = (0,) * len(shape)
    if single:
        return pl.BlockSpec(shape, lambda *_: zeros, pipeline_mode=pl.Buffered(1))
    return pl.BlockSpec(shape, lambda *_: zeros)


def _rmsnorm(x, g):
    return x * lax.rsqrt(jnp.mean(x * x, axis=-1, keepdims=True) + EPS) * g


def _conv_in_body(x_ref, g_ref, w_ref, ga_ref, zb_ref, gb_ref):
    u = _rmsnorm(x_ref[...], g_ref[...]).astype(BF16)
    z = jnp.dot(u, w_ref[...], preferred_element_type=F32)
    c = CONV_CH
    ga_ref[...] = z[:, 0:c] * jax.nn.sigmoid(z[:, c:2 * c])
    zb_ref[...] = z[:, 4 * c:5 * c] * z[:, 2 * c:3 * c]
    gb_ref[...] = z[:, 3 * c:4 * c]


def _conv_in(x, g, w, tm):
    t = x.shape[0]
    row = lambda i: (i, 0)
    out = jax.ShapeDtypeStruct((t, CONV_CH), F32)
    return pl.pallas_call(
        _conv_in_body,
        grid=(t // tm,),
        in_specs=[pl.BlockSpec((tm, D_MODEL), row), _const_spec((1, D_MODEL)),
                  _const_spec((D_MODEL, 5 * CONV_CH))],
        out_specs=[pl.BlockSpec((tm, CONV_CH), row)] * 3,
        out_shape=[out] * 3,
        compiler_params=_params("arbitrary"),
        name="conv_in",
    )(x, g, w)


HALO_A = 32
HALO_B = 8
CONV_CHUNK = 128


def _layernorm_silu(c, g, b):
    mu = jnp.mean(c, axis=-1, keepdims=True)
    d = c - mu
    var = jnp.mean(d * d, axis=-1, keepdims=True)
    y = d * lax.rsqrt(var + EPS) * g + b
    return y * jax.nn.sigmoid(y)


SUBLANES = 8
CONV_TILE = 1024
CONV_SUB = 256


def _conv_a_slab(ext_a, aw_ref, r0, s):
    cols = slice(s * LANES, (s + 1) * LANES)
    out = None
    for b in range(SUBLANES):
        rows = CONV_CHUNK if b == 0 else CONV_CHUNK + SUBLANES
        yb = None
        for a in range((CONV_WIDTH + 1) // SUBLANES + 1):
            j = SUBLANES * a + b - (HALO_A - (CONV_WIDTH - 1))
            if 0 <= j < CONV_WIDTH:
                term = aw_ref[j:j + 1, cols] * ext_a[r0 + SUBLANES * a:r0 + SUBLANES * a + rows, cols]
                yb = term if yb is None else yb + term
        yb = yb[b:b + CONV_CHUNK]
        out = yb if out is None else out + yb
    return out


def _conv_layer_body(*refs, n_cast):
    x_ref, g0_ref, win_ref, aw_ref, ab_ref, lng_ref, lnb_ref, bw_ref, wout_ref, g1_ref = refs[:10]
    cast_in = refs[10:10 + n_cast]
    o_ref, sta_ref, stb_ref = refs[10 + n_cast:13 + n_cast]
    cast_out = refs[13 + n_cast:13 + 2 * n_cast]
    win_bf, wout_bf = refs[13 + 2 * n_cast:15 + 2 * n_cast]
    a_scr, b_scr, gb_scr, mix_scr, u_scr, conv_scr = refs[15 + 2 * n_cast:]
    for src, dst in zip(cast_in, cast_out):
        dst[...] = src[...].astype(BF16)
    t = pl.program_id(1)

    @pl.when(jnp.logical_and(pl.program_id(0) == 0, t == 0))
    def _():
        win_bf[...] = win_ref[...].astype(BF16)
        wout_bf[...] = wout_ref[...].astype(BF16)
    sub, c = CONV_SUB, CONV_CH
    n_sub = CONV_TILE // sub

    @pl.when(t == 0)
    def _():
        a_scr[0, 0:HALO_A, :] = jnp.zeros((HALO_A, c), F32)
        b_scr[0, 0:HALO_B, :] = jnp.zeros((HALO_B, c), F32)

    def row_block(k):
        return pl.ds(k * sub, sub) if isinstance(k, int) else pl.ds(pl.multiple_of(k * sub, sub), sub)

    def project_steps(k):
        rows = row_block(k)
        nxt = (k + 1) % n_sub if isinstance(k, int) else jnp.where(k + 1 == n_sub, 0, k + 1)
        proj = lambda lo, hi: jnp.dot(u_scr[...], win_bf[:, lo * c:hi * c], preferred_element_type=F32)

        def norm():
            u_scr[...] = _rmsnorm(x_ref[rows, :], g0_ref[...]).astype(BF16)

        def mixer_a():
            za = proj(0, 2)
            ga = za[:, 0:c] * jax.nn.sigmoid(za[:, c:2 * c])
            a_scr[k, HALO_A:HALO_A + sub, :] = ga
            a_scr[nxt, 0:HALO_A, :] = ga[sub - HALO_A:sub]

        def mixer_b_in():
            zb = proj(4, 5) * proj(2, 3)
            b_scr[k, HALO_B:HALO_B + sub, :] = zb
            b_scr[nxt, 0:HALO_B, :] = zb[sub - HALO_B:sub]

        def mixer_b_gate():
            gb_scr[k] = proj(3, 4)

        return [norm, mixer_a, mixer_b_in, mixer_b_gate]

    def mix_steps(k):
        rows = row_block(k)
        a_buf, b_buf, gb_buf, mix_buf = a_scr.at[k], b_scr.at[k], gb_scr.at[k], mix_scr.at[k]
        off_b = HALO_B - (SC_WIDTH - 1)
        steps = []
        for r0 in range(0, sub, CONV_CHUNK):
            chunk = slice(r0, r0 + CONV_CHUNK)
            for s in range(c // LANES):
                def conv_slab(r0=r0, s=s, chunk=chunk):
                    conv_scr[chunk, s * LANES:(s + 1) * LANES] = _conv_a_slab(a_buf, aw_ref, r0, s)
                steps.append(conv_slab)

            def gate(r0=r0, chunk=chunk):
                ya = _layernorm_silu(conv_scr[chunk, :] + ab_ref[...], lng_ref[...], lnb_ref[...])
                cb = bw_ref[0:1, :] * b_buf[off_b + r0:off_b + r0 + CONV_CHUNK, :]
                for j in range(1, SC_WIDTH):
                    cb = cb + bw_ref[j:j + 1, :] * b_buf[off_b + r0 + j:off_b + r0 + j + CONV_CHUNK, :]
                yb = gb_buf[chunk, :] * cb
                mix_buf[chunk, :] = jnp.concatenate([ya, yb], axis=-1).astype(BF16)
            steps.append(gate)

        def out_proj():
            y = jnp.dot(mix_buf[...], wout_bf[...], preferred_element_type=F32)
            o_ref[rows, :] = x_ref[rows, :] + _rmsnorm(y, g1_ref[...])
        steps.append(out_proj)
        return steps

    def interleave(matmul_steps, vector_steps):
        per = -(-len(vector_steps) // max(len(matmul_steps), 1))
        while matmul_steps or vector_steps:
            if matmul_steps:
                matmul_steps.pop(0)()
            for _ in range(per):
                if vector_steps:
                    vector_steps.pop(0)()

    interleave(project_steps(0), [])
    for k in range(n_sub - 1):
        interleave(project_steps(k + 1), mix_steps(k))
    interleave([], mix_steps(n_sub - 1))

    sta_ref[0] = a_scr[0, 0:HALO_A, :]
    stb_ref[0] = b_scr[0, 0:HALO_B, :]


def _conv_layer(x, g0, w_in, aw, ab, lng, lnb, bw, w_out, g1, to_bf16=()):
    nt = SEQ // CONV_TILE
    n_sub = CONV_TILE // CONV_SUB
    row = lambda b, t: (b * nt + t, 0)
    seq = lambda b, t: (b, 0, 0)
    cast_in, cast_out, cast_shapes = _cast_specs(to_bf16, BATCH * nt, lambda b, t: b * nt + t)
    return pl.pallas_call(
        functools.partial(_conv_layer_body, n_cast=len(to_bf16)),
        grid=(BATCH, nt),
        in_specs=[pl.BlockSpec((CONV_TILE, D_MODEL), row), _const_spec((1, D_MODEL)),
                  _const_spec((D_MODEL, 5 * CONV_CH), single=True),
                  _const_spec((CONV_WIDTH, CONV_CH)), _const_spec((1, CONV_CH)),
                  _const_spec((1, CONV_CH)), _const_spec((1, CONV_CH)),
                  _const_spec((SC_WIDTH, CONV_CH)), _const_spec((D_MODEL, D_MODEL), single=True),
                  _const_spec((1, D_MODEL))] + cast_in,
        out_specs=[pl.BlockSpec((CONV_TILE, D_MODEL), row),
                   pl.BlockSpec((1, HALO_A, CONV_CH), seq), pl.BlockSpec((1, HALO_B, CONV_CH), seq)] + cast_out
                  + [_const_spec(w_in.shape, single=True), _const_spec(w_out.shape, single=True)],
        out_shape=[jax.ShapeDtypeStruct(x.shape, F32),
                   jax.ShapeDtypeStruct((BATCH, HALO_A, CONV_CH), F32),
                   jax.ShapeDtypeStruct((BATCH, HALO_B, CONV_CH), F32)] + cast_shapes
                  + [jax.ShapeDtypeStruct(w_in.shape, BF16), jax.ShapeDtypeStruct(w_out.shape, BF16)],
        scratch_shapes=[pltpu.VMEM((n_sub, halo + CONV_SUB, CONV_CH), F32) for halo in (HALO_A, HALO_B, 0)]
                       + [pltpu.VMEM((n_sub, CONV_SUB, 2 * CONV_CH), BF16),
                          pltpu.VMEM((CONV_SUB, D_MODEL), BF16), pltpu.VMEM((CONV_SUB, CONV_CH), F32)],
        compiler_params=_params("arbitrary", "arbitrary"),
        name="conv_layer",
    )(x, g0, w_in, aw, ab, lng, lnb, bw, w_out, g1, *[w for w, _ in to_bf16])


def _conv_mix_dec_body(ga_ref, zb_ref, gb_ref, h_ref, sta_ref, sb0_ref, sb1_ref, aw_ref, ab_ref,
                       lng_ref, lnb_ref, bw_ref, w_ref, g_ref, o_ref, nsta_ref):
    ga = ga_ref[...]
    n_state = CONV_WIDTH - 1
    acc = aw_ref[n_state:n_state + 1, :] * ga
    for j in range(n_state):
        acc = acc + aw_ref[j:j + 1, :] * sta_ref[j]
    for j in range(n_state - 1):
        nsta_ref[j] = sta_ref[j + 1]
    nsta_ref[n_state - 1] = ga
    ya = _layernorm_silu(acc + ab_ref[...], lng_ref[...], lnb_ref[...])
    cb = bw_ref[0:1, :] * sb0_ref[...] + bw_ref[1:2, :] * sb1_ref[...] + bw_ref[2:3, :] * zb_ref[...]
    yb = gb_ref[...] * cb
    mix = jnp.concatenate([ya, yb], axis=-1).astype(BF16)
    y = jnp.dot(mix, w_ref[...], preferred_element_type=F32)
    o_ref[...] = h_ref[...] + _rmsnorm(y, g_ref[...])


def _conv_mix_dec(ga, zb, gb, h, sta, sb0, sb1, aw, ab, lng, lnb, bw, w, g):
    args = (ga, zb, gb, h, sta, sb0, sb1, aw, ab, lng, lnb, bw, w, g)
    return pl.pallas_call(
        _conv_mix_dec_body,
        grid=(1,),
        in_specs=[_const_spec(a.shape) for a in args],
        out_specs=[_const_spec(h.shape), _const_spec(sta.shape)],
        out_shape=[jax.ShapeDtypeStruct(h.shape, F32), jax.ShapeDtypeStruct(sta.shape, F32)],
        compiler_params=_params("arbitrary"),
        name="conv_mix_dec",
    )(*args)


FF_CHUNK = 1024
MLP_TILE = 1024
MIX_MLP_TILE = 512


def _cast_specs(to_bf16, n_steps, step_of):
    cast_in, cast_out, cast_shapes = [], [], []
    for w, idx in to_bf16:
        rows, cols = w.shape[1] // n_steps, w.shape[2]
        assert rows * n_steps == w.shape[1] and rows % 16 == 0
        cast_in.append(pl.BlockSpec((None, rows, cols), lambda *g, idx=idx: (idx, step_of(*g), 0)))
        cast_out.append(pl.BlockSpec((rows, cols), lambda *g: (step_of(*g), 0)))
        cast_shapes.append(jax.ShapeDtypeStruct(w.shape[1:], BF16))
    return cast_in, cast_out, cast_shapes


def _shift_in_lanes(x, new_cols, first_seq, n_seq):
    lane = lax.broadcasted_iota(jnp.int32, (1, LANES), 1)
    new = jnp.concatenate([pltpu.roll(new_cols, LANES - 1 - first_seq - b, 1) for b in range(n_seq)], axis=0)
    n_t = x.shape[1] // LANES
    rolled = [pltpu.roll(x[:, t * LANES:(t + 1) * LANES], LANES - 1, 1) for t in range(n_t)]
    tiles = [jnp.where(lane == LANES - 1, rolled[t + 1] if t + 1 < n_t else new, rolled[t])
             for t in range(n_t)]
    return tiles[0] if n_t == 1 else jnp.concatenate(tiles, axis=1)


def _mlp_block(x, g2_ref, w1_ref, w2_ref, g3_ref):
    u = _rmsnorm(x, g2_ref[...]).astype(BF16)
    acc = jnp.zeros(x.shape, F32)
    for c in range(D_FF // FF_CHUNK):
        sl = slice(c * FF_CHUNK, (c + 1) * FF_CHUNK)
        hid = jnp.dot(u, w1_ref[:, sl], preferred_element_type=F32)
        hid = jnp.square(jnp.maximum(hid, 0.0)).astype(BF16)
        acc = acc + jnp.dot(hid, w2_ref[sl, :], preferred_element_type=F32)
    return x + _rmsnorm(acc, g3_ref[...])


def _mlp_body(xp_ref, xs_ref, g2_ref, w1_ref, w2_ref, g3_ref, *rest):
    n_cast = (len(rest) - 2) // 2
    cast_in, (op_ref, os_ref), cast_out = rest[:n_cast], rest[n_cast:n_cast + 2], rest[n_cast + 2:]
    i, n = pl.program_id(0), pl.num_programs(0) - 1

    @pl.when(i < n)
    def _():
        for src, dst in zip(cast_in, cast_out):
            dst[...] = src[...].astype(BF16)
        op_ref[...] = _mlp_block(xp_ref[...], g2_ref, w1_ref, w2_ref, g3_ref)

    @pl.when(i == n)
    def _():
        os_ref[...] = _mlp_block(xs_ref[...], g2_ref, w1_ref, w2_ref, g3_ref)


def _mix_mlp_body(mp_ref, ms_ref, hp_ref, hs_ref, wo_ref, g1_ref, g2_ref, w1_ref, w2_ref, g3_ref,
                  cache_ref, kvt_ref, op_ref, os_ref, ncache_ref, *, cache_rows):
    i, n = pl.program_id(0), pl.num_programs(0) - 1
    n_seq = cache_ref.shape[0]

    def block(mix_ref, h_ref):
        y = jnp.dot(mix_ref[...].astype(BF16), wo_ref[...], preferred_element_type=F32)
        x = h_ref[...] + _rmsnorm(y, g1_ref[...])
        return _mlp_block(x, g2_ref, w1_ref, w2_ref, g3_ref)

    @pl.when(i < n)
    def _():
        for part in range(2):
            x = cache_ref[:, part]
            r0 = cache_rows + part * HEAD_DIM
            new = _shift_in_lanes(x.reshape(n_seq * HEAD_DIM, x.shape[2]), kvt_ref[r0:r0 + HEAD_DIM, :],
                                  i * n_seq, n_seq)
            ncache_ref[:, part] = new.reshape(x.shape)
        op_ref[...] = block(mp_ref, hp_ref)

    @pl.when(i == n)
    def _():
        os_ref[...] = block(ms_ref, hs_ref)


def _two_group_specs(xp, xs, tm):
    n = xp.shape[0] // tm
    prompt = pl.BlockSpec((tm, xp.shape[1]), lambda i: (jnp.minimum(i, n - 1), 0))
    return n, prompt, _const_spec(xs.shape)


def _mix_mlp(mix_p, mix_s, hp, hs, wo, g1, g2, w1, w2, g3, cache, kvt, cache_rows, tm):
    n, mp_spec, ms_spec = _two_group_specs(mix_p, mix_s, tm)
    _, hp_spec, hs_spec = _two_group_specs(hp, hs, tm)
    n_seq = cache.shape[0] // n
    assert n_seq * n == cache.shape[0]
    c_spec = pl.BlockSpec((n_seq,) + cache.shape[1:], lambda i: (jnp.minimum(i, n - 1), 0, 0, 0))
    vec = _const_spec((1, D_MODEL))
    return pl.pallas_call(
        functools.partial(_mix_mlp_body, cache_rows=cache_rows),
        grid=(n + 1,),
        in_specs=[mp_spec, ms_spec, hp_spec, hs_spec, _const_spec(wo.shape, single=True), vec, vec,
                  _const_spec((D_MODEL, D_FF), single=True), _const_spec((D_FF, D_MODEL), single=True), vec,
                  c_spec, _const_spec(kvt.shape)],
        out_specs=[hp_spec, hs_spec, c_spec],
        out_shape=[jax.ShapeDtypeStruct(hp.shape, F32), jax.ShapeDtypeStruct(hs.shape, F32),
                   jax.ShapeDtypeStruct(cache.shape, F32)],
        compiler_params=_params("arbitrary"),
        name="mix_mlp",
    )(mix_p, mix_s, hp, hs, wo, g1, g2, w1, w2, g3, cache, kvt)


def _mlp(xp, xs, g2, w1, w2, g3, tm, to_bf16=()):
    n, p_spec, s_spec = _two_group_specs(xp, xs, tm)
    vec = _const_spec((1, D_MODEL))
    cast_in, cast_out, cast_shapes = _cast_specs(to_bf16, n, lambda i: jnp.minimum(i, n - 1))
    return pl.pallas_call(
        _mlp_body,
        grid=(n + 1,),
        in_specs=[p_spec, s_spec, vec,
                  _const_spec((D_MODEL, D_FF), single=True), _const_spec((D_FF, D_MODEL), single=True), vec]
                 + cast_in,
        out_specs=[p_spec, s_spec] + cast_out,
        out_shape=[jax.ShapeDtypeStruct(xp.shape, F32), jax.ShapeDtypeStruct(xs.shape, F32)] + cast_shapes,
        compiler_params=_params("arbitrary"),
        name="mlp",
    )(xp, xs, g2, w1, w2, g3, *[w for w, _ in to_bf16])


def _rope_slab(z, c, s_lo, s_hi):
    return z * c + pltpu.roll(z, LANES - ROT_HALF, 1) * s_lo + pltpu.roll(z, ROT_HALF, 1) * s_hi


def _attn_in_body(x_ref, g_ref, w_ref, ca_ref, sla_ref, sha_ref, cb_ref, slb_ref, shb_ref,
                  qn_ref, qd_ref, kv_ref, *, scale):
    u = _rmsnorm(x_ref[...], g_ref[...]).astype(BF16)
    z = jnp.dot(u, w_ref[...], preferred_element_type=F32)
    ca, sla, sha = ca_ref[...], sla_ref[...], sha_ref[...]
    for s in range(Q_COLS // LANES):
        sl = slice(s * LANES, (s + 1) * LANES)
        q = _rope_slab(z[:, sl], ca, sla, sha) * scale
        if s < QN_SLABS:
            qn_ref[:, sl] = q.astype(BF16)
        else:
            qd_ref[s - QN_SLABS] = q
    kv_ref[0] = _rope_slab(z[:, Q_COLS:Q_COLS + LANES], ca, sla, sha)
    kv_ref[1] = z[:, Q_COLS + LANES:Q_COLS + 2 * LANES]
    cb, slb, shb = cb_ref[...], slb_ref[...], shb_ref[...]
    for s in range(2, KV_SLABS):
        sl = slice(Q_COLS + s * LANES, Q_COLS + (s + 1) * LANES)
        kv_ref[s] = _rope_slab(z[:, sl], cb, slb, shb)


QN_SLABS = (SWA_HEADS + DIL_HEADS) * HEAD_DIM // LANES
QD_SLABS = Q_COLS // LANES - QN_SLABS
KV_SLABS = KV_COLS // LANES


def _attn_in(x, g, w, tabs, tm, scale):
    t = x.shape[0]
    row = lambda i: (i, 0)
    slab = lambda i: (0, i, 0)
    nper = tabs[0].shape[0] // tm
    tab = pl.BlockSpec((tm, LANES), lambda i: (i % nper, 0))
    return pl.pallas_call(
        functools.partial(_attn_in_body, scale=scale),
        grid=(t // tm,),
        in_specs=[pl.BlockSpec((tm, D_MODEL), row), _const_spec((1, D_MODEL)),
                  _const_spec((D_MODEL, ATTN_COLS), single=True)] + [tab] * 6,
        out_specs=[pl.BlockSpec((tm, QN_SLABS * LANES), row), pl.BlockSpec((QD_SLABS, tm, LANES), slab),
                   pl.BlockSpec((KV_SLABS, tm, LANES), slab)],
        out_shape=[jax.ShapeDtypeStruct((t, QN_SLABS * LANES), BF16),
                   jax.ShapeDtypeStruct((QD_SLABS, t, LANES), F32),
                   jax.ShapeDtypeStruct((KV_SLABS, t, LANES), F32)],
        compiler_params=_params("arbitrary"),
        name="attn_in",
    )(x, g, w, *tabs)


LOG2E = math.log2(math.e)
ATTN_STEP = 8 * BLK
ATTN_LOOKAHEAD = 2
_NT = (((1,), (1,)), ((), ()))
_TN = (((0,), (0,)), ((), ()))


def _both_halves(x, in_hi):
    lane = lax.broadcasted_iota(jnp.int32, x.shape, 1)
    return jnp.where(lane >= HEAD_DIM if in_hi else lane < HEAD_DIM, x, pltpu.roll(x, HEAD_DIM, 1))


def _block_scores(q_slabs, k_tiles, k_hi, bias):
    in_a = lax.broadcasted_iota(jnp.int32, (BLK, LANES), 1) < HEAD_DIM
    kk = jnp.concatenate([_both_halves(t, k_hi) for t in k_tiles], axis=0).astype(BF16)
    zero = jnp.zeros((BLK, LANES), BF16)
    scores = []
    for qs in q_slabs:
        qq = jnp.concatenate([jnp.where(in_a, qs, zero), jnp.where(in_a, zero, qs)], axis=0)
        scores.append(lax.dot_general(kk, qq, _NT, preferred_element_type=F32) + bias)
    return scores


def _block_outputs(scores, v_tiles, v_hi, sinks):
    lane = lax.broadcasted_iota(jnp.int32, (BLK, LANES), 1)
    one_lane = 0 if v_hi else HEAD_DIM
    vv = jnp.concatenate([jnp.where(lane == one_lane, 1.0, t) for t in v_tiles], axis=0).astype(BF16)
    probs, maxes = [], []
    for slab, sh in enumerate(scores):
        m = jnp.max(sh, axis=0, keepdims=True)
        if sinks is not None:
            m = jnp.maximum(m, sinks[slab])
        probs.append(jnp.exp2(sh - m).astype(BF16))
        maxes.append(m)
    v0 = HEAD_DIM if v_hi else 0
    outs = []
    for slab, p in enumerate(probs):
        ot = lax.dot_general(vv, p, _TN, preferred_element_type=F32)
        m = maxes[slab]
        den = ot[one_lane:one_lane + 1, :]
        if sinks is not None:
            den = den + jnp.exp2(sinks[slab] - m)
        o_t = jnp.concatenate([ot[v0:v0 + HEAD_DIM, 0:BLK], ot[v0:v0 + HEAD_DIM, BLK:2 * BLK]], axis=0)
        tile = lambda row: jnp.concatenate([jnp.broadcast_to(row[:, 0:BLK], (HEAD_DIM, BLK)),
                                            jnp.broadcast_to(row[:, BLK:2 * BLK], (HEAD_DIM, BLK))], axis=0)
        outs.append(((o_t / tile(den)).T, tile(m + jnp.log2(den)).T))
    return outs


N_RES = 4


def _attn_seq_body(qn_ref, qd_ref, kv_ref, sink_ref, mix_ref, qc1, kc1, qc2, kc2, res):
    key2 = lax.broadcasted_iota(jnp.int32, (2 * BLK, 2 * BLK), 0)
    qry2 = lax.broadcasted_iota(jnp.int32, (2 * BLK, 2 * BLK), 1) & (BLK - 1)
    slack = jnp.where(key2 < BLK, key2 - qry2, qry2 - key2 + BLK)
    in_prev = jnp.where(key2 < BLK, 4 * BLK, 0)
    neg_inf = jnp.float32(-jnp.inf)
    bias_both = jnp.where(slack >= 0, 0.0, neg_inf)
    bias_cur = bias_both[BLK:2 * BLK]

    for gi, (qc, kc) in ((1, (qc1, kc1)), (2, (qc2, kc2))):
        dil = DIL_PATTERNS[gi][1]
        n = SEQ // dil
        for r in range(dil):
            dst = slice(r * n, (r + 1) * n)
            for s in range(2):
                qc[dst, s * LANES:(s + 1) * LANES] = qd_ref[2 * (gi - 1) + s, pl.ds(r, n, stride=dil), :].astype(BF16)
            kc[dst, :] = kv_ref[2 + gi, pl.ds(r, n, stride=dil), :]

    def run(q_src, q_col0, n_slabs, k_src, v_src, n_kv, packed, chain, sinks, write, class_blocks=None):
        step_rows = ATTN_STEP
        slabs_per_kv = n_slabs // n_kv

        def step(it, carry):
            base = pl.multiple_of(it * step_rows, step_rows)
            items = [(j, kvh) for j in range(step_rows // BLK) for kvh in range(n_kv)]

            def key_rows(j):
                r0 = pl.multiple_of(base + j * BLK, BLK)
                if chain == "block" or (chain == "class" and j % class_blocks == 0):
                    return r0, [pl.ds(r0, BLK)], bias_cur
                if chain == "seq" and j == 0:
                    prev = pl.ds(pl.multiple_of(jnp.maximum(r0 - BLK, 0), BLK), BLK)
                    gone = in_prev * jnp.where(it > 0, 0, 1)
                    return r0, [prev, pl.ds(r0, BLK)], jnp.where(slack - gone >= 0, 0.0, neg_inf)
                return r0, [pl.ds(pl.multiple_of(r0 - BLK, BLK), BLK), pl.ds(r0, BLK)], bias_both

            def scores_of(item):
                j, kvh = item
                r0, krows, bias = key_rows(j)
                slabs = range(kvh * slabs_per_kv, (kvh + 1) * slabs_per_kv)
                q_slabs = [q_src[pl.ds(r0, BLK), q_col0 + s * LANES:q_col0 + (s + 1) * LANES] for s in slabs]
                k_hi = (not packed) and kvh == 1
                return _block_scores(q_slabs, [k_src[r, :] for r in krows], k_hi, bias)

            def finish(item, scores):
                j, kvh = item
                r0, krows, _ = key_rows(j)
                slabs = range(kvh * slabs_per_kv, (kvh + 1) * slabs_per_kv)
                sk = None if sinks is None else [sinks[s] for s in slabs]
                v_hi = packed or kvh == 1
                outs = _block_outputs(scores, [v_src[r, :] for r in krows], v_hi, sk)
                for s, (o, lse) in zip(slabs, outs):
                    write(it, j, r0, s, o, lse)

            ahead = min(ATTN_LOOKAHEAD, len(items))
            pending = [scores_of(item) for item in items[:ahead]]
            for n, item in enumerate(items):
                if n + ahead < len(items):
                    pending.append(scores_of(items[n + ahead]))
                finish(item, pending.pop(0))
            return carry

        lax.fori_loop(0, SEQ // step_rows, step, 0)

    sinks = [jnp.concatenate([jnp.broadcast_to(sink_ref[0:1, 2 * s + half:2 * s + half + 1] * LOG2E, (1, BLK))
                              for half in range(2)], axis=1) for s in range(SWA_HEADS // 2)]

    def write_swa(it, j, r0, s, o, lse):
        mix_ref[pl.ds(r0, BLK), s * LANES:(s + 1) * LANES] = o.astype(BF16)

    run(qn_ref, 0, SWA_HEADS // 2, kv_ref.at[0], kv_ref.at[1], SWA_KV, False, "seq", sinks, write_swa)

    def write_res(group, start_of, stride):
        def write(it, j, r0, s, o, lse):
            dst = pl.ds(start_of(it, j, r0), BLK) if stride == 1 else pl.ds(start_of(it, j, r0), BLK, stride=stride)
            res[group * N_RES + s, dst, :] = o
            res[group * N_RES + 2 + s, dst, :] = lse
        return write

    n_dil_slabs = DIL_HEADS // 2
    run(qn_ref, SWA_HEADS * HEAD_DIM, n_dil_slabs, kv_ref.at[2], kv_ref.at[2], 1, True, "seq", None,
        write_res(0, lambda it, j, r0: r0, 1))
    d1, d2 = DIL_PATTERNS[1][1], DIL_PATTERNS[2][1]
    cb1 = SEQ // d1 // BLK
    per_step = ATTN_STEP // BLK // cb1
    run(qc1, 0, n_dil_slabs, kc1, kc1, 1, True, "class", None,
        write_res(1, lambda it, j, r0: d1 * (j % cb1) * BLK + it * per_step + j // cb1, d1), class_blocks=cb1)
    run(qc2, 0, n_dil_slabs, kc2, kc2, 1, True, "block", None,
        write_res(2, lambda it, j, r0: it * (ATTN_STEP // BLK) + j, d2))

    def merge(c, carry):
        rows = pl.ds(pl.multiple_of(c * ATTN_STEP, ATTN_STEP), ATTN_STEP)
        for s in range(n_dil_slabs):
            lse = [res[g * N_RES + 2 + s, rows, :] for g in range(3)]
            m = jnp.maximum(jnp.maximum(lse[0], lse[1]), lse[2])
            e = [jnp.exp2(l - m) for l in lse]
            o = e[0] * res[s, rows, :] + e[1] * res[N_RES + s, rows, :] + e[2] * res[2 * N_RES + s, rows, :]
            col = SWA_HEADS * HEAD_DIM + s * LANES
            mix_ref[rows, col:col + LANES] = (o / (e[0] + e[1] + e[2])).astype(BF16)
        return carry

    lax.fori_loop(0, SEQ // ATTN_STEP, merge, 0)


def _attn_seq(qn, qd, kv, sinks):
    seq2 = lambda b: (b, 0)
    seq3 = lambda b: (0, b, 0)
    wd = DIL_HEADS * HEAD_DIM
    return pl.pallas_call(
        _attn_seq_body,
        grid=(BATCH,),
        in_specs=[pl.BlockSpec((SEQ, QN_SLABS * LANES), seq2), pl.BlockSpec((QD_SLABS, SEQ, LANES), seq3),
                  pl.BlockSpec((KV_SLABS, SEQ, LANES), seq3), _const_spec(sinks.shape)],
        out_specs=pl.BlockSpec((SEQ, MIX_ATTN), seq2),
        out_shape=jax.ShapeDtypeStruct((BATCH * SEQ, MIX_ATTN), BF16),
        scratch_shapes=[pltpu.VMEM((SEQ, wd), BF16), pltpu.VMEM((SEQ, LANES), F32),
                        pltpu.VMEM((SEQ, wd), BF16), pltpu.VMEM((SEQ, LANES), F32),
                        pltpu.VMEM((3 * N_RES, SEQ, LANES), F32)],
        compiler_params=_params("arbitrary"),
        name="attn_seq",
    )(qn, qd, kv, sinks)


DEC_TILE = 8
N_MIX_HEADS = SWA_HEADS + DIL_HEADS


def _attn_dec_body(q_ref, kv_ref, kv_all_ref, sink_ref, csw_ref, cd0_ref, cd1_ref, cd2_ref,
                   nsw_ref, nd0_ref, nd1_ref, o_ref, kvt_ref):
    step = pl.program_id(0)
    bb, hd, grp = DEC_TILE, HEAD_DIM, DIL_HEADS
    n_rows = grp * bb

    @pl.when(step == 0)
    def _():
        kvt_ref[...] = kv_all_ref[...].T

    q = q_ref[0]
    kv = kv_ref[0]
    row_seq = lax.broadcasted_iota(jnp.int32, (n_rows, bb * hd), 0) & (bb - 1)
    col_seq = lax.broadcasted_iota(jnp.int32, (n_rows, bb * hd), 1) // hd
    diag = row_seq == col_seq
    lane = lax.broadcasted_iota(jnp.int32, (1, LANES), 1)
    nt = (((1,), (1,)), ((), ()))

    def q_rows(c0):
        return jnp.concatenate([q[:, c0 + s * hd:c0 + (s + 1) * hd] for s in range(grp)], axis=0)

    def per_row(x):
        return jnp.concatenate([x] * grp, axis=0)

    def block_diag(qr):
        return jnp.where(diag, jnp.concatenate([qr] * bb, axis=1), 0.0).astype(BF16)

    def take_diag(ob):
        ob = jnp.where(diag, ob, 0.0)
        out = ob[:, 0:hd]
        for b in range(1, bb):
            out = out + ob[:, b * hd:(b + 1) * hd]
        return out

    def shift_in(x, r0):
        return _shift_in_lanes(x, kvt_ref[r0:r0 + hd, :], step * bb, bb)

    def stack(ref, idx):
        x = ref[:, idx]
        return x.reshape(bb * hd, x.shape[2])

    def unstack(x):
        return x.reshape(bb, hd, x.shape[1])

    cd_refs = (cd0_ref, cd1_ref, cd2_ref)
    nd_refs = (nd0_ref, nd1_ref, None)
    n_dil = len(DIL_PATTERNS)
    jobs = [dict(src=csw_ref, dst=nsw_ref, ki=kvh, vi=SWA_KV + kvh, q0=kvh * grp * hd,
                 kc=kvh * hd, vc=LANES + kvh * hd, dil=1) for kvh in range(SWA_KV)]
    jobs += [dict(src=cd_refs[gi], dst=nd_refs[gi], ki=0, vi=1, q0=SWA_HEADS * hd + gi * grp * hd,
                  kc=(2 + gi) * LANES, vc=(2 + gi) * LANES + hd, dil=DIL_PATTERNS[gi][1]) for gi in range(n_dil)]

    for jb in jobs:
        k = stack(jb["src"], jb["ki"])
        qr = q_rows(jb["q0"])
        s = jnp.dot(block_diag(qr), k.astype(BF16), preferred_element_type=F32)
        if jb["dil"] > 1:
            pos = lax.broadcasted_iota(jnp.int32, (1, k.shape[1]), 1)
            s = jnp.where((pos & (jb["dil"] - 1)) == 0, s, -jnp.inf)
        jb["s"] = s
        jb["s_new"] = jnp.sum(qr * per_row(kv[:, jb["kc"]:jb["kc"] + hd]), axis=1, keepdims=True)
        jb["v_new"] = per_row(kv[:, jb["vc"]:jb["vc"] + hd])
        jb["m"] = jnp.maximum(jnp.max(s, axis=1, keepdims=True), jb["s_new"])

    m_dil = jobs[SWA_KV]["m"]
    for jb in jobs[SWA_KV + 1:]:
        m_dil = jnp.maximum(m_dil, jb["m"])
    for kvh, jb in enumerate(jobs):
        if kvh < SWA_KV:
            sk = jnp.concatenate([jnp.broadcast_to(sink_ref[0:1, kvh * grp + s:kvh * grp + s + 1], (bb, 1))
                                  for s in range(grp)], axis=0)
            m = jnp.maximum(jb["m"], sk)
            jb["extra"] = jnp.exp(sk - m)
        else:
            m = m_dil
            jb["extra"] = 0.0
        p = jnp.exp(jb["s"] - m)
        jb["p_new"] = jnp.exp(jb["s_new"] - m)
        jb["den"] = jnp.sum(p, axis=1, keepdims=True) + jb["p_new"] + jb["extra"]
        jb["p"] = p.astype(BF16)

    for jb in jobs:
        v = stack(jb["src"], jb["vi"])
        pv = lax.dot_general(jb["p"], v.astype(BF16), nt, preferred_element_type=F32)
        jb["acc"] = take_diag(pv) + jb["p_new"] * jb["v_new"]

    for kvh in range(SWA_KV):
        o = jobs[kvh]["acc"] / jobs[kvh]["den"]
        for s_ in range(grp):
            o_ref[0, kvh * grp + s_] = o[s_ * bb:(s_ + 1) * bb]
    dil_jobs = jobs[SWA_KV:]
    o = sum(jb["acc"] for jb in dil_jobs[1:]) + dil_jobs[0]["acc"]
    o = o / (sum(jb["den"] for jb in dil_jobs[1:]) + dil_jobs[0]["den"])
    for s_ in range(grp):
        o_ref[0, SWA_HEADS + s_] = o[s_ * bb:(s_ + 1) * bb]

    for jb in jobs:
        if jb["dst"] is not None:
            jb["dst"][:, jb["ki"]] = unstack(shift_in(stack(jb["src"], jb["ki"]), jb["kc"]))
            jb["dst"][:, jb["vi"]] = unstack(shift_in(stack(jb["src"], jb["vi"]), jb["vc"]))


def _attn_dec(q, kv, sinks, csw, cd0, cd1, cd2):
    caches = (csw, cd0, cd1, cd2)
    n_tiles = DEC_BATCH // DEC_TILE
    q3 = q.reshape(n_tiles, DEC_TILE, Q_COLS)
    kv3 = kv.reshape(n_tiles, DEC_TILE, KV_COLS)
    tile3 = lambda w: pl.BlockSpec((1, DEC_TILE, w), lambda i: (i, 0, 0))
    cspec = lambda c: pl.BlockSpec((DEC_TILE,) + c.shape[1:], lambda i: (i, 0, 0, 0))
    o_shape = (n_tiles, N_MIX_HEADS, DEC_TILE, HEAD_DIM)
    shifted = caches[:3]
    outs = pl.pallas_call(
        _attn_dec_body,
        grid=(n_tiles,),
        in_specs=[tile3(Q_COLS), tile3(KV_COLS), _const_spec(kv.shape), _const_spec(sinks.shape)]
                 + [cspec(c) for c in caches],
        out_specs=[cspec(c) for c in shifted]
                  + [pl.BlockSpec((1,) + o_shape[1:], lambda i: (i, 0, 0, 0)),
                     _const_spec((KV_COLS, DEC_BATCH))],
        out_shape=[jax.ShapeDtypeStruct(c.shape, F32) for c in shifted]
                  + [jax.ShapeDtypeStruct(o_shape, F32), jax.ShapeDtypeStruct((KV_COLS, DEC_BATCH), F32)],
        compiler_params=_params("arbitrary"),
        name="attn_dec",
    )(q3, kv3, kv, sinks, *caches)
    mix = jnp.transpose(outs[3], (0, 2, 1, 3)).reshape(DEC_BATCH, MIX_ATTN)
    return outs[0], outs[1], outs[2], mix, outs[4]


def _attn_weight_order(w):
    hd = HEAD_DIM
    nq, nkv = SWA_HEADS * hd, SWA_KV * hd
    base = nq + 2 * nkv
    per = (DIL_HEADS + 2) * hd
    qd = [w[:, base + g * per:base + g * per + DIL_HEADS * hd] for g in range(3)]
    kvd = [w[:, base + g * per + DIL_HEADS * hd:base + (g + 1) * per] for g in range(3)]
    return jnp.concatenate([w[:, :nq]] + qd + [w[:, nq:base]] + kvd, axis=1)


def _rope_tables(pos):
    lane = np.arange(LANES)
    dim = lane % HEAD_DIM
    lo = (dim < ROT_HALF).astype(np.float32)
    hi = ((dim >= ROT_HALF) & (dim < 2 * ROT_HALF)).astype(np.float32)
    first = (lane < HEAD_DIM).astype(np.float32)
    inv = ROPE_THETA ** (-jnp.arange(ROT_HALF, dtype=F32) / ROT_HALF)
    ang = pos.astype(F32)[:, None] * jnp.tile(inv, LANES // ROT_HALF)[None, :]
    cos, sin = jnp.cos(ang), jnp.sin(ang)
    tables = []
    for heads in (np.ones(LANES, np.float32), first):
        rot = (lo + hi) * heads
        tables += [cos * rot + (1.0 - rot), -sin * (lo * heads), sin * (hi * heads)]
    return tuple(tables)


def _cache_view(c):
    b, l, two, kv, hd = c.shape
    return jnp.transpose(c, (0, 2, 3, 4, 1)).reshape(b, two * kv, hd, l)


def _cache_unview(c, kv):
    b, _, hd, l = c.shape
    return jnp.transpose(c.reshape(b, 2, kv, hd, l), (0, 4, 1, 2, 3))


def kernel(x_prompt, x_sample, state_conv_a, state_conv_b, cache_swa_kv, cache_dil0_kv, cache_dil1_kv,
           cache_dil2_kv, norm_g, w_in_conv, conv_a_w, conv_a_b, conv_a_ln_g, conv_a_ln_b, conv_b_w,
           w_out_conv, w_in_attn, attn_sinks, w_out_attn, mlp_w1, mlp_w2):
    tm = 512
    hp = x_prompt.reshape(BATCH * SEQ, D_MODEL)
    hs = x_sample.reshape(DEC_BATCH, D_MODEL)
    g = lambda layer, i: norm_g[layer, i].reshape(1, D_MODEL)

    conv_small = (conv_a_w[0], conv_a_b, conv_a_ln_g, conv_a_ln_b, conv_b_w[0])
    hp, sta_p, stb_p, w1_0, w2_0, w_in0, w_out0 = _conv_layer(
        hp, g(0, 0), w_in_conv[0], *conv_small, w_out_conv[0], g(0, 1), to_bf16=((mlp_w1, 0), (mlp_w2, 0)))
    ga_s, zb_s, gb_s = _conv_in(hs, g(0, 0), w_in0, DEC_BATCH)
    sta = jnp.transpose(state_conv_a[0], (1, 0, 2))
    sb0, sb1 = state_conv_b[0, :, 0], state_conv_b[0, :, 1]
    hs, new_sta = _conv_mix_dec(ga_s, zb_s, gb_s, hs, sta, sb0, sb1, *conv_small, w_out0, g(0, 1))

    later = ((w_in_attn, 0), (w_out_attn, 0), (mlp_w1, 1), (mlp_w2, 1))
    hp, hs, w_in1, w_out1, w1_1, w2_1 = _mlp(hp, hs, g(0, 2), w1_0, w2_0, g(0, 3), MLP_TILE, to_bf16=later)

    w_in1 = _attn_weight_order(w_in1)
    softmax_scale = HEAD_DIM ** -0.5
    qn_p, qd_p, kv_p = _attn_in(hp, g(1, 0), w_in1, _rope_tables(jnp.arange(SEQ)), tm,
                                softmax_scale * LOG2E)
    mix_p = _attn_seq(qn_p, qd_p, kv_p, attn_sinks)

    tabs_s = _rope_tables(jnp.full((DEC_BATCH,), PAST_LEN, jnp.int32))
    qn_s, qd_s, kv_s = _attn_in(hs, g(1, 0), w_in1, tabs_s, DEC_BATCH, softmax_scale)
    q_s = jnp.concatenate([qn_s.astype(F32)] + [qd_s[s] for s in range(QD_SLABS)], axis=1)
    kv_s = jnp.concatenate([kv_s[s] for s in range(KV_SLABS)], axis=1)
    caches = (cache_swa_kv[0], cache_dil0_kv[0], cache_dil1_kv[0], cache_dil2_kv[0])
    views = [_cache_view(c) for c in caches]
    nsw, nd0, nd1, mix_s, kvt_s = _attn_dec(q_s, kv_s, attn_sinks, *views)

    hp, hs, nd2 = _mix_mlp(mix_p, mix_s, hp, hs, w_out1, g(1, 1), g(1, 2), w1_1, w2_1, g(1, 3),
                           views[3], kvt_s, (KV_SLABS - 1) * LANES, MIX_MLP_TILE)

    n_a, n_b = CONV_WIDTH - 1, SC_WIDTH - 1
    kv4 = kv_p.reshape(KV_SLABS, BATCH, SEQ, LANES)
    swa_p = jnp.stack([kv4[0, :, SEQ - BLK:], kv4[1, :, SEQ - BLK:]], axis=2)
    swa_p = swa_p.reshape(BATCH, BLK, 2, SWA_KV, HEAD_DIM)
    dil_p = [kv4[2 + gi, :, SEQ - min(w, SEQ):].reshape(BATCH, min(w, SEQ), 2, 1, HEAD_DIM)
             for gi, (w, _) in enumerate(DIL_PATTERNS)]
    return (hp.reshape(BATCH, SEQ, D_MODEL), hs.reshape(DEC_BATCH, 1, D_MODEL),
            sta_p[None, :, HALO_A - n_a:],
            jnp.transpose(new_sta, (1, 0, 2))[None],
            stb_p[None, :, HALO_B - n_b:],
            jnp.stack([sb1, zb_s], axis=1)[None],
            swa_p[None], _cache_unview(nsw, SWA_KV)[None],
            dil_p[0][None], _cache_unview(nd0, 1)[None],
            dil_p[1][None], _cache_unview(nd1, 1)[None],
            dil_p[2][None], _cache_unview(nd2, 1)[None])
```
